```python
import math
import jax, jax.numpy as jnp
from jax import lax
import numpy as np

D_MODEL = 2048
BATCH = 4
SEQ = 4096
DEPTH = 4

N_HEADS = 16
HEAD_DIM = D_MODEL // N_HEADS
D_ATTN = N_HEADS * HEAD_DIM
ROPE_THETA = 10000.0
Q_BLOCK = 128
MASK_VALUE = -1e30
LN_EPS = 1e-5
N_MIXERS = 4
DILATED_GROUPS = ((128, 1), (512, 4), (2048, 16))
B_KV_HEADS = 4
GQA_GROUP = N_HEADS // B_KV_HEADS
KV_WIDTH = B_KV_HEADS * HEAD_DIM
IDX_HEADS = 16
IDX_DIM = 64
IDX_TOPK_MAX = 256
DSA_SIZES = (D_ATTN, KV_WIDTH, KV_WIDTH, IDX_HEADS * IDX_DIM, IDX_DIM, IDX_HEADS)
FORGET_BIAS_MEAN = 3.0
MOBA_BLOCK = 256
MOBA_TOPK = 3
MOBA_Q_CHUNK = 16
N_EXPERTS = 32
TOP_K = 4
D_EXPERT = 1024
SWIGLU_ALPHA = 1.702
SWIGLU_LIMIT = 7.0
MOE_ROW_BLOCK = 128
DEEPNORM_ALPHA = (2 * DEPTH) ** 0.25
DEEPNORM_BETA = (8 * DEPTH) ** -0.25
MIXER_IN_WIDTHS = (3 * D_ATTN, sum(DSA_SIZES), 3 * D_ATTN + N_HEADS, 3 * D_ATTN)

kernel_name = 'hybrid_dilated_dsa_fox_moba_moe_deepnorm'


def layer_norm(x, g, b):
    xf = x.astype(jnp.float32)
    mu = xf.mean(-1, keepdims=True)
    var = jnp.mean(jnp.square(xf - mu), -1, keepdims=True)
    y = (xf - mu) * lax.rsqrt(var + LN_EPS) * g.astype(jnp.float32) + b.astype(jnp.float32)
    return y.astype(x.dtype)


def rope(t, pos):
    half = t.shape[-1] // 2
    inv_freq = ROPE_THETA ** (-jnp.arange(half, dtype=jnp.float32) / half)
    ang = pos.astype(jnp.float32)[:, None] * inv_freq[None, :]
    cos = jnp.cos(ang)[:, None, :]
    sin = jnp.sin(ang)[:, None, :]
    t1 = t[..., :half].astype(jnp.float32)
    t2 = t[..., half:].astype(jnp.float32)
    return jnp.concatenate([t1 * cos - t2 * sin, t2 * cos + t1 * sin], -1).astype(t.dtype)


def _split_heads(t, n):
    return t.reshape(*t.shape[:-1], n, -1)


def banded_window_attention(q, k, v, w):
    L = q.shape[-2]
    nb = -(-L // w)
    padcfg = [(0, 0)] * (q.ndim - 2) + [(0, nb * w - L), (0, 0)]
    q, k, v = (jnp.pad(t, padcfg) for t in (q, k, v))
    lead = q.shape[:-2]
    dh = q.shape[-1]
    qb = q.reshape(*lead, nb, w, dh)
    kb = k.reshape(*lead, nb, w, dh)
    vb = v.reshape(*lead, nb, w, dh)
    prevcfg = [(0, 0)] * len(lead) + [(1, 0), (0, 0), (0, 0)]
    kcat = jnp.concatenate([jnp.pad(kb, prevcfg)[..., :-1, :, :], kb], axis=-2)
    vcat = jnp.concatenate([jnp.pad(vb, prevcfg)[..., :-1, :, :], vb], axis=-2)
    s = jnp.einsum('...nqd,...nkd->...nqk', qb, kcat).astype(jnp.float32) * (dh ** -0.5)
    qi = jnp.arange(w)[:, None] + w
    ki = jnp.arange(2 * w)[None, :]
    dist = qi - ki
    key_pos = jnp.arange(nb)[:, None, None] * w + ki[None] - w
    mask = (dist >= 0) & (dist <= w) & (key_pos >= 0)
    s = jnp.where(mask, s, MASK_VALUE)
    m = s.max(-1, keepdims=True)
    p = jnp.exp(s - m)
    l = p.sum(-1, keepdims=True)
    o = jnp.einsum('...nqk,...nkd->...nqd', (p / l).astype(v.dtype), vcat)
    lse = (m + jnp.log(l))[..., 0]
    o = o.reshape(*lead, nb * w, dh)[..., :L, :]
    lse = lse.reshape(*lead, nb * w)[..., :L]
    return o, lse


def dilated_branch(q, k, v, window, dil):
    bsz, nh, seq, dh = q.shape
    sp = -(-seq // dil) * dil

    def to_sub(t):
        t = jnp.pad(t, ((0, 0), (0, 0), (0, sp - seq), (0, 0)))
        return t.reshape(bsz, nh, sp // dil, dil, dh).swapaxes(2, 3)

    o, lse = banded_window_attention(to_sub(q), to_sub(k), to_sub(v), window // dil)
    o = o.swapaxes(2, 3).reshape(bsz, nh, sp, dh)[:, :, :seq]
    lse = lse.swapaxes(2, 3).reshape(bsz, nh, sp)[:, :, :seq]
    return o, lse


def mixer_dilated(x, pos, w_in, w_out):
    bsz, seq, _ = x.shape
    q, k, v = jnp.split(x @ w_in, 3, axis=-1)
    q = rope(_split_heads(q, N_HEADS), pos).transpose(0, 2, 1, 3)
    k = rope(_split_heads(k, N_HEADS), pos).transpose(0, 2, 1, 3)
    v = _split_heads(v, N_HEADS).transpose(0, 2, 1, 3)
    outs, lses = [], []
    for window, dil in DILATED_GROUPS:
        o, lse = dilated_branch(q, k, v, window, dil)
        outs.append(o)
        lses.append(lse)
    wts = jax.nn.softmax(jnp.stack(lses), axis=0)
    o = jnp.einsum('gbhs,gbhsd->bshd', wts, jnp.stack(outs).astype(jnp.float32))
    return o.reshape(bsz, seq, D_ATTN).astype(x.dtype) @ w_out


def mixer_dsa(x, pos, w_in, idx_norm_g, idx_norm_b, w_out):
    bsz, seq, _ = x.shape
    q, k, v, qi, ki, wi = jnp.split(x @ w_in, np.cumsum(DSA_SIZES)[:-1].tolist(), axis=-1)
    q = rope(_split_heads(q, N_HEADS), pos).reshape(bsz, seq, B_KV_HEADS, GQA_GROUP, HEAD_DIM)
    k = rope(_split_heads(k, B_KV_HEADS), pos)
    v = _split_heads(v, B_KV_HEADS)
    qi = rope(_split_heads(qi, IDX_HEADS), pos)
    ki = rope(layer_norm(ki, idx_norm_g, idx_norm_b)[:, :, None, :], pos)[:, :, 0]
    wi = wi.astype(jnp.float32) * (IDX_HEADS ** -0.5 * IDX_DIM ** -0.5)
    topk = min(IDX_TOPK_MAX, seq // 4)
    nqb = seq // Q_BLOCK
    scale = HEAD_DIM ** -0.5
    key_pos = jnp.arange(seq)

    def blk(t):
        return t.reshape(bsz, nqb, Q_BLOCK, *t.shape[2:]).swapaxes(0, 1)

    def block(args):
        qb, qib, wib, j = args
        tq = j * Q_BLOCK + jnp.arange(Q_BLOCK)
        logits = jnp.einsum('bqhd,bsd->bqhs', qib, ki).astype(jnp.float32)
        score = jnp.einsum('bqh,bqhs->bqs', wib, jax.nn.relu(logits))
        score = jnp.where(key_pos[None, None, :] <= tq[None, :, None], score, -jnp.inf)
        _, sel = lax.top_k(score, topk)
        kg = jax.vmap(lambda kk, ii: kk[ii])(k, sel)
        vg = jax.vmap(lambda vv, ii: vv[ii])(v, sel)
        s = jnp.einsum('bqgrd,bqkgd->bqgrk', qb, kg).astype(jnp.float32) * scale
        ok = (sel <= tq[None, :, None])[:, :, None, None, :]
        p = jax.nn.softmax(jnp.where(ok, s, MASK_VALUE), axis=-1).astype(v.dtype)
        return jnp.einsum('bqgrk,bqkgd->bqgrd', p, vg)

    o = lax.map(block, (blk(q), blk(qi), blk(wi), jnp.arange(nqb)))
    return o.swapaxes(0, 1).reshape(bsz, seq, D_ATTN) @ w_out


def mixer_fox(x, w_in, b_forget, w_out):
    bsz, seq, _ = x.shape
    q, k, v, f_pre = jnp.split(x @ w_in, [D_ATTN, 2 * D_ATTN, 3 * D_ATTN], axis=-1)
    q, k, v = (_split_heads(t, N_HEADS).transpose(0, 2, 1, 3) for t in (q, k, v))
    log_f = jax.nn.log_sigmoid((f_pre + b_forget).astype(jnp.float32))
    cum = jnp.cumsum(log_f, axis=1).transpose(0, 2, 1)
    nqb = seq // Q_BLOCK
    q_blocks = q.reshape(bsz, N_HEADS, nqb, Q_BLOCK, HEAD_DIM).transpose(2, 0, 1, 3, 4)
    c_blocks = cum.reshape(bsz, N_HEADS, nqb, Q_BLOCK).transpose(2, 0, 1, 3)
    key_pos = jnp.arange(seq)
    scale = HEAD_DIM ** -0.5

    def block(args):
        qb, cq, j = args
        tq = j * Q_BLOCK + jnp.arange(Q_BLOCK)
        s = (jnp.einsum('bhqd,bhkd->bhqk', qb, k).astype(jnp.float32) * scale
             + cq[..., None] - cum[:, :, None, :])
        s = jnp.where(key_pos[None, :] <= tq[:, None], s, MASK_VALUE)
        p = jax.nn.softmax(s, axis=-1).astype(v.dtype)
        return jnp.einsum('bhqk,bhkd->bhqd', p, v)

    o = lax.map(block, (q_blocks, c_blocks, jnp.arange(nqb)))
    return o.transpose(1, 0, 3, 2, 4).reshape(bsz, seq, D_ATTN) @ w_out


def mixer_moba(x, pos, w_in, w_out):
    bsz, seq, _ = x.shape
    q, k, v = jnp.split(x @ w_in, 3, axis=-1)
    q = rope(_split_heads(q, N_HEADS), pos).transpose(0, 2, 1, 3)
    k = rope(_split_heads(k, N_HEADS), pos).transpose(0, 2, 1, 3)
    v = _split_heads(v, N_HEADS).transpose(0, 2, 1, 3)
    nblk = -(-seq // MOBA_BLOCK)
    padk = ((0, 0), (0, 0), (0, nblk * MOBA_BLOCK - seq), (0, 0))
    kb = jnp.pad(k, padk).reshape(bsz, N_HEADS, nblk, MOBA_BLOCK, HEAD_DIM)
    vb = jnp.pad(v, padk).reshape(bsz, N_HEADS, nblk, MOBA_BLOCK, HEAD_DIM)
    kmean = kb.mean(axis=3)
    n_sel = min(MOBA_TOPK, nblk)
    n_chunks = seq // MOBA_Q_CHUNK
    q_chunks = q.reshape(bsz, N_HEADS, n_chunks, MOBA_Q_CHUNK, HEAD_DIM).transpose(2, 0, 1, 3, 4)
    b_idx = jnp.arange(bsz)[:, None, None, None]
    h_idx = jnp.arange(N_HEADS)[None, :, None, None]
    scale = HEAD_DIM ** -0.5

    def chunk(args):
        qc, c = args
        tq = c * MOBA_Q_CHUNK + jnp.arange(MOBA_Q_CHUNK)
        own = (c * MOBA_Q_CHUNK) // MOBA_BLOCK
        gate = jnp.einsum('bhqd,bhnd->bhqn', qc, kmean).astype(jnp.float32)
        gate = jnp.where(jnp.arange(nblk) < own, gate, -jnp.inf)
        _, sel = lax.top_k(gate, n_sel)
        sel_ok = sel < own
        kg = kb[b_idx, h_idx, sel]
        vg = vb[b_idx, h_idx, sel]
        s_sel = jnp.einsum('bhqd,bhqnkd->bhqnk', qc, kg).astype(jnp.float32) * scale
        s_sel = jnp.where(sel_ok[..., None], s_sel, MASK_VALUE)
        k_own = lax.dynamic_index_in_dim(kb, own, axis=2, keepdims=False)
        v_own = lax.dynamic_index_in_dim(vb, own, axis=2, keepdims=False)
        s_own = jnp.einsum('bhqd,bhkd->bhqk', qc, k_own).astype(jnp.float32) * scale
        own_pos = own * MOBA_BLOCK + jnp.arange(MOBA_BLOCK)
        s_own = jnp.where(own_pos[None, :] <= tq[:, None], s_own, MASK_VALUE)
        n_sk = n_sel * MOBA_BLOCK
        s = jnp.concatenate([s_sel.reshape(bsz, N_HEADS, MOBA_Q_CHUNK, n_sk), s_own], axis=-1)
        p = jax.nn.softmax(s, axis=-1).astype(v.dtype)
        p_sel = p[..., :n_sk].reshape(bsz, N_HEADS, MOBA_Q_CHUNK, n_sel, MOBA_BLOCK)
        return (jnp.einsum('bhqnk,bhqnkd->bhqd', p_sel, vg)
                + jnp.einsum('bhqk,bhkd->bhqd', p[..., n_sk:], v_own))

    o = lax.map(chunk, (q_chunks, jnp.arange(n_chunks)))
    return o.transpose(1, 0, 3, 2, 4).reshape(bsz, seq, D_ATTN) @ w_out


def moe_ffn(x, router_w, router_b, w_gu, b_gu, w_dn, b_dn):
    bsz, seq, dm = x.shape
    xt = x.reshape(-1, dm)
    n_tok = xt.shape[0]
    n_assign = n_tok * TOP_K
    logits = (xt @ router_w + router_b).astype(jnp.float32)
    top_logit, top_exp = lax.top_k(logits, TOP_K)
    gates = jax.nn.softmax(top_logit, axis=-1)
    flat_exp = top_exp.reshape(-1)
    order = jnp.argsort(flat_exp)
    sorted_exp = flat_exp[order]
    sorted_tok = (order // TOP_K).astype(jnp.int32)
    sorted_gate = gates.reshape(-1)[order]
    counts = jnp.bincount(flat_exp, length=N_EXPERTS)
    padded = (counts + MOE_ROW_BLOCK - 1) // MOE_ROW_BLOCK * MOE_ROW_BLOCK
    start = jnp.cumsum(counts) - counts
    pad_end = jnp.cumsum(padded)
    pad_start = pad_end - padded
    dest = pad_start[sorted_exp] + jnp.arange(n_assign) - start[sorted_exp]
    n_blocks = -(-n_assign // MOE_ROW_BLOCK) + N_EXPERTS
    n_rows = n_blocks * MOE_ROW_BLOCK
    row_tok = jnp.full((n_rows,), n_tok, jnp.int32).at[dest].set(sorted_tok)
    row_gate = jnp.zeros((n_rows,), jnp.float32).at[dest].set(sorted_gate)
    block_exp = jnp.minimum(
        jnp.searchsorted(pad_end, jnp.arange(n_blocks) * MOE_ROW_BLOCK, side='right'), N_EXPERTS - 1)
    x_rows = jnp.concatenate([xt, jnp.zeros((1, dm), xt.dtype)])[row_tok]
    x_rows = x_rows.reshape(n_blocks, MOE_ROW_BLOCK, dm)

    def expert_block(args):
        xb, e = args
        h = xb @ w_gu[e] + b_gu[e]
        glu, lin = jnp.split(h, 2, axis=-1)
        glu = jnp.minimum(glu, SWIGLU_LIMIT)
        lin = jnp.clip(lin, -SWIGLU_LIMIT, SWIGLU_LIMIT)
        act = glu * jax.nn.sigmoid(SWIGLU_ALPHA * glu) * (lin + 1.0)
        return act @ w_dn[e] + b_dn[e]

    y_rows = lax.map(expert_block, (x_rows, block_exp)).reshape(n_rows, dm)
    y = jax.ops.segment_sum(y_rows * row_gate[:, None], row_tok, num_segments=n_tok + 1)[:n_tok]
    return y.reshape(bsz, seq, dm).astype(x.dtype)


def setup_inputs(seed: int = 0) -> dict:
    keys = iter(jax.random.split(jax.random.key(seed), 16 * DEPTH + 1))

    def rnd(shape, scale, offset=0.0):
        return offset + scale * jax.random.normal(next(keys), shape, jnp.float32)

    inputs = {'x': rnd((BATCH, SEQ, D_MODEL), 1.0)}
    for i in range(DEPTH):
        kind = i % N_MIXERS
        p = 'l%d_' % i
        inputs[p + 'w_in'] = rnd((D_MODEL, MIXER_IN_WIDTHS[kind]), D_MODEL ** -0.5)
        if kind == 1:
            inputs[p + 'idx_norm_g'] = rnd((IDX_DIM,), 0.02, 1.0)
            inputs[p + 'idx_norm_b'] = rnd((IDX_DIM,), 0.02)
        if kind == 2:
            inputs[p + 'b_forget'] = rnd((N_HEADS,), 0.5, FORGET_BIAS_MEAN)
        inputs[p + 'w_out'] = rnd((D_ATTN, D_MODEL), D_ATTN ** -0.5 * DEEPNORM_BETA)
        inputs[p + 'ln1_g'] = rnd((D_MODEL,), 0.02, 1.0)
        inputs[p + 'ln1_b'] = rnd((D_MODEL,), 0.02)
        inputs[p + 'router_w'] = rnd((D_MODEL, N_EXPERTS), D_MODEL ** -0.5)
        inputs[p + 'router_b'] = rnd((N_EXPERTS,), 0.01)
        inputs[p + 'w_gu'] = rnd((N_EXPERTS, D_MODEL, 2 * D_EXPERT), D_MODEL ** -0.5)
        inputs[p + 'b_gu'] = rnd((N_EXPERTS, 2 * D_EXPERT), 0.02)
        inputs[p + 'w_dn'] = rnd((N_EXPERTS, D_EXPERT, D_MODEL), D_EXPERT ** -0.5 * DEEPNORM_BETA)
        inputs[p + 'b_dn'] = rnd((N_EXPERTS, D_MODEL), 0.02)
        inputs[p + 'ln2_g'] = rnd((D_MODEL,), 0.02, 1.0)
        inputs[p + 'ln2_b'] = rnd((D_MODEL,), 0.02)
    return inputs


def reference(x,
              l0_w_in, l0_w_out, l0_ln1_g, l0_ln1_b, l0_router_w, l0_router_b,
              l0_w_gu, l0_b_gu, l0_w_dn, l0_b_dn, l0_ln2_g, l0_ln2_b,
              l1_w_in, l1_idx_norm_g, l1_idx_norm_b, l1_w_out, l1_ln1_g, l1_ln1_b,
              l1_router_w, l1_router_b, l1_w_gu, l1_b_gu, l1_w_dn, l1_b_dn, l1_ln2_g, l1_ln2_b,
              l2_w_in, l2_b_forget, l2_w_out, l2_ln1_g, l2_ln1_b, l2_router_w, l2_router_b,
              l2_w_gu, l2_b_gu, l2_w_dn, l2_b_dn, l2_ln2_g, l2_ln2_b,
              l3_w_in, l3_w_out, l3_ln1_g, l3_ln1_b, l3_router_w, l3_router_b,
              l3_w_gu, l3_b_gu, l3_w_dn, l3_b_dn, l3_ln2_g, l3_ln2_b):
    pos = jnp.arange(x.shape[1], dtype=jnp.int32)
    mixer_params = ((l0_w_in, l0_w_out),
                    (l1_w_in, l1_idx_norm_g, l1_idx_norm_b, l1_w_out),
                    (l2_w_in, l2_b_forget, l2_w_out),
                    (l3_w_in, l3_w_out))
    ln1 = ((l0_ln1_g, l0_ln1_b), (l1_ln1_g, l1_ln1_b), (l2_ln1_g, l2_ln1_b), (l3_ln1_g, l3_ln1_b))
    ffn = ((l0_router_w, l0_router_b, l0_w_gu, l0_b_gu, l0_w_dn, l0_b_dn),
           (l1_router_w, l1_router_b, l1_w_gu, l1_b_gu, l1_w_dn, l1_b_dn),
           (l2_router_w, l2_router_b, l2_w_gu, l2_b_gu, l2_w_dn, l2_b_dn),
           (l3_router_w, l3_router_b, l3_w_gu, l3_b_gu, l3_w_dn, l3_b_dn))
    ln2 = ((l0_ln2_g, l0_ln2_b), (l1_ln2_g, l1_ln2_b), (l2_ln2_g, l2_ln2_b), (l3_ln2_g, l3_ln2_b))
    for i in range(DEPTH):
        kind = i % N_MIXERS
        if kind == 0:
            h = mixer_dilated(x, pos, *mixer_params[i])
        elif kind == 1:
            h = mixer_dsa(x, pos, *mixer_params[i])
        elif kind == 2:
            h = mixer_fox(x, *mixer_params[i])
        else:
            h = mixer_moba(x, pos, *mixer_params[i])
        x = layer_norm(DEEPNORM_ALPHA * x + h, *ln1[i])
        x = layer_norm(DEEPNORM_ALPHA * x + moe_ffn(x, *ffn[i]), *ln2[i])
    return x
```

```python
import functools

import jax
import jax.numpy as jnp
import numpy as np
from jax import lax
from jax.experimental import pallas as pl
from jax.experimental.pallas import tpu as pltpu

N_HEADS = 16
HEAD_DIM = 128
D_MODEL = 2048
D_ATTN = N_HEADS * HEAD_DIM
ROPE_THETA = 10000.0
MASK_VALUE = -1e30
LN_EPS = 1e-5
DILATED_GROUPS = ((128, 1), (512, 4), (2048, 16))
B_KV_HEADS = 4
GQA_GROUP = N_HEADS // B_KV_HEADS
KV_WIDTH = B_KV_HEADS * HEAD_DIM
IDX_HEADS = 16
IDX_DIM = 64
IDX_TOPK_MAX = 256
MOBA_BLOCK = 256
MOBA_TOPK = 3
N_EXPERTS = 32
TOP_K = 4
D_EXPERT = 1024
SWIGLU_ALPHA = 1.702
SWIGLU_LIMIT = 7.0
DEPTH = 4
DEEPNORM_ALPHA = (2 * DEPTH) ** 0.25

LANES = 128
VMEM_LIMIT = 56 * 1024 * 1024
MOE_TM = 256
INT_MIN = -(2 ** 31)
NEG_INF_KEY = -2139095041

BF16 = jnp.bfloat16
F32 = jnp.float32


def _cparams(sem):
    return pltpu.CompilerParams(dimension_semantics=sem, vmem_limit_bytes=VMEM_LIMIT)


def _rope_tables(seq, half):
    inv_freq = ROPE_THETA ** (-jnp.arange(half, dtype=F32) / half)
    ang = jnp.arange(seq, dtype=F32)[:, None] * inv_freq[None, :]
    cos = jnp.tile(jnp.cos(ang), (1, LANES // half))
    sin = jnp.tile(jnp.concatenate([-jnp.sin(ang), jnp.sin(ang)], -1), (1, LANES // (2 * half)))
    return cos, sin


def _rotate_half(t, half):
    if 2 * half == LANES:
        return pltpu.roll(t, half, 1)
    lane = lax.broadcasted_iota(jnp.int32, t.shape, 1)
    first = (lane % (2 * half)) < half
    return jnp.where(first, pltpu.roll(t, LANES - half, 1), pltpu.roll(t, half, 1))


def _proj_body(*refs, half):
    if half:
        x_ref, w_ref, cos_ref, sin_ref, o_ref = refs
    else:
        x_ref, w_ref, o_ref = refs
    acc = jnp.dot(x_ref[...].astype(BF16), w_ref[...], preferred_element_type=F32)
    if not half:
        o_ref[...] = acc.astype(o_ref.dtype)
        return
    cos = cos_ref[...]
    sin = sin_ref[...]
    for c in range(acc.shape[1] // LANES):
        t = acc[:, c * LANES:(c + 1) * LANES]
        o_ref[:, c * LANES:(c + 1) * LANES] = (t * cos + _rotate_half(t, half) * sin).astype(o_ref.dtype)


def _proj(x2, w, seq, half=0, out_dtype=BF16, tm=512):
    m, k = x2.shape
    n = w.shape[1]
    tm = min(tm, seq)
    tn = 512 if n % 512 == 0 else (256 if n % 256 == 0 else LANES)
    in_specs = [pl.BlockSpec((tm, k), lambda i, j: (i, 0)),
                pl.BlockSpec((k, tn), lambda i, j: (0, j))]
    args = [x2, w]
    if half:
        cos, sin = _rope_tables(seq, half)
        nsb = seq // tm
        in_specs += [pl.BlockSpec((tm, LANES), lambda i, j: (i % nsb, 0))] * 2
        args += [cos, sin]
    return pl.pallas_call(
        functools.partial(_proj_body, half=half),
        grid=(m // tm, n // tn),
        in_specs=in_specs,
        out_specs=pl.BlockSpec((tm, tn), lambda i, j: (i, j)),
        out_shape=jax.ShapeDtypeStruct((m, n), out_dtype),
        compiler_params=_cparams(("parallel", "arbitrary")),
        name="proj_rope%d" % half,
    )(*args)


def _layer_norm_rows(z, g, b):
    mu = jnp.mean(z, axis=-1, keepdims=True)
    zc = z - mu
    var = jnp.mean(zc * zc, axis=-1, keepdims=True)
    return zc * lax.rsqrt(var + LN_EPS) * g + b


def _outln_body(a_ref, w_ref, x_ref, g_ref, b_ref, rw_ref, rb_ref, xo_ref, xb_ref, lg_ref):
    h = jnp.dot(a_ref[...], w_ref[...], preferred_element_type=F32)
    y = _layer_norm_rows(DEEPNORM_ALPHA * x_ref[...] + h, g_ref[...], b_ref[...])
    xo_ref[...] = y
    yb = y.astype(BF16)
    xb_ref[...] = yb
    lg_ref[...] = jnp.dot(yb, rw_ref[...], preferred_element_type=F32) + rb_ref[...]


def _outln(a, w_out, x2, g, b, rw, rb, tm=256):
    m, d = x2.shape
    row = lambda i: (i, 0)
    const = lambda i: (0, 0)
    return pl.pallas_call(
        _outln_body,
        grid=(m // tm,),
        in_specs=[pl.BlockSpec((tm, d), row), pl.BlockSpec((d, d), const), pl.BlockSpec((tm, d), row),
                  pl.BlockSpec((1, d), const), pl.BlockSpec((1, d), const),
                  pl.BlockSpec((d, N_EXPERTS), const), pl.BlockSpec((1, N_EXPERTS), const)],
        out_specs=[pl.BlockSpec((tm, d), row), pl.BlockSpec((tm, d), row),
                   pl.BlockSpec((tm, N_EXPERTS), row)],
        out_shape=[jax.ShapeDtypeStruct((m, d), F32), jax.ShapeDtypeStruct((m, d), BF16),
                   jax.ShapeDtypeStruct((m, N_EXPERTS), F32)],
        compiler_params=_cparams(("parallel",)),
        name="outproj_ln_router",
    )(a, w_out, x2, g.reshape(1, d), b.reshape(1, d), rw, rb.reshape(1, N_EXPERTS))


def _moe_body(be_ref, nb_ref, x_ref, wgu_ref, bgu_ref, wdn_ref, bdn_ref, o_ref):
    i = pl.program_id(0)

    @pl.when(i < nb_ref[0])
    def _():
        h = jnp.dot(x_ref[...], wgu_ref[0], preferred_element_type=F32) + bgu_ref[0]
        glu = jnp.minimum(h[:, :D_EXPERT], SWIGLU_LIMIT)
        lin = jnp.clip(h[:, D_EXPERT:], -SWIGLU_LIMIT, SWIGLU_LIMIT)
        act = glu * jax.nn.sigmoid(SWIGLU_ALPHA * glu) * (lin + 1.0)
        o_ref[...] = jnp.dot(act.astype(BF16), wdn_ref[0], preferred_element_type=F32) + bdn_ref[0]

    @pl.when(i >= nb_ref[0])
    def _():
        o_ref[...] = jnp.zeros_like(o_ref)


def _moe_experts(x_rows, block_exp, n_used, w_gu, b_gu, w_dn, b_dn):
    n_rows, d = x_rows.shape
    n_blocks = n_rows // MOE_TM
    grid_spec = pltpu.PrefetchScalarGridSpec(
        num_scalar_prefetch=2,
        grid=(n_blocks,),
        in_specs=[pl.BlockSpec((MOE_TM, d), lambda i, be, nb: (i, 0)),
                  pl.BlockSpec((1, d, 2 * D_EXPERT), lambda i, be, nb: (be[i], 0, 0)),
                  pl.BlockSpec((1, 1, 2 * D_EXPERT), lambda i, be, nb: (be[i], 0, 0)),
                  pl.BlockSpec((1, D_EXPERT, d), lambda i, be, nb: (be[i], 0, 0)),
                  pl.BlockSpec((1, 1, d), lambda i, be, nb: (be[i], 0, 0))],
        out_specs=pl.BlockSpec((MOE_TM, d), lambda i, be, nb: (i, 0)),
    )
    return pl.pallas_call(
        _moe_body,
        grid_spec=grid_spec,
        out_shape=jax.ShapeDtypeStruct((n_rows, d), F32),
        compiler_params=_cparams(("arbitrary",)),
        name="moe_experts",
    )(block_exp, n_used, x_rows, w_gu, b_gu.reshape(N_EXPERTS, 1, -1), w_dn, b_dn.reshape(N_EXPERTS, 1, -1))


def _combine_body(y4_ref, gt_ref, x_ref, g_ref, b_ref, xo_ref, xb_ref):
    d = x_ref.shape[1]
    gt = gt_ref[...]
    y = gt[:, 0:1] * y4_ref[:, 0:d]
    for k in range(1, TOP_K):
        y = y + gt[:, k:k + 1] * y4_ref[:, k * d:(k + 1) * d]
    out = _layer_norm_rows(DEEPNORM_ALPHA * x_ref[...] + y, g_ref[...], b_ref[...])
    xo_ref[...] = out
    xb_ref[...] = out.astype(BF16)


def _combine_ln(y4, gates, x2, g, b, tm=256):
    m, d = x2.shape
    row = lambda i: (i, 0)
    const = lambda i: (0, 0)
    return pl.pallas_call(
        _combine_body,
        grid=(m // tm,),
        in_specs=[pl.BlockSpec((tm, TOP_K * d), row), pl.BlockSpec((tm, TOP_K), row),
                  pl.BlockSpec((tm, d), row), pl.BlockSpec((1, d), const), pl.BlockSpec((1, d), const)],
        out_specs=[pl.BlockSpec((tm, d), row), pl.BlockSpec((tm, d), row)],
        out_shape=[jax.ShapeDtypeStruct((m, d), F32), jax.ShapeDtypeStruct((m, d), BF16)],
        compiler_params=_cparams(("parallel",)),
        name="moe_combine_ln",
    )(y4, gates, x2, g.reshape(1, d), b.reshape(1, d))


def _moe_layer(x1, x1b, logits, w_gu, b_gu, w_dn, b_dn, g, b):
    n_tok, d = x1.shape
    n_assign = n_tok * TOP_K
    top_logit, top_exp = lax.top_k(logits, TOP_K)
    gates = jax.nn.softmax(top_logit, axis=-1)
    flat_exp = top_exp.reshape(-1)
    order = jnp.argsort(flat_exp)
    sorted_exp = flat_exp[order]
    sorted_tok = (order // TOP_K).astype(jnp.int32)
    counts = jnp.bincount(flat_exp, length=N_EXPERTS)
    padded = (counts + MOE_TM - 1) // MOE_TM * MOE_TM
    start = jnp.cumsum(counts) - counts
    pad_end = jnp.cumsum(padded)
    pad_start = pad_end - padded
    dest = (pad_start[sorted_exp] + jnp.arange(n_assign) - start[sorted_exp]).astype(jnp.int32)
    n_blocks = n_assign // MOE_TM + N_EXPERTS
    n_rows = n_blocks * MOE_TM
    row_tok = jnp.zeros((n_rows,), jnp.int32).at[dest].set(sorted_tok)
    pos = jnp.zeros((n_assign,), jnp.int32).at[order].set(dest)
    block_exp = jnp.minimum(
        jnp.searchsorted(pad_end, jnp.arange(n_blocks) * MOE_TM, side='right'), N_EXPERTS - 1).astype(jnp.int32)
    n_used = (pad_end[-1] // MOE_TM).astype(jnp.int32).reshape(1)
    x_rows = x1b[row_tok]
    y_rows = _moe_experts(x_rows, block_exp, n_used, w_gu, b_gu, w_dn, b_dn)
    y4 = y_rows[pos].reshape(n_tok, TOP_K * d)
    return _combine_ln(y4, gates, x1, g, b)


def _band_body(*refs, has_prev, is_last, scale):
    if has_prev:
        q_ref, kc_ref, kp_ref, vc_ref, vp_ref, oi_ref, li_ref = refs[:7]
        outs = refs[7:]
    else:
        q_ref, kc_ref, kp_ref, vc_ref, vp_ref = refs[:5]
        outs = refs[5:]
    o_ref = outs[0]
    j = pl.program_id(2)
    w = q_ref.shape[1]
    qi = lax.broadcasted_iota(jnp.int32, (w, 2 * w), 0) + w
    ki = lax.broadcasted_iota(jnp.int32, (w, 2 * w), 1)
    dist = qi - ki
    mask = (dist >= 0) & (dist <= w) & ((ki >= w) | (j > 0))
    lane = lax.broadcasted_iota(jnp.int32, (w, LANES), 1)
    lse_all = jnp.zeros((w, LANES), F32)
    for h in range(N_HEADS):
        hs = slice(h * HEAD_DIM, (h + 1) * HEAD_DIM)
        kcat = jnp.concatenate([kp_ref[0, :, hs], kc_ref[0, :, hs]], axis=0)
        vcat = jnp.concatenate([vp_ref[0, :, hs], vc_ref[0, :, hs]], axis=0)
        s = lax.dot_general(q_ref[0, :, hs], kcat, (((1,), (1,)), ((), ())),
                            preferred_element_type=F32) * scale
        s = jnp.where(mask, s, MASK_VALUE)
        m = jnp.max(s, axis=-1, keepdims=True)
        p = jnp.exp(s - m)
        l = jnp.sum(p, axis=-1, keepdims=True)
        o = jnp.dot((p / l).astype(BF16), vcat, preferred_element_type=F32)
        lse = m + jnp.log(l)
        if has_prev:
            lse_prev = li_ref[0, :, h:h + 1]
            mx = jnp.maximum(lse, lse_prev)
            e_new = jnp.exp(lse - mx)
            e_old = jnp.exp(lse_prev - mx)
            tot = e_new + e_old
            o = (e_new * o + e_old * oi_ref[0, :, hs]) / tot
            lse = mx + jnp.log(tot)
        o_ref[0, :, hs] = o.astype(o_ref.dtype)
        lse_all = jnp.where(lane == h, lse, lse_all)
    if not is_last:
        outs[1][0] = lse_all


def _band_stage(qk, v, prev, dil, is_last):
    bsz, seq, _ = v.shape
    w = LANES
    sub = seq // dil
    qk_v = qk.reshape(bsz, sub, dil * 2 * D_ATTN)
    v_v = v.reshape(bsz, sub, dil * D_ATTN)
    nb = sub // w
    in_specs = [pl.BlockSpec((1, w, D_ATTN), lambda b, r, j: (b, j, 2 * r)),
                pl.BlockSpec((1, w, D_ATTN), lambda b, r, j: (b, j, 2 * r + 1)),
                pl.BlockSpec((1, w, D_ATTN), lambda b, r, j: (b, jnp.maximum(j - 1, 0), 2 * r + 1)),
                pl.BlockSpec((1, w, D_ATTN), lambda b, r, j: (b, j, r)),
                pl.BlockSpec((1, w, D_ATTN), lambda b, r, j: (b, jnp.maximum(j - 1, 0), r))]
    args = [qk_v, qk_v, qk_v, v_v, v_v]
    o_spec = pl.BlockSpec((1, w, D_ATTN), lambda b, r, j: (b, j, r))
    l_spec = pl.BlockSpec((1, w, LANES), lambda b, r, j: (b, j, r))
    if prev is not None:
        in_specs += [o_spec, l_spec]
        args += [prev[0].reshape(bsz, sub, dil * D_ATTN), prev[1].reshape(bsz, sub, dil * LANES)]
    if is_last:
        out_specs = [o_spec]
        out_shape = [jax.ShapeDtypeStruct((bsz, sub, dil * D_ATTN), BF16)]
    else:
        out_specs = [o_spec, l_spec]
        out_shape = [jax.ShapeDtypeStruct((bsz, sub, dil * D_ATTN), F32),
                     jax.ShapeDtypeStruct((bsz, sub, dil * LANES), F32)]
    outs = pl.pallas_call(
        functools.partial(_band_body, has_prev=prev is not None, is_last=is_last, scale=HEAD_DIM ** -0.5),
        grid=(bsz, dil, nb),
        in_specs=in_specs, out_specs=out_specs, out_shape=out_shape,
        compiler_params=_cparams(("parallel", "parallel", "arbitrary")),
        name="dilated_band_d%d" % dil,
    )(*args)
    if is_last:
        return outs[0].reshape(bsz, seq, D_ATTN)
    return outs[0].reshape(bsz, seq, D_ATTN), outs[1].reshape(bsz, seq, LANES)


def _mixer_dilated(xb2, bsz, seq, w_in):
    qk = _proj(xb2, w_in[:, :2 * D_ATTN], seq, half=64).reshape(bsz, seq, 2 * D_ATTN)
    v = _proj(xb2, w_in[:, 2 * D_ATTN:], seq).reshape(bsz, seq, D_ATTN)
    prev = None
    for g, (window, dil) in enumerate(DILATED_GROUPS):
        assert window // dil == LANES and seq % (dil * LANES) == 0
        prev = _band_stage(qk, v, prev, dil, g == len(DILATED_GROUPS) - 1)
    return prev.reshape(bsz * seq, D_ATTN)


def _flash_body(*refs, kind, hp, tq, scale):
    if kind == "fox":
        q_ref, k_ref, v_ref, cq_ref, ck_ref, o_ref = refs
    else:
        q_ref, k_ref, v_ref, o_ref, kmean_ref = refs
    hg = pl.program_id(1)
    i = pl.program_id(2)
    seq = k_ref.shape[1]
    nblk = seq // tq
    rel = lax.broadcasted_iota(jnp.int32, (tq, tq), 0) - lax.broadcasted_iota(jnp.int32, (tq, tq), 1)
    causal = rel >= 0

    if kind == "moba":
        @pl.when(i == 0)
        def _():
            kmean_ref[...] = jnp.zeros_like(kmean_ref)
            for hh in range(hp):
                hs = slice(hh * HEAD_DIM, (hh + 1) * HEAD_DIM)
                for n in range(nblk):
                    kb = k_ref[0, n * tq:(n + 1) * tq, hs].astype(F32)
                    kmean_ref[hh, n:n + 1, :] = jnp.sum(kb, axis=0, keepdims=True) / tq

    for hh in range(hp):
        hs = slice(hh * HEAD_DIM, (hh + 1) * HEAD_DIM)
        q = q_ref[0, :, hs]
        if kind == "fox":
            cq = cq_ref[0, 0, :, hh:hh + 1]
            head = hg * hp + hh
        else:
            gate = lax.dot_general(q, kmean_ref[hh].astype(BF16), (((1,), (1,)), ((), ())),
                                   preferred_element_type=F32)
            blk_id = lax.broadcasted_iota(jnp.int32, gate.shape, 1).astype(F32)
            own = i.astype(F32)
            gate = jnp.where(blk_id < own, gate, -jnp.inf)
            sel = jnp.zeros(gate.shape, F32)
            for _ in range(min(MOBA_TOPK, nblk)):
                mx = jnp.max(gate, axis=-1, keepdims=True)
                first = jnp.min(jnp.where(gate == mx, blk_id, float(LANES)), axis=-1, keepdims=True)
                pick = blk_id == first
                sel = jnp.where(pick & (first < own), 1.0, sel)
                gate = jnp.where(pick, -jnp.inf, gate)

        def step(j, carry, diag):
            m, l, acc = carry
            off = pl.multiple_of(j * tq, tq)
            k = k_ref[0, pl.ds(off, tq), hs]
            v = v_ref[0, pl.ds(off, tq), hs]
            s = lax.dot_general(q, k, (((1,), (1,)), ((), ())), preferred_element_type=F32) * scale
            if kind == "fox":
                s = s + (cq - ck_ref[0, pl.ds(head, 1), pl.ds(off, tq)])
            if diag:
                s = jnp.where(causal, s, MASK_VALUE)
            elif kind == "moba":
                rowsel = jnp.max(jnp.where(blk_id == j.astype(F32), sel, 0.0), axis=-1, keepdims=True) > 0.0
                s = jnp.where(rowsel, s, MASK_VALUE)
            m_new = jnp.maximum(m, jnp.max(s, axis=-1, keepdims=True))
            p = jnp.exp(s - m_new)
            alpha = jnp.exp(m - m_new)
            l = alpha * l + jnp.sum(p, axis=-1, keepdims=True)
            acc = alpha * acc + jnp.dot(p.astype(BF16), v, preferred_element_type=F32)
            return m_new, l, acc

        init = (jnp.full((tq, 1), -jnp.inf, F32), jnp.zeros((tq, 1), F32), jnp.zeros((tq, HEAD_DIM), F32))
        carry = lax.fori_loop(0, i, functools.partial(step, diag=False), init)
        m, l, acc = step(i, carry, True)
        o_ref[0, :, hs] = (acc / l).astype(o_ref.dtype)


def _flash(kind, q_arr, q_off, k_arr, k_off, v_arr, v_off, extra, hp, tq):
    bsz, seq, _ = q_arr.shape
    wid = hp * HEAD_DIM
    hgs = N_HEADS // hp
    qo, ko, vo = q_off // hp, k_off // hp, v_off // hp
    in_specs = [pl.BlockSpec((1, tq, wid), lambda b, h, i: (b, i, qo + h)),
                pl.BlockSpec((1, seq, wid), lambda b, h, i: (b, 0, ko + h)),
                pl.BlockSpec((1, seq, wid), lambda b, h, i: (b, 0, vo + h))]
    args = [q_arr, k_arr, v_arr]
    scratch = []
    if kind == "fox":
        cum_col, cum_row = extra
        in_specs += [pl.BlockSpec((1, 1, tq, hp), lambda b, h, i: (b, h, i, 0)),
                     pl.BlockSpec((1, N_HEADS, seq), lambda b, h, i: (b, 0, 0))]
        args += [cum_col, cum_row]
    else:
        assert seq // tq <= LANES
        scratch = [pltpu.VMEM((hp, LANES, HEAD_DIM), F32)]
    return pl.pallas_call(
        functools.partial(_flash_body, kind=kind, hp=hp, tq=tq, scale=HEAD_DIM ** -0.5),
        grid=(bsz, hgs, seq // tq),
        in_specs=in_specs,
        out_specs=pl.BlockSpec((1, tq, wid), lambda b, h, i: (b, i, h)),
        out_shape=jax.ShapeDtypeStruct((bsz, seq, D_ATTN), BF16),
        scratch_shapes=scratch,
        compiler_params=_cparams(("parallel", "parallel", "arbitrary")),
        name="flash_" + kind,
    )(*args)


def _cumsum_body(f_ref, b_ref, o_ref, carry_ref):
    j = pl.program_id(1)

    @pl.when(j == 0)
    def _():
        carry_ref[...] = jnp.zeros_like(carry_ref)

    ts = f_ref.shape[1]
    logf = jax.nn.log_sigmoid(f_ref[0] + b_ref[...])
    tri = (lax.broadcasted_iota(jnp.int32, (ts, ts), 0) >= lax.broadcasted_iota(jnp.int32, (ts, ts), 1)).astype(F32)
    cum = jnp.dot(tri, logf, precision=lax.Precision.HIGHEST, preferred_element_type=F32) + carry_ref[...]
    o_ref[0] = cum
    carry_ref[...] = cum[ts - 1:ts, :]


def _forget_cumsum(f_raw, b_pad, ts=256):
    bsz, seq, _ = f_raw.shape
    return pl.pallas_call(
        _cumsum_body,
        grid=(bsz, seq // ts),
        in_specs=[pl.BlockSpec((1, ts, LANES), lambda b, j: (b, j, 0)),
                  pl.BlockSpec((1, LANES), lambda b, j: (0, 0))],
        out_specs=pl.BlockSpec((1, ts, LANES), lambda b, j: (b, j, 0)),
        out_shape=jax.ShapeDtypeStruct((bsz, seq, LANES), F32),
        scratch_shapes=[pltpu.VMEM((1, LANES), F32)],
        compiler_params=_cparams(("parallel", "arbitrary")),
        name="forget_cumsum",
    )(f_raw, b_pad)


def _pad_cols(w, n):
    return jnp.pad(w, ((0, 0), (0, n - w.shape[1])))


def _mixer_fox(xb2, bsz, seq, w_in, b_forget, hp=2, tq=512):
    tq = min(tq, seq)
    qkv = _proj(xb2, w_in[:, :3 * D_ATTN], seq).reshape(bsz, seq, 3 * D_ATTN)
    f_raw = _proj(xb2, _pad_cols(w_in[:, 3 * D_ATTN:], LANES), seq, out_dtype=F32).reshape(bsz, seq, LANES)
    cum = _forget_cumsum(f_raw, _pad_cols(b_forget.reshape(1, N_HEADS), LANES))[..., :N_HEADS]
    cum_row = cum.transpose(0, 2, 1)
    cum_col = cum.reshape(bsz, seq, N_HEADS // hp, hp).transpose(0, 2, 1, 3)
    o = _flash("fox", qkv, 0, qkv, N_HEADS, qkv, 2 * N_HEADS, (cum_col, cum_row), hp, tq)
    return o.reshape(bsz * seq, D_ATTN)


def _mixer_moba(xb2, bsz, seq, w_in, hp=2):
    qkv = _proj(xb2, w_in[:, :2 * D_ATTN], seq, half=64).reshape(bsz, seq, 2 * D_ATTN)
    v = _proj(xb2, w_in[:, 2 * D_ATTN:], seq).reshape(bsz, seq, D_ATTN)
    assert seq % MOBA_BLOCK == 0
    o = _flash("moba", qkv, 0, qkv, N_HEADS, v, 0, None, hp, MOBA_BLOCK)
    return o.reshape(bsz * seq, D_ATTN)


def _kiprep_body(r_ref, g_ref, b_ref, cos_ref, sin_ref, a_ref, b2_ref):
    x = r_ref[...]
    lane = lax.broadcasted_iota(jnp.int32, x.shape, 1)
    inside = lane < IDX_DIM
    mu = jnp.sum(jnp.where(inside, x, 0.0), axis=-1, keepdims=True) / IDX_DIM
    xc = jnp.where(inside, x - mu, 0.0)
    var = jnp.sum(xc * xc, axis=-1, keepdims=True) / IDX_DIM
    y = xc * lax.rsqrt(var + LN_EPS) * g_ref[...] + b_ref[...]
    y = y * cos_ref[...] + _rotate_half(y, IDX_DIM // 2) * sin_ref[...]
    y = jnp.where(inside, y, 0.0)
    a_ref[...] = y.astype(BF16)
    b2_ref[...] = pltpu.roll(y, IDX_DIM, 1).astype(BF16)


def _ki_prep(raw2, g, b, seq, tm=512):
    m = raw2.shape[0]
    tm = min(tm, seq)
    cos, sin = _rope_tables(seq, IDX_DIM // 2)
    nsb = seq // tm
    row = lambda i: (i, 0)
    const = lambda i: (0, 0)
    return pl.pallas_call(
        _kiprep_body,
        grid=(m // tm,),
        in_specs=[pl.BlockSpec((tm, LANES), row), pl.BlockSpec((1, LANES), const), pl.BlockSpec((1, LANES), const),
                  pl.BlockSpec((tm, LANES), lambda i: (i % nsb, 0)), pl.BlockSpec((tm, LANES), lambda i: (i % nsb, 0))],
        out_specs=[pl.BlockSpec((tm, LANES), row)] * 2,
        out_shape=[jax.ShapeDtypeStruct((m, LANES), BF16)] * 2,
        compiler_params=_cparams(("parallel",)),
        name="dsa_ki_prep",
    )(raw2, _pad_cols(g.reshape(1, IDX_DIM), LANES), _pad_cols(b.reshape(1, IDX_DIM), LANES), cos, sin)


def _dsa_body(qi_ref, wr_ref, kia_ref, kib_ref, q_ref, k_ref, v_ref, o_ref, key_ref, bias_ref,
              *, tq, ch, topk, scale, wscale):
    i = pl.program_id(1)
    nch = (i * tq + tq + ch - 1) // ch
    nsl = ch // LANES
    wi = wr_ref[0][:, IDX_DIM:IDX_DIM + IDX_HEADS] * wscale
    rowpos = i * tq + lax.broadcasted_iota(jnp.int32, (tq, 1), 0)
    lane_ch = lax.broadcasted_iota(jnp.int32, (tq, ch), 1)
    lane_1 = lax.broadcasted_iota(jnp.int32, (tq, LANES), 1)
    dn = (((1,), (1,)), ((), ()))

    def score_chunk(c, _):
        off = pl.multiple_of(c * ch, ch)
        ka = kia_ref[0, pl.ds(off, ch), :]
        kb = kib_ref[0, pl.ds(off, ch), :]
        sc = jnp.zeros((tq, ch), F32)
        for hpair in range(IDX_HEADS // 2):
            qp = qi_ref[0, :, hpair * LANES:(hpair + 1) * LANES]
            for t, kk in enumerate((ka, kb)):
                h = 2 * hpair + t
                lg = lax.dot_general(qp, kk, dn, preferred_element_type=F32)
                sc = sc + wi[:, h:h + 1] * jnp.maximum(lg, 0.0)
        bits = pltpu.bitcast(sc, jnp.int32)
        bits = jnp.where(bits == INT_MIN, 0, bits)
        key = bits ^ ((bits >> 31) & 0x7FFFFFFF)
        key = jnp.where(off + lane_ch <= rowpos, key, NEG_INF_KEY)
        key_ref[:, pl.ds(off, ch)] = key
        return 0

    lax.fori_loop(0, nch, score_chunk, 0)

    def count(pred):
        def cb(c, acc):
            off = pl.multiple_of(c * ch, ch)
            blk = key_ref[:, pl.ds(off, ch)]
            for s_ in range(nsl):
                acc = acc + jnp.where(pred(blk[:, s_ * LANES:(s_ + 1) * LANES], off + s_ * LANES + lane_1), 1, 0)
            return acc
        acc = lax.fori_loop(0, nch, cb, jnp.zeros((tq, LANES), jnp.int32))
        return jnp.sum(acc.astype(F32), axis=-1, keepdims=True).astype(jnp.int32)

    def bisect_val(t, lo):
        cand = lo + (jnp.int32(1) << (31 - t))
        return jnp.where(count(lambda kv, idx: kv >= cand) >= topk, cand, lo)

    thr = lax.fori_loop(0, 32, bisect_val, jnp.full((tq, LANES), INT_MIN, jnp.int32))
    need = topk - count(lambda kv, idx: kv > thr)

    def bisect_idx(t, lo):
        cand = lo + (jnp.int32(1) << (12 - t))
        return jnp.where(count(lambda kv, idx: (kv == thr) & (idx < cand)) < need, cand, lo)

    jmax = lax.fori_loop(0, 13, bisect_idx, jnp.zeros((tq, LANES), jnp.int32))

    def bias_chunk(c, _):
        off = pl.multiple_of(c * ch, ch)
        blk = key_ref[:, pl.ds(off, ch)]
        for s_ in range(nsl):
            kv = blk[:, s_ * LANES:(s_ + 1) * LANES]
            idx = off + s_ * LANES + lane_1
            keep = ((kv > thr) | ((kv == thr) & (idx <= jmax))) & (idx <= rowpos)
            bias_ref[:, pl.ds(off + s_ * LANES, LANES)] = jnp.where(keep, 0.0, MASK_VALUE)
        return 0

    lax.fori_loop(0, nch, bias_chunk, 0)

    for g in range(B_KV_HEADS):
        gs = slice(g * HEAD_DIM, (g + 1) * HEAD_DIM)
        qg = jnp.concatenate(
            [q_ref[0, :, (g * GQA_GROUP + r) * HEAD_DIM:(g * GQA_GROUP + r + 1) * HEAD_DIM]
             for r in range(GQA_GROUP)], axis=0)
        rows = GQA_GROUP * tq

        def att(c, carry):
            m, l, acc = carry
            off = pl.multiple_of(c * ch, ch)
            kc = k_ref[0, pl.ds(off, ch), gs]
            vc = v_ref[0, pl.ds(off, ch), gs]
            bias = bias_ref[:, pl.ds(off, ch)]
            s = lax.dot_general(qg, kc, dn, preferred_element_type=F32) * scale
            s = s + jnp.concatenate([bias] * GQA_GROUP, axis=0)
            m_new = jnp.maximum(m, jnp.max(s, axis=-1, keepdims=True))
            p = jnp.exp(s - m_new)
            alpha = jnp.exp(m - m_new)
            l = alpha * l + jnp.sum(p, axis=-1, keepdims=True)
            acc = alpha * acc + jnp.dot(p.astype(BF16), vc, preferred_element_type=F32)
            return m_new, l, acc

        init = (jnp.full((rows, 1), -jnp.inf, F32), jnp.zeros((rows, 1), F32), jnp.zeros((rows, HEAD_DIM), F32))
        m, l, acc = lax.fori_loop(0, nch, att, init)
        o = acc / l
        for r in range(GQA_GROUP):
            h = g * GQA_GROUP + r
            o_ref[0, :, h * HEAD_DIM:(h + 1) * HEAD_DIM] = o[r * tq:(r + 1) * tq].astype(o_ref.dtype)


def _mixer_dsa(xb2, bsz, seq, w_in, idx_g, idx_b, tq=128, ch=512):
    ch = min(ch, seq)
    nqk = D_ATTN + KV_WIDTH
    qk = _proj(xb2, w_in[:, :nqk], seq, half=64).reshape(bsz, seq, nqk)
    v = _proj(xb2, w_in[:, nqk:nqk + KV_WIDTH], seq).reshape(bsz, seq, KV_WIDTH)
    o_qi = nqk + KV_WIDTH
    n_qi = IDX_HEADS * IDX_DIM
    qi = _proj(xb2, w_in[:, o_qi:o_qi + n_qi], seq, half=IDX_DIM // 2).reshape(bsz, seq, n_qi)
    raw2 = _proj(xb2, _pad_cols(w_in[:, o_qi + n_qi:], LANES), seq, out_dtype=F32)
    kia, kib = _ki_prep(raw2, idx_g, idx_b, seq)
    topk = min(IDX_TOPK_MAX, seq // 4)
    body = functools.partial(_dsa_body, tq=tq, ch=ch, topk=topk, scale=HEAD_DIM ** -0.5,
                             wscale=IDX_HEADS ** -0.5 * IDX_DIM ** -0.5)
    res = lambda b, i: (b, 0, 0)
    o = pl.pallas_call(
        body,
        grid=(bsz, seq // tq),
        in_specs=[pl.BlockSpec((1, tq, n_qi), lambda b, i: (b, i, 0)),
                  pl.BlockSpec((1, tq, LANES), lambda b, i: (b, i, 0)),
                  pl.BlockSpec((1, seq, LANES), res), pl.BlockSpec((1, seq, LANES), res),
                  pl.BlockSpec((1, tq, D_ATTN), lambda b, i: (b, i, 0)),
                  pl.BlockSpec((1, seq, KV_WIDTH), lambda b, i: (b, 0, D_ATTN // KV_WIDTH)),
                  pl.BlockSpec((1, seq, KV_WIDTH), res)],
        out_specs=pl.BlockSpec((1, tq, D_ATTN), lambda b, i: (b, i, 0)),
        out_shape=jax.ShapeDtypeStruct((bsz, seq, D_ATTN), BF16),
        scratch_shapes=[pltpu.VMEM((tq, seq), jnp.int32), pltpu.VMEM((tq, seq), F32)],
        compiler_params=_cparams(("parallel", "arbitrary")),
        name="dsa_select_attend",
    )(qi, raw2.reshape(bsz, seq, LANES), kia.reshape(bsz, seq, LANES), kib.reshape(bsz, seq, LANES), qk, qk, v)
    return o.reshape(bsz * seq, D_ATTN)


def _trunk(x, layers):
    bsz, seq, d = x.shape
    x2 = x.reshape(bsz * seq, d)
    xb2 = x2
    for kind, p in enumerate(layers):
        w_in = p["w_in"].astype(BF16)
        if kind == 0:
            a = _mixer_dilated(xb2, bsz, seq, w_in)
        elif kind == 1:
            a = _mixer_dsa(xb2, bsz, seq, w_in, p["idx_norm_g"], p["idx_norm_b"])
        elif kind == 2:
            a = _mixer_fox(xb2, bsz, seq, w_in, p["b_forget"])
        else:
            a = _mixer_moba(xb2, bsz, seq, w_in)
        x1, x1b, logits = _outln(a, p["w_out"].astype(BF16), x2, p["ln1_g"], p["ln1_b"],
                                 p["router_w"].astype(BF16), p["router_b"])
        x2, xb2 = _moe_layer(x1, x1b, logits, p["w_gu"].astype(BF16), p["b_gu"], p["w_dn"].astype(BF16),
                             p["b_dn"], p["ln2_g"], p["ln2_b"])
    return x2.reshape(bsz, seq, d)


def kernel(x, l0_w_in, l0_w_out, l0_ln1_g, l0_ln1_b, l0_router_w, l0_router_b, l0_w_gu, l0_b_gu, l0_w_dn, l0_b_dn, l0_ln2_g, l0_ln2_b, l1_w_in, l1_idx_norm_g, l1_idx_norm_b, l1_w_out, l1_ln1_g, l1_ln1_b, l1_router_w, l1_router_b, l1_w_gu, l1_b_gu, l1_w_dn, l1_b_dn, l1_ln2_g, l1_ln2_b, l2_w_in, l2_b_forget, l2_w_out, l2_ln1_g, l2_ln1_b, l2_router_w, l2_router_b, l2_w_gu, l2_b_gu, l2_w_dn, l2_b_dn, l2_ln2_g, l2_ln2_b, l3_w_in, l3_w_out, l3_ln1_g, l3_ln1_b, l3_router_w, l3_router_b, l3_w_gu, l3_b_gu, l3_w_dn, l3_b_dn, l3_ln2_g, l3_ln2_b):
    names = ("w_out", "ln1_g", "ln1_b", "router_w", "router_b", "w_gu", "b_gu", "w_dn", "b_dn", "ln2_g", "ln2_b")
    l0 = dict(zip(("w_in",) + names, (l0_w_in, l0_w_out, l0_ln1_g, l0_ln1_b, l0_router_w, l0_router_b,
                                      l0_w_gu, l0_b_gu, l0_w_dn, l0_b_dn, l0_ln2_g, l0_ln2_b)))
    l1 = dict(zip(("w_in", "idx_norm_g", "idx_norm_b") + names,
                  (l1_w_in, l1_idx_norm_g, l1_idx_norm_b, l1_w_out, l1_ln1_g, l1_ln1_b, l1_router_w, l1_router_b,
                   l1_w_gu, l1_b_gu, l1_w_dn, l1_b_dn, l1_ln2_g, l1_ln2_b)))
    l2 = dict(zip(("w_in", "b_forget") + names,
                  (l2_w_in, l2_b_forget, l2_w_out, l2_ln1_g, l2_ln1_b, l2_router_w, l2_router_b,
                   l2_w_gu, l2_b_gu, l2_w_dn, l2_b_dn, l2_ln2_g, l2_ln2_b)))
    l3 = dict(zip(("w_in",) + names, (l3_w_in, l3_w_out, l3_ln1_g, l3_ln1_b, l3_router_w, l3_router_b,
                                      l3_w_gu, l3_b_gu, l3_w_dn, l3_b_dn, l3_ln2_g, l3_ln2_b)))
    return _trunk(x, (l0, l1, l2, l3))
```

```python
import functools

import jax
import jax.numpy as jnp
import numpy as np
from jax import lax
from jax.experimental import pallas as pl
from jax.experimental.pallas import tpu as pltpu

N_HEADS = 16
HEAD_DIM = 128
D_MODEL = 2048
D_ATTN = N_HEADS * HEAD_DIM
ROPE_THETA = 10000.0
MASK_VALUE = -1e30
LN_EPS = 1e-5
DILATED_GROUPS = ((128, 1), (512, 4), (2048, 16))
DIL_MAX = 16
B_KV_HEADS = 4
GQA_GROUP = N_HEADS // B_KV_HEADS
KV_WIDTH = B_KV_HEADS * HEAD_DIM
IDX_HEADS = 16
IDX_DIM = 64
IDX_TOPK_MAX = 256
MOBA_BLOCK = 256
MOBA_TOPK = 3
N_EXPERTS = 32
TOP_K = 4
D_EXPERT = 1024
SWIGLU_ALPHA = 1.702
SWIGLU_LIMIT = 7.0
DEPTH = 4
DEEPNORM_ALPHA = (2 * DEPTH) ** 0.25

LANES = 128
VMEM_LIMIT = 56 * 1024 * 1024
MOE_TM = 256
INT_MIN = -(2 ** 31)
NEG_INF_KEY = -2139095041
LOG2E = 1.4426950408889634
QSCALE = HEAD_DIM ** -0.5 * LOG2E

BF16 = jnp.bfloat16
F32 = jnp.float32
_DN_T = (((1,), (1,)), ((), ()))


def _cparams(sem):
    return pltpu.CompilerParams(dimension_semantics=sem, vmem_limit_bytes=VMEM_LIMIT)


def _rope_tables(pos, half):
    inv_freq = ROPE_THETA ** (-jnp.arange(half, dtype=F32) / half)
    ang = pos.astype(F32)[:, None] * inv_freq[None, :]
    cos = jnp.tile(jnp.cos(ang), (1, LANES // half))
    sin = jnp.tile(jnp.concatenate([-jnp.sin(ang), jnp.sin(ang)], -1), (1, LANES // (2 * half)))
    return cos, sin


def _rotate_half(t, half):
    if 2 * half == LANES:
        return pltpu.roll(t, half, 1)
    lane = lax.broadcasted_iota(jnp.int32, t.shape, 1)
    first = (lane % (2 * half)) < half
    return jnp.where(first, pltpu.roll(t, LANES - half, 1), pltpu.roll(t, half, 1))


def _proj_body(*refs, half, q_tiles):
    if half:
        x_ref, w_ref, cos_ref, sin_ref, o_ref = refs
    else:
        x_ref, w_ref, o_ref = refs
    acc = jnp.dot(x_ref[...].astype(BF16), w_ref[...], preferred_element_type=F32)
    if q_tiles:
        acc = acc * jnp.where(pl.program_id(1) < q_tiles, QSCALE, 1.0)
    if not half:
        o_ref[...] = acc.astype(o_ref.dtype)
        return
    cos = cos_ref[...]
    sin = sin_ref[...]
    for c in range(acc.shape[1] // LANES):
        t = acc[:, c * LANES:(c + 1) * LANES]
        o_ref[:, c * LANES:(c + 1) * LANES] = (t * cos + _rotate_half(t, half) * sin).astype(o_ref.dtype)


def _proj(x2, w, pos, half=0, out_dtype=BF16, tm=512, qcols=0):
    m, k = x2.shape
    n = w.shape[1]
    seq = pos.shape[0]
    tm = min(tm, seq)
    tn = 512 if n % 512 == 0 else (256 if n % 256 == 0 else LANES)
    assert qcols % tn == 0 and seq % tm == 0
    in_specs = [pl.BlockSpec((tm, k), lambda i, j: (i, 0)),
                pl.BlockSpec((k, tn), lambda i, j: (0, j))]
    args = [x2, w]
    if half:
        cos, sin = _rope_tables(pos, half)
        nsb = seq // tm
        in_specs += [pl.BlockSpec((tm, LANES), lambda i, j: (i % nsb, 0))] * 2
        args += [cos, sin]
    return pl.pallas_call(
        functools.partial(_proj_body, half=half, q_tiles=qcols // tn),
        grid=(m // tm, n // tn),
        in_specs=in_specs,
        out_specs=pl.BlockSpec((tm, tn), lambda i, j: (i, j)),
        out_shape=jax.ShapeDtypeStruct((m, n), out_dtype),
        compiler_params=_cparams(("parallel", "arbitrary")),
        name="proj_rope%d" % half,
    )(*args)


def _layer_norm_rows(z, g, b):
    mu = jnp.mean(z, axis=-1, keepdims=True)
    zc = z - mu
    var = jnp.mean(zc * zc, axis=-1, keepdims=True)
    return zc * lax.rsqrt(var + LN_EPS) * g + b


def _outln_body(a_ref, w_ref, x_ref, g_ref, b_ref, rw_ref, rb_ref, xo_ref, xb_ref, lg_ref):
    h = jnp.dot(a_ref[...], w_ref[...], preferred_element_type=F32)
    y = _layer_norm_rows(DEEPNORM_ALPHA * x_ref[...] + h, g_ref[...], b_ref[...])
    xo_ref[...] = y
    yb = y.astype(BF16)
    xb_ref[...] = yb
    lg_ref[...] = jnp.dot(yb, rw_ref[...], preferred_element_type=F32) + rb_ref[...]


def _outln(a, w_out, x2, g, b, rw, rb, tm=256):
    m, d = x2.shape
    row = lambda i: (i, 0)
    const = lambda i: (0, 0)
    return pl.pallas_call(
        _outln_body,
        grid=(m // tm,),
        in_specs=[pl.BlockSpec((tm, d), row), pl.BlockSpec((d, d), const), pl.BlockSpec((tm, d), row),
                  pl.BlockSpec((1, d), const), pl.BlockSpec((1, d), const),
                  pl.BlockSpec((d, N_EXPERTS), const), pl.BlockSpec((1, N_EXPERTS), const)],
        out_specs=[pl.BlockSpec((tm, d), row), pl.BlockSpec((tm, d), row),
                   pl.BlockSpec((tm, N_EXPERTS), row)],
        out_shape=[jax.ShapeDtypeStruct((m, d), F32), jax.ShapeDtypeStruct((m, d), BF16),
                   jax.ShapeDtypeStruct((m, N_EXPERTS), F32)],
        compiler_params=_cparams(("parallel",)),
        name="outproj_ln_router",
    )(a, w_out, x2, g.reshape(1, d), b.reshape(1, d), rw, rb.reshape(1, N_EXPERTS))


def _moe_body(be_ref, nb_ref, x_ref, wgu_ref, bgu_ref, wdn_ref, bdn_ref, o_ref):
    i = pl.program_id(0)

    @pl.when(i < nb_ref[0])
    def _():
        h = jnp.dot(x_ref[...], wgu_ref[0], preferred_element_type=F32) + bgu_ref[0]
        glu = jnp.minimum(h[:, :D_EXPERT], SWIGLU_LIMIT)
        lin = jnp.clip(h[:, D_EXPERT:], -SWIGLU_LIMIT, SWIGLU_LIMIT)
        act = glu * jax.nn.sigmoid(SWIGLU_ALPHA * glu) * (lin + 1.0)
        o_ref[...] = jnp.dot(act.astype(BF16), wdn_ref[0], preferred_element_type=F32) + bdn_ref[0]

    @pl.when(i >= nb_ref[0])
    def _():
        o_ref[...] = jnp.zeros_like(o_ref)


def _moe_experts(x_rows, block_exp, n_used, w_gu, b_gu, w_dn, b_dn):
    n_rows, d = x_rows.shape
    n_blocks = n_rows // MOE_TM
    grid_spec = pltpu.PrefetchScalarGridSpec(
        num_scalar_prefetch=2,
        grid=(n_blocks,),
        in_specs=[pl.BlockSpec((MOE_TM, d), lambda i, be, nb: (i, 0)),
                  pl.BlockSpec((1, d, 2 * D_EXPERT), lambda i, be, nb: (be[i], 0, 0)),
                  pl.BlockSpec((1, 1, 2 * D_EXPERT), lambda i, be, nb: (be[i], 0, 0)),
                  pl.BlockSpec((1, D_EXPERT, d), lambda i, be, nb: (be[i], 0, 0)),
                  pl.BlockSpec((1, 1, d), lambda i, be, nb: (be[i], 0, 0))],
        out_specs=pl.BlockSpec((MOE_TM, d), lambda i, be, nb: (i, 0)),
    )
    return pl.pallas_call(
        _moe_body,
        grid_spec=grid_spec,
        out_shape=jax.ShapeDtypeStruct((n_rows, d), F32),
        compiler_params=_cparams(("arbitrary",)),
        name="moe_experts",
    )(block_exp, n_used, x_rows, w_gu, b_gu.reshape(N_EXPERTS, 1, -1), w_dn, b_dn.reshape(N_EXPERTS, 1, -1))


def _combine_body(*refs):
    y_refs = refs[:TOP_K]
    gt_ref, x_ref, g_ref, b_ref, xo_ref, xb_ref = refs[TOP_K:]
    gt = gt_ref[...]
    y = gt[:, 0:1] * y_refs[0][...]
    for k in range(1, TOP_K):
        y = y + gt[:, k:k + 1] * y_refs[k][...]
    out = _layer_norm_rows(DEEPNORM_ALPHA * x_ref[...] + y, g_ref[...], b_ref[...])
    xo_ref[...] = out
    xb_ref[...] = out.astype(BF16)


def _combine_ln(y4, gates, x2, g, b, tm=256):
    m, d = x2.shape
    nb = m // tm
    row = lambda i: (i, 0)
    const = lambda i: (0, 0)
    y_specs = [pl.BlockSpec((tm, d), functools.partial(lambda i, k: (k * nb + i, 0), k=k)) for k in range(TOP_K)]
    return pl.pallas_call(
        _combine_body,
        grid=(nb,),
        in_specs=y_specs + [pl.BlockSpec((tm, TOP_K), row),
                            pl.BlockSpec((tm, d), row), pl.BlockSpec((1, d), const), pl.BlockSpec((1, d), const)],
        out_specs=[pl.BlockSpec((tm, d), row), pl.BlockSpec((tm, d), row)],
        out_shape=[jax.ShapeDtypeStruct((m, d), F32), jax.ShapeDtypeStruct((m, d), BF16)],
        compiler_params=_cparams(("parallel",)),
        name="moe_combine_ln",
    )(*([y4] * TOP_K), gates, x2, g.reshape(1, d), b.reshape(1, d))


def _moe_layer(x1, x1b, logits, w_gu, b_gu, w_dn, b_dn, g, b):
    n_tok, d = x1.shape
    n_assign = n_tok * TOP_K
    top_logit, top_exp = lax.top_k(logits, TOP_K)
    gates = jax.nn.softmax(top_logit, axis=-1)
    flat_exp = top_exp.reshape(-1)
    order = jnp.argsort(flat_exp)
    sorted_exp = flat_exp[order]
    sorted_tok = (order // TOP_K).astype(jnp.int32)
    counts = jnp.bincount(flat_exp, length=N_EXPERTS)
    padded = (counts + MOE_TM - 1) // MOE_TM * MOE_TM
    start = jnp.cumsum(counts) - counts
    pad_end = jnp.cumsum(padded)
    pad_start = pad_end - padded
    dest = (pad_start[sorted_exp] + jnp.arange(n_assign) - start[sorted_exp]).astype(jnp.int32)
    n_blocks = n_assign // MOE_TM + N_EXPERTS
    n_rows = n_blocks * MOE_TM
    row_tok = jnp.zeros((n_rows,), jnp.int32).at[dest].set(sorted_tok)
    pos = jnp.zeros((n_assign,), jnp.int32).at[order].set(dest)
    block_exp = jnp.minimum(
        jnp.searchsorted(pad_end, jnp.arange(n_blocks) * MOE_TM, side='right'), N_EXPERTS - 1).astype(jnp.int32)
    n_used = (pad_end[-1] // MOE_TM).astype(jnp.int32).reshape(1)
    x_rows = x1b[row_tok]
    y_rows = _moe_experts(x_rows, block_exp, n_used, w_gu, b_gu, w_dn, b_dn)
    y4 = y_rows[pos.reshape(n_tok, TOP_K).T.reshape(-1)]
    return _combine_ln(y4, gates, x1, g, b)


def _attn_tile(q, k, v, m, l, acc, col_fn):
    s = lax.dot_general(q, k, _DN_T, preferred_element_type=F32)
    cols = [col_fn(c, s[:, c * LANES:(c + 1) * LANES]) for c in range(s.shape[1] // LANES)]
    m_new = jnp.maximum(m, jnp.max(functools.reduce(jnp.maximum, cols), axis=-1, keepdims=True))
    alpha = jnp.exp2(m - m_new)
    ps = [jnp.exp2(c - m_new) for c in cols]
    l = alpha * l + functools.reduce(jnp.add, ps)
    p = jnp.concatenate([x.astype(BF16) for x in ps], axis=1)
    acc = alpha * acc + jnp.dot(p, v, preferred_element_type=F32)
    return m_new, l, acc


def _attn_init(rows):
    return (jnp.full((rows, LANES), -jnp.inf, F32), jnp.zeros((rows, LANES), F32),
            jnp.zeros((rows, HEAD_DIM), F32))


def _attn_finish(l, acc):
    return acc / jnp.sum(l, axis=-1, keepdims=True)


def _band_body(*refs, has_prev, is_last, planes, rows):
    if has_prev:
        q_ref, kc_ref, kp_ref, vc_ref, vp_ref, oi_ref, li_ref = refs[:7]
        outs = refs[7:]
    else:
        q_ref, kc_ref, kp_ref, vc_ref, vp_ref = refs[:5]
        outs = refs[5:]
    o_ref = outs[0]
    j = pl.program_id(2)
    w = planes * rows

    def local(t):
        return (t % rows) * planes + t // rows

    tq = lax.broadcasted_iota(jnp.int32, (w, 2 * w), 0)
    tk = lax.broadcasted_iota(jnp.int32, (w, 2 * w), 1)
    cur = tk >= w
    dist = (w + local(tq)) - (local(tk % w) + jnp.where(cur, w, 0))
    mask = (dist >= 0) & (dist <= LANES) & (cur | (j > 0))
    lane = lax.broadcasted_iota(jnp.int32, (w, LANES), 1)
    lse_all = jnp.zeros((w, LANES), F32)

    def blk(ref, hs):
        return ref[0, :, 0, :, hs].reshape(w, HEAD_DIM)

    if has_prev:
        lse_in = li_ref[0, :, 0].reshape(w, LANES)

    for h in range(N_HEADS):
        hs = slice(h * HEAD_DIM, (h + 1) * HEAD_DIM)
        kcat = jnp.concatenate([blk(kp_ref, hs), blk(kc_ref, hs)], axis=0)
        vcat = jnp.concatenate([blk(vp_ref, hs), blk(vc_ref, hs)], axis=0)
        s = lax.dot_general(blk(q_ref, hs), kcat, _DN_T, preferred_element_type=F32)
        s = jnp.where(mask, s, MASK_VALUE)
        m = jnp.max(s, axis=-1, keepdims=True)
        p = jnp.exp2(s - m)
        l = jnp.sum(p, axis=-1, keepdims=True)
        o = jnp.dot((p / l).astype(BF16), vcat, preferred_element_type=F32)
        lse = m + jnp.log2(l)
        if has_prev:
            lse_prev = lse_in[:, h:h + 1]
            mx = jnp.maximum(lse, lse_prev)
            e_new = jnp.exp2(lse - mx)
            e_old = jnp.exp2(lse_prev - mx)
            tot = e_new + e_old
            o = (e_new * o + e_old * blk(oi_ref, hs)) / tot
            lse = mx + jnp.log2(tot)
        o_ref[0, :, 0, :, hs] = o.astype(o_ref.dtype).reshape(planes, rows, HEAD_DIM)
        lse_all = jnp.where(lane == h, lse, lse_all)
    if not is_last:
        outs[1][0, :, 0] = lse_all.reshape(planes, rows, LANES)


def _band_stage(qk, v, prev, dil, is_last):
    bsz, _, sub, _ = v.shape
    planes = DIL_MAX // dil
    groups = DIL_MAX // planes
    rows = (LANES if planes < DIL_MAX else 2 * LANES) // planes
    view = lambda a: a.reshape(bsz, planes, groups, sub, a.shape[-1])
    nb = sub // rows
    blk = lambda col, prevblk: pl.BlockSpec(
        (1, planes, 1, rows, D_ATTN),
        (lambda b, g, j: (b, 0, g, jnp.maximum(j - 1, 0), col)) if prevblk else (lambda b, g, j: (b, 0, g, j, col)))
    in_specs = [blk(0, False), blk(1, False), blk(1, True), blk(0, False), blk(0, True)]
    args = [view(qk)] * 3 + [view(v)] * 2
    o_spec = blk(0, False)
    l_spec = pl.BlockSpec((1, planes, 1, rows, LANES), lambda b, g, j: (b, 0, g, j, 0))
    if prev is not None:
        in_specs += [o_spec, l_spec]
        args += [view(prev[0]), view(prev[1])]
    out_specs = [o_spec] if is_last else [o_spec, l_spec]
    out_shape = [jax.ShapeDtypeStruct((bsz, planes, groups, sub, D_ATTN), BF16 if is_last else F32)]
    if not is_last:
        out_shape.append(jax.ShapeDtypeStruct((bsz, planes, groups, sub, LANES), F32))
    outs = pl.pallas_call(
        functools.partial(_band_body, has_prev=prev is not None, is_last=is_last, planes=planes, rows=rows),
        grid=(bsz, groups, nb),
        in_specs=in_specs, out_specs=out_specs, out_shape=out_shape,
        compiler_params=_cparams(("parallel", "parallel", "arbitrary")),
        name="dilated_band_d%d" % dil,
    )(*args)
    unview = lambda a: a.reshape(bsz, DIL_MAX, sub, a.shape[-1])
    if is_last:
        return unview(outs[0])
    return unview(outs[0]), unview(outs[1])


def _mixer_dilated(xb2, bsz, seq, w_in):
    d = xb2.shape[1]
    sub = seq // DIL_MAX
    assert seq % (DIL_MAX * 2 * LANES) == 0
    xp = xb2.reshape(bsz, sub, DIL_MAX, d).transpose(0, 2, 1, 3).reshape(bsz * seq, d)
    pos = (jnp.arange(sub)[None, :] * DIL_MAX + jnp.arange(DIL_MAX)[:, None]).reshape(-1)
    qk = _proj(xp, w_in[:, :2 * D_ATTN], pos, half=64, qcols=D_ATTN).reshape(bsz, DIL_MAX, sub, 2 * D_ATTN)
    v = _proj(xp, w_in[:, 2 * D_ATTN:], pos).reshape(bsz, DIL_MAX, sub, D_ATTN)
    prev = None
    for g, (window, dil) in enumerate(DILATED_GROUPS):
        assert window // dil == LANES and DIL_MAX % dil == 0
        prev = _band_stage(qk, v, prev, dil, g == len(DILATED_GROUPS) - 1)
    return prev.transpose(0, 2, 1, 3).reshape(bsz * seq, D_ATTN)


def _flash_body(*refs, kind, hp, tq):
    if kind == "fox":
        q_ref, k_ref, v_ref, cq_ref, ck_ref, o_ref = refs
    else:
        q_ref, k_ref, v_ref, o_ref, kmean_ref = refs
    hg = pl.program_id(1)
    i = pl.program_id(2)
    seq = k_ref.shape[1]
    nsl = tq // LANES
    rel = lax.broadcasted_iota(jnp.int32, (tq, LANES), 0) - lax.broadcasted_iota(jnp.int32, (tq, LANES), 1)
    causal = [rel >= c * LANES for c in range(nsl)]
    heads = []

    if kind == "moba":
        nblk = seq // MOBA_BLOCK
        spb = MOBA_BLOCK // LANES
        bpt = tq // MOBA_BLOCK

        @pl.when(i == 0)
        def _():
            kmean_ref[...] = jnp.zeros_like(kmean_ref)
            for hh in range(hp):
                hs = slice(hh * HEAD_DIM, (hh + 1) * HEAD_DIM)
                for n in range(nblk):
                    kb = k_ref[0, n * MOBA_BLOCK:(n + 1) * MOBA_BLOCK, hs].astype(F32)
                    kmean_ref[hh, n:n + 1, :] = jnp.sum(kb, axis=0, keepdims=True) / MOBA_BLOCK

        row_blk = lax.broadcasted_iota(jnp.int32, (tq, 1), 0) // MOBA_BLOCK
        own = (i * bpt + row_blk).astype(F32)
        blk_id = lax.broadcasted_iota(jnp.int32, (tq, LANES), 1).astype(F32)

    for hh in range(hp):
        hs = slice(hh * HEAD_DIM, (hh + 1) * HEAD_DIM)
        q = q_ref[0, :, hs]
        if kind == "fox":
            cq = jnp.broadcast_to(cq_ref[0, 0, :, hh:hh + 1], (tq, LANES))
            heads.append((hs, q, cq, hg * hp + hh))
        else:
            gate = lax.dot_general(q, kmean_ref[hh].astype(BF16), _DN_T, preferred_element_type=F32)
            gate = jnp.where(blk_id < own, gate, -jnp.inf)
            sel = jnp.zeros(gate.shape, F32)
            for _ in range(min(MOBA_TOPK, nblk)):
                mx = jnp.max(gate, axis=-1, keepdims=True)
                first = jnp.min(jnp.where(gate == mx, blk_id, float(LANES)), axis=-1, keepdims=True)
                pick = blk_id == first
                sel = jnp.where(pick & (first < own), 1.0, sel)
                gate = jnp.where(pick, -jnp.inf, gate)
            heads.append((hs, q, sel, None))

    def step(j, carry, diag):
        off = pl.multiple_of(j * tq, tq)
        out = []
        for (hs, q, aux, head), (m, l, acc) in zip(heads, carry):
            k = k_ref[0, pl.ds(off, tq), hs]
            v = v_ref[0, pl.ds(off, tq), hs]
            if kind == "fox":
                ck = ck_ref[0, pl.ds(head, 1), pl.ds(off, tq)]

                def col_fn(c, sc):
                    sc = sc + (aux - ck[:, c * LANES:(c + 1) * LANES])
                    return jnp.where(causal[c], sc, MASK_VALUE) if diag else sc
            else:
                rowsel = [jnp.max(jnp.where(blk_id == (j * bpt + n).astype(F32), aux, 0.0),
                                  axis=-1, keepdims=True) > 0.0 for n in range(bpt)]

                def col_fn(c, sc):
                    keep = rowsel[c // spb]
                    if diag:
                        keep = keep | ((row_blk == c // spb) & causal[c])
                    return jnp.where(keep, sc, MASK_VALUE)
            out.append(_attn_tile(q, k, v, m, l, acc, col_fn))
        return tuple(out)

    carry = tuple(_attn_init(tq) for _ in range(hp))
    carry = lax.fori_loop(0, i, functools.partial(step, diag=False), carry)
    carry = step(i, carry, True)
    for (hs, _, _, _), (m, l, acc) in zip(heads, carry):
        o_ref[0, :, hs] = _attn_finish(l, acc).astype(o_ref.dtype)


def _flash(kind, q_arr, q_off, k_arr, k_off, v_arr, v_off, extra, hp, tq):
    bsz, seq, _ = q_arr.shape
    wid = hp * HEAD_DIM
    hgs = N_HEADS // hp
    qo, ko, vo = q_off // hp, k_off // hp, v_off // hp
    in_specs = [pl.BlockSpec((1, tq, wid), lambda b, h, i: (b, i, qo + h)),
                pl.BlockSpec((1, seq, wid), lambda b, h, i: (b, 0, ko + h)),
                pl.BlockSpec((1, seq, wid), lambda b, h, i: (b, 0, vo + h))]
    args = [q_arr, k_arr, v_arr]
    scratch = []
    if kind == "fox":
        cum_col, cum_row = extra
        in_specs += [pl.BlockSpec((1, 1, tq, hp), lambda b, h, i: (b, h, i, 0)),
                     pl.BlockSpec((1, N_HEADS, seq), lambda b, h, i: (b, 0, 0))]
        args += [cum_col, cum_row]
    else:
        assert seq // MOBA_BLOCK <= LANES and tq % MOBA_BLOCK == 0
        scratch = [pltpu.VMEM((hp, LANES, HEAD_DIM), F32)]
    return pl.pallas_call(
        functools.partial(_flash_body, kind=kind, hp=hp, tq=tq),
        grid=(bsz, hgs, seq // tq),
        in_specs=in_specs,
        out_specs=pl.BlockSpec((1, tq, wid), lambda b, h, i: (b, i, h)),
        out_shape=jax.ShapeDtypeStruct((bsz, seq, D_ATTN), BF16),
        scratch_shapes=scratch,
        compiler_params=_cparams(("parallel", "parallel", "arbitrary")),
        name="flash_" + kind,
    )(*args)


def _cumsum_body(f_ref, b_ref, o_ref, carry_ref):
    j = pl.program_id(1)

    @pl.when(j == 0)
    def _():
        carry_ref[...] = jnp.zeros_like(carry_ref)

    ts = f_ref.shape[1]
    logf = jax.nn.log_sigmoid(f_ref[0] + b_ref[...])
    tri = (lax.broadcasted_iota(jnp.int32, (ts, ts), 0) >= lax.broadcasted_iota(jnp.int32, (ts, ts), 1)).astype(F32)
    cum = jnp.dot(tri, logf, precision=lax.Precision.HIGHEST, preferred_element_type=F32) + carry_ref[...]
    o_ref[0] = cum * LOG2E
    carry_ref[...] = cum[ts - 1:ts, :]


def _forget_cumsum(f_raw, b_pad, ts=256):
    bsz, seq, _ = f_raw.shape
    return pl.pallas_call(
        _cumsum_body,
        grid=(bsz, seq // ts),
        in_specs=[pl.BlockSpec((1, ts, LANES), lambda b, j: (b, j, 0)),
                  pl.BlockSpec((1, LANES), lambda b, j: (0, 0))],
        out_specs=pl.BlockSpec((1, ts, LANES), lambda b, j: (b, j, 0)),
        out_shape=jax.ShapeDtypeStruct((bsz, seq, LANES), F32),
        scratch_shapes=[pltpu.VMEM((1, LANES), F32)],
        compiler_params=_cparams(("parallel", "arbitrary")),
        name="forget_cumsum",
    )(f_raw, b_pad)


def _pad_cols(w, n):
    return jnp.pad(w, ((0, 0), (0, n - w.shape[1])))


def _mixer_fox(xb2, bsz, seq, w_in, b_forget, hp=2, tq=512):
    tq = min(tq, seq)
    pos = jnp.arange(seq)
    qkv = _proj(xb2, w_in[:, :3 * D_ATTN], pos, qcols=D_ATTN).reshape(bsz, seq, 3 * D_ATTN)
    f_raw = _proj(xb2, _pad_cols(w_in[:, 3 * D_ATTN:], LANES), pos, out_dtype=F32).reshape(bsz, seq, LANES)
    cum = _forget_cumsum(f_raw, _pad_cols(b_forget.reshape(1, N_HEADS), LANES))[..., :N_HEADS]
    cum_row = cum.transpose(0, 2, 1)
    cum_col = cum.reshape(bsz, seq, N_HEADS // hp, hp).transpose(0, 2, 1, 3)
    o = _flash("fox", qkv, 0, qkv, N_HEADS, qkv, 2 * N_HEADS, (cum_col, cum_row), hp, tq)
    return o.reshape(bsz * seq, D_ATTN)


def _mixer_moba(xb2, bsz, seq, w_in, hp=2, tq=512):
    tq = min(tq, seq)
    pos = jnp.arange(seq)
    qk = _proj(xb2, w_in[:, :2 * D_ATTN], pos, half=64, qcols=D_ATTN).reshape(bsz, seq, 2 * D_ATTN)
    v = _proj(xb2, w_in[:, 2 * D_ATTN:], pos).reshape(bsz, seq, D_ATTN)
    assert seq % MOBA_BLOCK == 0
    o = _flash("moba", qk, 0, qk, N_HEADS, v, 0, None, hp, tq)
    return o.reshape(bsz * seq, D_ATTN)


def _kiprep_body(r_ref, g_ref, b_ref, cos_ref, sin_ref, a_ref, b2_ref):
    x = r_ref[...]
    lane = lax.broadcasted_iota(jnp.int32, x.shape, 1)
    inside = lane < IDX_DIM
    mu = jnp.sum(jnp.where(inside, x, 0.0), axis=-1, keepdims=True) / IDX_DIM
    xc = jnp.where(inside, x - mu, 0.0)
    var = jnp.sum(xc * xc, axis=-1, keepdims=True) / IDX_DIM
    y = xc * lax.rsqrt(var + LN_EPS) * g_ref[...] + b_ref[...]
    y = y * cos_ref[...] + _rotate_half(y, IDX_DIM // 2) * sin_ref[...]
    y = jnp.where(inside, y, 0.0)
    a_ref[...] = y.astype(BF16)
    b2_ref[...] = pltpu.roll(y, IDX_DIM, 1).astype(BF16)


def _ki_prep(raw2, g, b, seq, tm=512):
    m = raw2.shape[0]
    tm = min(tm, seq)
    cos, sin = _rope_tables(jnp.arange(seq), IDX_DIM // 2)
    nsb = seq // tm
    row = lambda i: (i, 0)
    const = lambda i: (0, 0)
    return pl.pallas_call(
        _kiprep_body,
        grid=(m // tm,),
        in_specs=[pl.BlockSpec((tm, LANES), row), pl.BlockSpec((1, LANES), const), pl.BlockSpec((1, LANES), const),
                  pl.BlockSpec((tm, LANES), lambda i: (i % nsb, 0)), pl.BlockSpec((tm, LANES), lambda i: (i % nsb, 0))],
        out_specs=[pl.BlockSpec((tm, LANES), row)] * 2,
        out_shape=[jax.ShapeDtypeStruct((m, LANES), BF16)] * 2,
        compiler_params=_cparams(("parallel",)),
        name="dsa_ki_prep",
    )(raw2, _pad_cols(g.reshape(1, IDX_DIM), LANES), _pad_cols(b.reshape(1, IDX_DIM), LANES), cos, sin)


def _dsa_body(qi_ref, wr_ref, kia_ref, kib_ref, q_ref, k_ref, v_ref, o_ref, key_ref, bias_ref,
              *, tq, ch, topk, wscale):
    i = pl.program_id(1)
    seq = k_ref.shape[1]
    nch = (i * tq + tq + ch - 1) // ch
    nsl = ch // LANES
    wi = wr_ref[0][:, IDX_DIM:IDX_DIM + IDX_HEADS] * wscale
    rowpos = i * tq + lax.broadcasted_iota(jnp.int32, (tq, 1), 0)
    lane_ch = lax.broadcasted_iota(jnp.int32, (tq, ch), 1)
    lane_1 = lax.broadcasted_iota(jnp.int32, (tq, LANES), 1)

    def score_chunk(c, _):
        off = pl.multiple_of(c * ch, ch)
        ka = kia_ref[0, pl.ds(off, ch), :]
        kb = kib_ref[0, pl.ds(off, ch), :]
        sc = jnp.zeros((tq, ch), F32)
        for hpair in range(IDX_HEADS // 2):
            qp = qi_ref[0, :, hpair * LANES:(hpair + 1) * LANES]
            for t, kk in enumerate((ka, kb)):
                h = 2 * hpair + t
                lg = lax.dot_general(qp, kk, _DN_T, preferred_element_type=F32)
                sc = sc + wi[:, h:h + 1] * jnp.maximum(lg, 0.0)
        bits = pltpu.bitcast(sc, jnp.int32)
        bits = jnp.where(bits == INT_MIN, 0, bits)
        key = bits ^ ((bits >> 31) & 0x7FFFFFFF)
        key = jnp.where(off + lane_ch <= rowpos, key, NEG_INF_KEY)
        key_ref[:, pl.ds(off, ch)] = key
        return 0

    lax.fori_loop(0, nch, score_chunk, 0)

    def count(pred):
        def cb(c, acc):
            off = pl.multiple_of(c * ch, ch)
            blk = key_ref[:, pl.ds(off, ch)]
            for s_ in range(nsl):
                acc = acc + jnp.where(pred(blk[:, s_ * LANES:(s_ + 1) * LANES], off + s_ * LANES + lane_1), 1, 0)
            return acc
        acc = lax.fori_loop(0, nch, cb, jnp.zeros((tq, LANES), jnp.int32))
        return jnp.sum(acc.astype(F32), axis=-1, keepdims=True).astype(jnp.int32)

    def bisect_val(t, lo):
        cand = lo + (jnp.int32(1) << (31 - t))
        return jnp.where(count(lambda kv, idx: kv >= cand) >= topk, cand, lo)

    thr = lax.fori_loop(0, 32, bisect_val, jnp.full((tq, LANES), INT_MIN, jnp.int32))
    need = topk - count(lambda kv, idx: kv > thr)
    n_eq = count(lambda kv, idx: kv == thr)
    excess = (n_eq > need) & (thr[:, 0:1] > NEG_INF_KEY)
    any_excess = jnp.max(jnp.where(excess, 1.0, 0.0)) > 0.0

    def tie_path():
        def bisect_idx(t, lo):
            cand = lo + (jnp.int32(1) << (12 - t))
            return jnp.where(count(lambda kv, idx: (kv == thr) & (idx < cand)) < need, cand, lo)
        return lax.fori_loop(0, 13, bisect_idx, jnp.zeros((tq, LANES), jnp.int32))

    jmax = lax.cond(any_excess, tie_path, lambda: jnp.full((tq, LANES), seq, jnp.int32))

    def bias_chunk(c, _):
        off = pl.multiple_of(c * ch, ch)
        blk = key_ref[:, pl.ds(off, ch)]
        for s_ in range(nsl):
            kv = blk[:, s_ * LANES:(s_ + 1) * LANES]
            idx = off + s_ * LANES + lane_1
            keep = ((kv > thr) | ((kv == thr) & (idx <= jmax))) & (idx <= rowpos)
            bias_ref[:, pl.ds(pl.multiple_of(off + s_ * LANES, LANES), LANES)] = jnp.where(keep, 0.0, MASK_VALUE)
        return 0

    lax.fori_loop(0, nch, bias_chunk, 0)

    rows = GQA_GROUP * tq
    for g in range(B_KV_HEADS):
        gs = slice(g * HEAD_DIM, (g + 1) * HEAD_DIM)
        qg = jnp.concatenate(
            [q_ref[0, :, (g * GQA_GROUP + r) * HEAD_DIM:(g * GQA_GROUP + r + 1) * HEAD_DIM]
             for r in range(GQA_GROUP)], axis=0)

        def att(c, carry):
            off = pl.multiple_of(c * ch, ch)
            bias = bias_ref[:, pl.ds(off, ch)]

            def col_fn(s_, sc):
                return sc + jnp.concatenate([bias[:, s_ * LANES:(s_ + 1) * LANES]] * GQA_GROUP, axis=0)

            return _attn_tile(qg, k_ref[0, pl.ds(off, ch), gs], v_ref[0, pl.ds(off, ch), gs], *carry, col_fn)

        m, l, acc = lax.fori_loop(0, nch, att, _attn_init(rows))
        o = _attn_finish(l, acc)
        for r in range(GQA_GROUP):
            h = g * GQA_GROUP + r
            o_ref[0, :, h * HEAD_DIM:(h + 1) * HEAD_DIM] = o[r * tq:(r + 1) * tq].astype(o_ref.dtype)


def _mixer_dsa(xb2, bsz, seq, w_in, idx_g, idx_b, tq=256, ch=512):
    ch = min(ch, seq)
    pos = jnp.arange(seq)
    nqk = D_ATTN + KV_WIDTH
    qk = _proj(xb2, w_in[:, :nqk], pos, half=64, qcols=D_ATTN).reshape(bsz, seq, nqk)
    v = _proj(xb2, w_in[:, nqk:nqk + KV_WIDTH], pos).reshape(bsz, seq, KV_WIDTH)
    o_qi = nqk + KV_WIDTH
    n_qi = IDX_HEADS * IDX_DIM
    qi = _proj(xb2, w_in[:, o_qi:o_qi + n_qi], pos, half=IDX_DIM // 2).reshape(bsz, seq, n_qi)
    raw2 = _proj(xb2, _pad_cols(w_in[:, o_qi + n_qi:], LANES), pos, out_dtype=F32)
    kia, kib = _ki_prep(raw2, idx_g, idx_b, seq)
    topk = min(IDX_TOPK_MAX, seq // 4)
    body = functools.partial(_dsa_body, tq=tq, ch=ch, topk=topk, wscale=IDX_HEADS ** -0.5 * IDX_DIM ** -0.5)
    res = lambda b, i: (b, 0, 0)
    o = pl.pallas_call(
        body,
        grid=(bsz, seq // tq),
        in_specs=[pl.BlockSpec((1, tq, n_qi), lambda b, i: (b, i, 0)),
                  pl.BlockSpec((1, tq, LANES), lambda b, i: (b, i, 0)),
                  pl.BlockSpec((1, seq, LANES), res), pl.BlockSpec((1, seq, LANES), res),
                  pl.BlockSpec((1, tq, D_ATTN), lambda b, i: (b, i, 0)),
                  pl.BlockSpec((1, seq, KV_WIDTH), lambda b, i: (b, 0, D_ATTN // KV_WIDTH)),
                  pl.BlockSpec((1, seq, KV_WIDTH), res)],
        out_specs=pl.BlockSpec((1, tq, D_ATTN), lambda b, i: (b, i, 0)),
        out_shape=jax.ShapeDtypeStruct((bsz, seq, D_ATTN), BF16),
        scratch_shapes=[pltpu.VMEM((tq, seq), jnp.int32), pltpu.VMEM((tq, seq), F32)],
        compiler_params=_cparams(("parallel", "arbitrary")),
        name="dsa_select_attend",
    )(qi, raw2.reshape(bsz, seq, LANES), kia.reshape(bsz, seq, LANES), kib.reshape(bsz, seq, LANES), qk, qk, v)
    return o.reshape(bsz * seq, D_ATTN)


def _trunk(x, layers):
    bsz, seq, d = x.shape
    x2 = x.reshape(bsz * seq, d)
    xb2 = x2
    for kind, p in enumerate(layers):
        w_in = p["w_in"].astype(BF16)
        if kind == 0:
            a = _mixer_dilated(xb2, bsz, seq, w_in)
        elif kind == 1:
            a = _mixer_dsa(xb2, bsz, seq, w_in, p["idx_norm_g"], p["idx_norm_b"])
        elif kind == 2:
            a = _mixer_fox(xb2, bsz, seq, w_in, p["b_forget"])
        else:
            a = _mixer_moba(xb2, bsz, seq, w_in)
        x1, x1b, logits = _outln(a, p["w_out"].astype(BF16), x2, p["ln1_g"], p["ln1_b"],
                                 p["router_w"].astype(BF16), p["router_b"])
        x2, xb2 = _moe_layer(x1, x1b, logits, p["w_gu"].astype(BF16), p["b_gu"], p["w_dn"].astype(BF16),
                             p["b_dn"], p["ln2_g"], p["ln2_b"])
    return x2.reshape(bsz, seq, d)


def kernel(x, l0_w_in, l0_w_out, l0_ln1_g, l0_ln1_b, l0_router_w, l0_router_b, l0_w_gu, l0_b_gu, l0_w_dn, l0_b_dn, l0_ln2_g, l0_ln2_b, l1_w_in, l1_idx_norm_g, l1_idx_norm_b, l1_w_out, l1_ln1_g, l1_ln1_b, l1_router_w, l1_router_b, l1_w_gu, l1_b_gu, l1_w_dn, l1_b_dn, l1_ln2_g, l1_ln2_b, l2_w_in, l2_b_forget, l2_w_out, l2_ln1_g, l2_ln1_b, l2_router_w, l2_router_b, l2_w_gu, l2_b_gu, l2_w_dn, l2_b_dn, l2_ln2_g, l2_ln2_b, l3_w_in, l3_w_out, l3_ln1_g, l3_ln1_b, l3_router_w, l3_router_b, l3_w_gu, l3_b_gu, l3_w_dn, l3_b_dn, l3_ln2_g, l3_ln2_b):
    names = ("w_out", "ln1_g", "ln1_b", "router_w", "router_b", "w_gu", "b_gu", "w_dn", "b_dn", "ln2_g", "ln2_b")
    l0 = dict(zip(("w_in",) + names, (l0_w_in, l0_w_out, l0_ln1_g, l0_ln1_b, l0_router_w, l0_router_b,
                                      l0_w_gu, l0_b_gu, l0_w_dn, l0_b_dn, l0_ln2_g, l0_ln2_b)))
    l1 = dict(zip(("w_in", "idx_norm_g", "idx_norm_b") + names,
                  (l1_w_in, l1_idx_norm_g, l1_idx_norm_b, l1_w_out, l1_ln1_g, l1_ln1_b, l1_router_w, l1_router_b,
                   l1_w_gu, l1_b_gu, l1_w_dn, l1_b_dn, l1_ln2_g, l1_ln2_b)))
    l2 = dict(zip(("w_in", "b_forget") + names,
                  (l2_w_in, l2_b_forget, l2_w_out, l2_ln1_g, l2_ln1_b, l2_router_w, l2_router_b,
                   l2_w_gu, l2_b_gu, l2_w_dn, l2_b_dn, l2_ln2_g, l2_ln2_b)))
    l3 = dict(zip(("w_in",) + names, (l3_w_in, l3_w_out, l3_ln1_g, l3_ln1_b, l3_router_w, l3_router_b,
                                      l3_w_gu, l3_b_gu, l3_w_dn, l3_b_dn, l3_ln2_g, l3_ln2_b)))
    return _trunk(x, (l0, l1, l2, l3))
```

```python
import functools

import jax
import jax.numpy as jnp
import numpy as np
from jax import lax
from jax.experimental import pallas as pl
from jax.experimental.pallas import tpu as pltpu

N_HEADS = 16
HEAD_DIM = 128
D_MODEL = 2048
D_ATTN = N_HEADS * HEAD_DIM
ROPE_THETA = 10000.0
MASK_VALUE = -1e30
LN_EPS = 1e-5
DILATED_GROUPS = ((128, 1), (512, 4), (2048, 16))
DIL_MAX = 16
B_KV_HEADS = 4
GQA_GROUP = N_HEADS // B_KV_HEADS
KV_WIDTH = B_KV_HEADS * HEAD_DIM
IDX_HEADS = 16
IDX_DIM = 64
IDX_TOPK_MAX = 256
MOBA_BLOCK = 256
MOBA_TOPK = 3
N_EXPERTS = 32
TOP_K = 4
D_EXPERT = 1024
SWIGLU_ALPHA = 1.702
SWIGLU_LIMIT = 7.0
DEPTH = 4
DEEPNORM_ALPHA = (2 * DEPTH) ** 0.25

LANES = 128
VMEM_LIMIT = 56 * 1024 * 1024
MOE_TM = 256
INT_MIN = -(2 ** 31)
NEG_INF_KEY = -2139095041
LOG2E = 1.4426950408889634
QSCALE = HEAD_DIM ** -0.5 * LOG2E

BF16 = jnp.bfloat16
F32 = jnp.float32
_DN_T = (((1,), (1,)), ((), ()))


def _cparams(sem):
    return pltpu.CompilerParams(dimension_semantics=sem, vmem_limit_bytes=VMEM_LIMIT)


def _rope_tables(pos, half):
    inv_freq = ROPE_THETA ** (-jnp.arange(half, dtype=F32) / half)
    ang = pos.astype(F32)[:, None] * inv_freq[None, :]
    cos = jnp.tile(jnp.cos(ang), (1, LANES // half))
    sin = jnp.tile(jnp.concatenate([-jnp.sin(ang), jnp.sin(ang)], -1), (1, LANES // (2 * half)))
    return cos, sin


def _rotate_half(t, half):
    if 2 * half == LANES:
        return pltpu.roll(t, half, 1)
    lane = lax.broadcasted_iota(jnp.int32, t.shape, 1)
    first = (lane % (2 * half)) < half
    return jnp.where(first, pltpu.roll(t, LANES - half, 1), pltpu.roll(t, half, 1))


def _proj_body(*refs, half, q_tiles):
    if half:
        x_ref, w_ref, cos_ref, sin_ref, o_ref = refs
    else:
        x_ref, w_ref, o_ref = refs
    acc = jnp.dot(x_ref[...].astype(BF16), w_ref[...], preferred_element_type=F32)
    if q_tiles:
        acc = acc * jnp.where(pl.program_id(1) < q_tiles, QSCALE, 1.0)
    if not half:
        o_ref[...] = acc.astype(o_ref.dtype)
        return
    cos = cos_ref[...]
    sin = sin_ref[...]
    for c in range(acc.shape[1] // LANES):
        t = acc[:, c * LANES:(c + 1) * LANES]
        o_ref[:, c * LANES:(c + 1) * LANES] = (t * cos + _rotate_half(t, half) * sin).astype(o_ref.dtype)


def _proj(x2, w, pos, half=0, out_dtype=BF16, tm=512, qcols=0):
    m, k = x2.shape
    n = w.shape[1]
    seq = pos.shape[0]
    tm = min(tm, seq)
    tn = 512 if n % 512 == 0 else (256 if n % 256 == 0 else LANES)
    assert qcols % tn == 0 and seq % tm == 0
    in_specs = [pl.BlockSpec((tm, k), lambda i, j: (i, 0)),
                pl.BlockSpec((k, tn), lambda i, j: (0, j))]
    args = [x2, w]
    if half:
        cos, sin = _rope_tables(pos, half)
        nsb = seq // tm
        in_specs += [pl.BlockSpec((tm, LANES), lambda i, j: (i % nsb, 0))] * 2
        args += [cos, sin]
    return pl.pallas_call(
        functools.partial(_proj_body, half=half, q_tiles=qcols // tn),
        grid=(m // tm, n // tn),
        in_specs=in_specs,
        out_specs=pl.BlockSpec((tm, tn), lambda i, j: (i, j)),
        out_shape=jax.ShapeDtypeStruct((m, n), out_dtype),
        compiler_params=_cparams(("parallel", "arbitrary")),
        name="proj_rope%d" % half,
    )(*args)


def _layer_norm_rows(z, g, b):
    mu = jnp.mean(z, axis=-1, keepdims=True)
    zc = z - mu
    var = jnp.mean(zc * zc, axis=-1, keepdims=True)
    return zc * lax.rsqrt(var + LN_EPS) * g + b


def _outln_body(a_ref, w_ref, x_ref, g_ref, b_ref, rw_ref, rb_ref, xo_ref, xb_ref, lg_ref):
    h = jnp.dot(a_ref[...], w_ref[...], preferred_element_type=F32)
    y = _layer_norm_rows(DEEPNORM_ALPHA * x_ref[...] + h, g_ref[...], b_ref[...])
    xo_ref[...] = y
    yb = y.astype(BF16)
    xb_ref[...] = yb
    lg_ref[...] = jnp.dot(yb, rw_ref[...], preferred_element_type=F32) + rb_ref[...]


def _outln(a, w_out, x2, g, b, rw, rb, tm=256):
    m, d = x2.shape
    row = lambda i: (i, 0)
    const = lambda i: (0, 0)
    return pl.pallas_call(
        _outln_body,
        grid=(m // tm,),
        in_specs=[pl.BlockSpec((tm, d), row), pl.BlockSpec((d, d), const), pl.BlockSpec((tm, d), row),
                  pl.BlockSpec((1, d), const), pl.BlockSpec((1, d), const),
                  pl.BlockSpec((d, N_EXPERTS), const), pl.BlockSpec((1, N_EXPERTS), const)],
        out_specs=[pl.BlockSpec((tm, d), row), pl.BlockSpec((tm, d), row),
                   pl.BlockSpec((tm, N_EXPERTS), row)],
        out_shape=[jax.ShapeDtypeStruct((m, d), F32), jax.ShapeDtypeStruct((m, d), BF16),
                   jax.ShapeDtypeStruct((m, N_EXPERTS), F32)],
        compiler_params=_cparams(("parallel",)),
        name="outproj_ln_router",
    )(a, w_out, x2, g.reshape(1, d), b.reshape(1, d), rw, rb.reshape(1, N_EXPERTS))


def _moe_body(be_ref, nb_ref, x_ref, wgu_ref, bgu_ref, wdn_ref, bdn_ref, o_ref, wgu_b, wdn_b):
    i = pl.program_id(0)
    used = i < nb_ref[0]
    new_expert = (i == 0) | (be_ref[i] != be_ref[jnp.maximum(i - 1, 0)])

    @pl.when(used & new_expert)
    def _():
        rows = 256
        for r in range(0, wgu_b.shape[0], rows):
            wgu_b[r:r + rows, :] = wgu_ref[0, r:r + rows, :].astype(BF16)
        for r in range(0, wdn_b.shape[0], rows):
            wdn_b[r:r + rows, :] = wdn_ref[0, r:r + rows, :].astype(BF16)

    @pl.when(used)
    def _():
        h = jnp.dot(x_ref[...], wgu_b[...], preferred_element_type=F32) + bgu_ref[0]
        glu = jnp.minimum(h[:, :D_EXPERT], SWIGLU_LIMIT)
        lin = jnp.clip(h[:, D_EXPERT:], -SWIGLU_LIMIT, SWIGLU_LIMIT)
        act = glu * jax.nn.sigmoid(SWIGLU_ALPHA * glu) * (lin + 1.0)
        o_ref[...] = jnp.dot(act.astype(BF16), wdn_b[...], preferred_element_type=F32) + bdn_ref[0]

    @pl.when(jnp.logical_not(used))
    def _():
        o_ref[...] = jnp.zeros_like(o_ref)


def _moe_experts(x_rows, block_exp, n_used, w_gu, b_gu, w_dn, b_dn):
    n_rows, d = x_rows.shape
    n_blocks = n_rows // MOE_TM
    once = pl.Buffered(1)
    grid_spec = pltpu.PrefetchScalarGridSpec(
        num_scalar_prefetch=2,
        grid=(n_blocks,),
        in_specs=[pl.BlockSpec((MOE_TM, d), lambda i, be, nb: (i, 0)),
                  pl.BlockSpec((1, d, 2 * D_EXPERT), lambda i, be, nb: (be[i], 0, 0), pipeline_mode=once),
                  pl.BlockSpec((1, 1, 2 * D_EXPERT), lambda i, be, nb: (be[i], 0, 0)),
                  pl.BlockSpec((1, D_EXPERT, d), lambda i, be, nb: (be[i], 0, 0), pipeline_mode=once),
                  pl.BlockSpec((1, 1, d), lambda i, be, nb: (be[i], 0, 0))],
        out_specs=pl.BlockSpec((MOE_TM, d), lambda i, be, nb: (i, 0)),
        scratch_shapes=[pltpu.VMEM((d, 2 * D_EXPERT), BF16), pltpu.VMEM((D_EXPERT, d), BF16)],
    )
    return pl.pallas_call(
        _moe_body,
        grid_spec=grid_spec,
        out_shape=jax.ShapeDtypeStruct((n_rows, d), F32),
        compiler_params=_cparams(("arbitrary",)),
        name="moe_experts",
    )(block_exp, n_used, x_rows, w_gu, b_gu.reshape(N_EXPERTS, 1, -1), w_dn, b_dn.reshape(N_EXPERTS, 1, -1))


def _combine_body(*refs):
    y_refs = refs[:TOP_K]
    gt_ref, x_ref, g_ref, b_ref, xo_ref, xb_ref = refs[TOP_K:]
    gt = gt_ref[...]
    y = gt[:, 0:1] * y_refs[0][...]
    for k in range(1, TOP_K):
        y = y + gt[:, k:k + 1] * y_refs[k][...]
    out = _layer_norm_rows(DEEPNORM_ALPHA * x_ref[...] + y, g_ref[...], b_ref[...])
    xo_ref[...] = out
    xb_ref[...] = out.astype(BF16)


def _combine_ln(y4, gates, x2, g, b, tm=256):
    m, d = x2.shape
    nb = m // tm
    row = lambda i: (i, 0)
    const = lambda i: (0, 0)
    y_specs = [pl.BlockSpec((tm, d), functools.partial(lambda i, k: (k * nb + i, 0), k=k)) for k in range(TOP_K)]
    return pl.pallas_call(
        _combine_body,
        grid=(nb,),
        in_specs=y_specs + [pl.BlockSpec((tm, TOP_K), row),
                            pl.BlockSpec((tm, d), row), pl.BlockSpec((1, d), const), pl.BlockSpec((1, d), const)],
        out_specs=[pl.BlockSpec((tm, d), row), pl.BlockSpec((tm, d), row)],
        out_shape=[jax.ShapeDtypeStruct((m, d), F32), jax.ShapeDtypeStruct((m, d), BF16)],
        compiler_params=_cparams(("parallel",)),
        name="moe_combine_ln",
    )(*([y4] * TOP_K), gates, x2, g.reshape(1, d), b.reshape(1, d))


def _moe_layer(x1, x1b, logits, w_gu, b_gu, w_dn, b_dn, g, b):
    n_tok, d = x1.shape
    n_assign = n_tok * TOP_K
    top_logit, top_exp = lax.top_k(logits, TOP_K)
    gates = jax.nn.softmax(top_logit, axis=-1)
    flat_exp = top_exp.reshape(-1).astype(jnp.int32)
    order = jnp.argsort(flat_exp).astype(jnp.int32)
    rank = jnp.argsort(order).astype(jnp.int32)
    experts = jnp.arange(N_EXPERTS, dtype=jnp.int32)
    counts = jnp.sum((flat_exp[:, None] == experts[None, :]).astype(jnp.int32), axis=0)
    padded = (counts + MOE_TM - 1) // MOE_TM * MOE_TM
    start = jnp.cumsum(counts) - counts
    pad_end = jnp.cumsum(padded)
    pad_start = pad_end - padded
    shift = (pad_start - start).astype(jnp.int32)
    pos = rank + shift[flat_exp]
    n_blocks = n_assign // MOE_TM + N_EXPERTS
    n_rows = n_blocks * MOE_TM
    block_start = jnp.arange(n_blocks, dtype=jnp.int32) * MOE_TM
    block_exp = jnp.minimum(jnp.sum((pad_end[None, :] <= block_start[:, None]).astype(jnp.int32), axis=1),
                            N_EXPERTS - 1).astype(jnp.int32)
    n_used = (pad_end[-1] // MOE_TM).astype(jnp.int32).reshape(1)
    row = jnp.arange(n_rows, dtype=jnp.int32)
    row_exp = jnp.repeat(block_exp, MOE_TM)
    valid = (row - pad_start[row_exp].astype(jnp.int32)) < counts[row_exp]
    row_tok = jnp.where(valid, order[jnp.clip(row - shift[row_exp], 0, n_assign - 1)] // TOP_K, 0)
    x_rows = x1b[row_tok]
    y_rows = _moe_experts(x_rows, block_exp, n_used, w_gu, b_gu, w_dn, b_dn)
    y4 = y_rows[pos.reshape(n_tok, TOP_K).T.reshape(-1)]
    return _combine_ln(y4, gates, x1, g, b)


def _attn_tile(q, k, v, m, l, acc, col_fn):
    s = lax.dot_general(q, k, _DN_T, preferred_element_type=F32)
    cols = [col_fn(c, s[:, c * LANES:(c + 1) * LANES]) for c in range(s.shape[1] // LANES)]
    m_new = jnp.maximum(m, jnp.max(functools.reduce(jnp.maximum, cols), axis=-1, keepdims=True))
    alpha = jnp.exp2(m - m_new)
    ps = [jnp.exp2(c - m_new) for c in cols]
    l = alpha * l + functools.reduce(jnp.add, ps)
    p = jnp.concatenate([x.astype(BF16) for x in ps], axis=1)
    acc = alpha * acc + jnp.dot(p, v, preferred_element_type=F32)
    return m_new, l, acc


def _attn_init(rows):
    return (jnp.full((rows, LANES), -jnp.inf, F32), jnp.zeros((rows, LANES), F32),
            jnp.zeros((rows, HEAD_DIM), F32))


def _attn_finish(l, acc):
    return acc / jnp.sum(l, axis=-1, keepdims=True)


def _band_body(*refs, has_prev, is_last, planes, rows):
    if has_prev:
        q_ref, kc_ref, kp_ref, vc_ref, vp_ref, oi_ref, li_ref = refs[:7]
        outs = refs[7:]
    else:
        q_ref, kc_ref, kp_ref, vc_ref, vp_ref = refs[:5]
        outs = refs[5:]
    o_ref = outs[0]
    j = pl.program_id(2)
    w = planes * rows

    def local(t):
        return (t % rows) * planes + t // rows

    tq = lax.broadcasted_iota(jnp.int32, (w, 2 * w), 0)
    tk = lax.broadcasted_iota(jnp.int32, (w, 2 * w), 1)
    cur = tk >= w
    dist = (w + local(tq)) - (local(tk % w) + jnp.where(cur, w, 0))
    mask = (dist >= 0) & (dist <= LANES) & (cur | (j > 0))
    lane = lax.broadcasted_iota(jnp.int32, (w, LANES), 1)
    lse_all = jnp.zeros((w, LANES), F32)

    def blk(ref, hs):
        return ref[0, :, 0, :, hs].reshape(w, HEAD_DIM)

    if has_prev:
        lse_in = li_ref[0, :, 0].reshape(w, LANES)

    for h in range(N_HEADS):
        hs = slice(h * HEAD_DIM, (h + 1) * HEAD_DIM)
        kcat = jnp.concatenate([blk(kp_ref, hs), blk(kc_ref, hs)], axis=0)
        vcat = jnp.concatenate([blk(vp_ref, hs), blk(vc_ref, hs)], axis=0)
        s = lax.dot_general(blk(q_ref, hs), kcat, _DN_T, preferred_element_type=F32)
        s = jnp.where(mask, s, MASK_VALUE)
        m = jnp.max(s, axis=-1, keepdims=True)
        p = jnp.exp2(s - m)
        l = jnp.sum(p, axis=-1, keepdims=True)
        o = jnp.dot((p / l).astype(BF16), vcat, preferred_element_type=F32)
        lse = m + jnp.log2(l)
        if has_prev:
            lse_prev = lse_in[:, h:h + 1]
            mx = jnp.maximum(lse, lse_prev)
            e_new = jnp.exp2(lse - mx)
            e_old = jnp.exp2(lse_prev - mx)
            tot = e_new + e_old
            o = (e_new * o + e_old * blk(oi_ref, hs)) / tot
            lse = mx + jnp.log2(tot)
        o_ref[0, :, 0, :, hs] = o.astype(o_ref.dtype).reshape(planes, rows, HEAD_DIM)
        lse_all = jnp.where(lane == h, lse, lse_all)
    if not is_last:
        outs[1][0, :, 0] = lse_all.reshape(planes, rows, LANES)


def _band_stage(qk, v, prev, dil, is_last):
    bsz, _, sub, _ = v.shape
    planes = DIL_MAX // dil
    groups = DIL_MAX // planes
    rows = (LANES if planes < DIL_MAX else 2 * LANES) // planes
    view = lambda a: a.reshape(bsz, planes, groups, sub, a.shape[-1])
    nb = sub // rows
    blk = lambda col, prevblk: pl.BlockSpec(
        (1, planes, 1, rows, D_ATTN),
        (lambda b, g, j: (b, 0, g, jnp.maximum(j - 1, 0), col)) if prevblk else (lambda b, g, j: (b, 0, g, j, col)))
    in_specs = [blk(0, False), blk(1, False), blk(1, True), blk(0, False), blk(0, True)]
    args = [view(qk)] * 3 + [view(v)] * 2
    o_spec = blk(0, False)
    l_spec = pl.BlockSpec((1, planes, 1, rows, LANES), lambda b, g, j: (b, 0, g, j, 0))
    if prev is not None:
        in_specs += [o_spec, l_spec]
        args += [view(prev[0]), view(prev[1])]
    out_specs = [o_spec] if is_last else [o_spec, l_spec]
    out_shape = [jax.ShapeDtypeStruct((bsz, planes, groups, sub, D_ATTN), BF16 if is_last else F32)]
    if not is_last:
        out_shape.append(jax.ShapeDtypeStruct((bsz, planes, groups, sub, LANES), F32))
    outs = pl.pallas_call(
        functools.partial(_band_body, has_prev=prev is not None, is_last=is_last, planes=planes, rows=rows),
        grid=(bsz, groups, nb),
        in_specs=in_specs, out_specs=out_specs, out_shape=out_shape,
        compiler_params=_cparams(("parallel", "parallel", "arbitrary")),
        name="dilated_band_d%d" % dil,
    )(*args)
    unview = lambda a: a.reshape(bsz, DIL_MAX, sub, a.shape[-1])
    if is_last:
        return unview(outs[0])
    return unview(outs[0]), unview(outs[1])


def _mixer_dilated(xb2, bsz, seq, w_in):
    d = xb2.shape[1]
    sub = seq // DIL_MAX
    assert seq % (DIL_MAX * 2 * LANES) == 0
    xp = xb2.reshape(bsz, sub, DIL_MAX, d).transpose(0, 2, 1, 3).reshape(bsz * seq, d)
    pos = (jnp.arange(sub)[None, :] * DIL_MAX + jnp.arange(DIL_MAX)[:, None]).reshape(-1)
    qk = _proj(xp, w_in[:, :2 * D_ATTN], pos, half=64, qcols=D_ATTN).reshape(bsz, DIL_MAX, sub, 2 * D_ATTN)
    v = _proj(xp, w_in[:, 2 * D_ATTN:], pos).reshape(bsz, DIL_MAX, sub, D_ATTN)
    prev = None
    for g, (window, dil) in enumerate(DILATED_GROUPS):
        assert window // dil == LANES and DIL_MAX % dil == 0
        prev = _band_stage(qk, v, prev, dil, g == len(DILATED_GROUPS) - 1)
    return prev.transpose(0, 2, 1, 3).reshape(bsz * seq, D_ATTN)


def _flash_body(*refs, kind, hp, tq):
    if kind == "fox":
        q_ref, k_ref, v_ref, cq_ref, ck_ref, o_ref = refs
    else:
        q_ref, k_ref, v_ref, o_ref, kmean_ref = refs
    hg = pl.program_id(1)
    i = pl.program_id(2)
    seq = k_ref.shape[1]
    nsl = tq // LANES
    rel = lax.broadcasted_iota(jnp.int32, (tq, LANES), 0) - lax.broadcasted_iota(jnp.int32, (tq, LANES), 1)
    causal = [rel >= c * LANES for c in range(nsl)]
    heads = []

    if kind == "moba":
        nblk = seq // MOBA_BLOCK
        spb = MOBA_BLOCK // LANES
        bpt = tq // MOBA_BLOCK

        @pl.when(i == 0)
        def _():
            kmean_ref[...] = jnp.zeros_like(kmean_ref)
            for hh in range(hp):
                hs = slice(hh * HEAD_DIM, (hh + 1) * HEAD_DIM)
                for n in range(nblk):
                    kb = k_ref[0, n * MOBA_BLOCK:(n + 1) * MOBA_BLOCK, hs].astype(F32)
                    kmean_ref[hh, n:n + 1, :] = jnp.sum(kb, axis=0, keepdims=True) / MOBA_BLOCK

        row_blk = lax.broadcasted_iota(jnp.int32, (tq, 1), 0) // MOBA_BLOCK
        own = (i * bpt + row_blk).astype(F32)
        blk_id = lax.broadcasted_iota(jnp.int32, (tq, LANES), 1).astype(F32)

    for hh in range(hp):
        hs = slice(hh * HEAD_DIM, (hh + 1) * HEAD_DIM)
        q = q_ref[0, :, hs]
        if kind == "fox":
            cq = jnp.broadcast_to(cq_ref[0, 0, :, hh:hh + 1], (tq, LANES))
            heads.append((hs, q, cq, hg * hp + hh))
        else:
            gate = lax.dot_general(q, kmean_ref[hh].astype(BF16), _DN_T, preferred_element_type=F32)
            gate = jnp.where(blk_id < own, gate, -jnp.inf)
            sel = jnp.zeros(gate.shape, F32)
            for _ in range(min(MOBA_TOPK, nblk)):
                mx = jnp.max(gate, axis=-1, keepdims=True)
                first = jnp.min(jnp.where(gate == mx, blk_id, float(LANES)), axis=-1, keepdims=True)
                pick = blk_id == first
                sel = jnp.where(pick & (first < own), 1.0, sel)
                gate = jnp.where(pick, -jnp.inf, gate)
            heads.append((hs, q, sel, None))

    def step(j, carry, diag):
        off = pl.multiple_of(j * tq, tq)
        out = []
        for (hs, q, aux, head), (m, l, acc) in zip(heads, carry):
            k = k_ref[0, pl.ds(off, tq), hs]
            v = v_ref[0, pl.ds(off, tq), hs]
            if kind == "fox":
                ck = ck_ref[0, pl.ds(head, 1), pl.ds(off, tq)]

                def col_fn(c, sc):
                    sc = sc + (aux - ck[:, c * LANES:(c + 1) * LANES])
                    return jnp.where(causal[c], sc, MASK_VALUE) if diag else sc
            else:
                rowsel = [jnp.max(jnp.where(blk_id == (j * bpt + n).astype(F32), aux, 0.0),
                                  axis=-1, keepdims=True) > 0.0 for n in range(bpt)]

                def col_fn(c, sc):
                    keep = rowsel[c // spb]
                    if diag:
                        keep = keep | ((row_blk == c // spb) & causal[c])
                    return jnp.where(keep, sc, MASK_VALUE)
            out.append(_attn_tile(q, k, v, m, l, acc, col_fn))
        return tuple(out)

    carry = tuple(_attn_init(tq) for _ in range(hp))
    carry = lax.fori_loop(0, i, functools.partial(step, diag=False), carry)
    carry = step(i, carry, True)
    for (hs, _, _, _), (m, l, acc) in zip(heads, carry):
        o_ref[0, :, hs] = _attn_finish(l, acc).astype(o_ref.dtype)


def _flash(kind, q_arr, q_off, k_arr, k_off, v_arr, v_off, extra, hp, tq):
    bsz, seq, _ = q_arr.shape
    wid = hp * HEAD_DIM
    hgs = N_HEADS // hp
    qo, ko, vo = q_off // hp, k_off // hp, v_off // hp
    in_specs = [pl.BlockSpec((1, tq, wid), lambda b, h, i: (b, i, qo + h)),
                pl.BlockSpec((1, seq, wid), lambda b, h, i: (b, 0, ko + h)),
                pl.BlockSpec((1, seq, wid), lambda b, h, i: (b, 0, vo + h))]
    args = [q_arr, k_arr, v_arr]
    scratch = []
    if kind == "fox":
        cum_col, cum_row = extra
        in_specs += [pl.BlockSpec((1, 1, tq, hp), lambda b, h, i: (b, h, i, 0)),
                     pl.BlockSpec((1, N_HEADS, seq), lambda b, h, i: (b, 0, 0))]
        args += [cum_col, cum_row]
    else:
        assert seq // MOBA_BLOCK <= LANES and tq % MOBA_BLOCK == 0
        scratch = [pltpu.VMEM((hp, LANES, HEAD_DIM), F32)]
    return pl.pallas_call(
        functools.partial(_flash_body, kind=kind, hp=hp, tq=tq),
        grid=(bsz, hgs, seq // tq),
        in_specs=in_specs,
        out_specs=pl.BlockSpec((1, tq, wid), lambda b, h, i: (b, i, h)),
        out_shape=jax.ShapeDtypeStruct((bsz, seq, D_ATTN), BF16),
        scratch_shapes=scratch,
        compiler_params=_cparams(("parallel", "parallel", "arbitrary")),
        name="flash_" + kind,
    )(*args)


def _cumsum_body(f_ref, b_ref, o_ref, carry_ref):
    j = pl.program_id(1)

    @pl.when(j == 0)
    def _():
        carry_ref[...] = jnp.zeros_like(carry_ref)

    ts = f_ref.shape[1]
    logf = jax.nn.log_sigmoid(f_ref[0] + b_ref[...])
    tri = (lax.broadcasted_iota(jnp.int32, (ts, ts), 0) >= lax.broadcasted_iota(jnp.int32, (ts, ts), 1)).astype(F32)
    cum = jnp.dot(tri, logf, precision=lax.Precision.HIGHEST, preferred_element_type=F32) + carry_ref[...]
    o_ref[0] = cum * LOG2E
    carry_ref[...] = cum[ts - 1:ts, :]


def _forget_cumsum(f_raw, b_pad, ts=256):
    bsz, seq, _ = f_raw.shape
    return pl.pallas_call(
        _cumsum_body,
        grid=(bsz, seq // ts),
        in_specs=[pl.BlockSpec((1, ts, LANES), lambda b, j: (b, j, 0)),
                  pl.BlockSpec((1, LANES), lambda b, j: (0, 0))],
        out_specs=pl.BlockSpec((1, ts, LANES), lambda b, j: (b, j, 0)),
        out_shape=jax.ShapeDtypeStruct((bsz, seq, LANES), F32),
        scratch_shapes=[pltpu.VMEM((1, LANES), F32)],
        compiler_params=_cparams(("parallel", "arbitrary")),
        name="forget_cumsum",
    )(f_raw, b_pad)


def _pad_cols(w, n):
    return jnp.pad(w, ((0, 0), (0, n - w.shape[1])))


def _mixer_fox(xb2, bsz, seq, w_in, b_forget, hp=2, tq=512):
    tq = min(tq, seq)
    pos = jnp.arange(seq)
    qkv = _proj(xb2, w_in[:, :3 * D_ATTN], pos, qcols=D_ATTN).reshape(bsz, seq, 3 * D_ATTN)
    f_raw = _proj(xb2, _pad_cols(w_in[:, 3 * D_ATTN:], LANES), pos, out_dtype=F32).reshape(bsz, seq, LANES)
    cum = _forget_cumsum(f_raw, _pad_cols(b_forget.reshape(1, N_HEADS), LANES))[..., :N_HEADS]
    cum_row = cum.transpose(0, 2, 1)
    cum_col = cum.reshape(bsz, seq, N_HEADS // hp, hp).transpose(0, 2, 1, 3)
    o = _flash("fox", qkv, 0, qkv, N_HEADS, qkv, 2 * N_HEADS, (cum_col, cum_row), hp, tq)
    return o.reshape(bsz * seq, D_ATTN)


def _mixer_moba(xb2, bsz, seq, w_in, hp=2, tq=512):
    tq = min(tq, seq)
    pos = jnp.arange(seq)
    qk = _proj(xb2, w_in[:, :2 * D_ATTN], pos, half=64, qcols=D_ATTN).reshape(bsz, seq, 2 * D_ATTN)
    v = _proj(xb2, w_in[:, 2 * D_ATTN:], pos).reshape(bsz, seq, D_ATTN)
    assert seq % MOBA_BLOCK == 0
    o = _flash("moba", qk, 0, qk, N_HEADS, v, 0, None, hp, tq)
    return o.reshape(bsz * seq, D_ATTN)


def _kiprep_body(r_ref, g_ref, b_ref, cos_ref, sin_ref, a_ref, b2_ref):
    x = r_ref[...]
    lane = lax.broadcasted_iota(jnp.int32, x.shape, 1)
    inside = lane < IDX_DIM
    mu = jnp.sum(jnp.where(inside, x, 0.0), axis=-1, keepdims=True) / IDX_DIM
    xc = jnp.where(inside, x - mu, 0.0)
    var = jnp.sum(xc * xc, axis=-1, keepdims=True) / IDX_DIM
    y = xc * lax.rsqrt(var + LN_EPS) * g_ref[...] + b_ref[...]
    y = y * cos_ref[...] + _rotate_half(y, IDX_DIM // 2) * sin_ref[...]
    y = jnp.where(inside, y, 0.0)
    a_ref[...] = y.astype(BF16)
    b2_ref[...] = pltpu.roll(y, IDX_DIM, 1).astype(BF16)


def _ki_prep(raw2, g, b, seq, tm=512):
    m = raw2.shape[0]
    tm = min(tm, seq)
    cos, sin = _rope_tables(jnp.arange(seq), IDX_DIM // 2)
    nsb = seq // tm
    row = lambda i: (i, 0)
    const = lambda i: (0, 0)
    return pl.pallas_call(
        _kiprep_body,
        grid=(m // tm,),
        in_specs=[pl.BlockSpec((tm, LANES), row), pl.BlockSpec((1, LANES), const), pl.BlockSpec((1, LANES), const),
                  pl.BlockSpec((tm, LANES), lambda i: (i % nsb, 0)), pl.BlockSpec((tm, LANES), lambda i: (i % nsb, 0))],
        out_specs=[pl.BlockSpec((tm, LANES), row)] * 2,
        out_shape=[jax.ShapeDtypeStruct((m, LANES), BF16)] * 2,
        compiler_params=_cparams(("parallel",)),
        name="dsa_ki_prep",
    )(raw2, _pad_cols(g.reshape(1, IDX_DIM), LANES), _pad_cols(b.reshape(1, IDX_DIM), LANES), cos, sin)


def _dsa_body(qi_ref, wr_ref, kia_ref, kib_ref, q_ref, k_ref, v_ref, o_ref, key_ref, bias_ref,
              *, tq, ch, topk, wscale):
    i = pl.program_id(1)
    seq = k_ref.shape[1]
    nch = (i * tq + tq + ch - 1) // ch
    nsl = ch // LANES
    wi = wr_ref[0][:, IDX_DIM:IDX_DIM + IDX_HEADS] * wscale
    rowpos = i * tq + lax.broadcasted_iota(jnp.int32, (tq, 1), 0)
    lane_ch = lax.broadcasted_iota(jnp.int32, (tq, ch), 1)
    lane_1 = lax.broadcasted_iota(jnp.int32, (tq, LANES), 1)

    def score_chunk(c, _):
        off = pl.multiple_of(c * ch, ch)
        ka = kia_ref[0, pl.ds(off, ch), :]
        kb = kib_ref[0, pl.ds(off, ch), :]
        sc = jnp.zeros((tq, ch), F32)
        for hpair in range(IDX_HEADS // 2):
            qp = qi_ref[0, :, hpair * LANES:(hpair + 1) * LANES]
            for t, kk in enumerate((ka, kb)):
                h = 2 * hpair + t
                lg = lax.dot_general(qp, kk, _DN_T, preferred_element_type=F32)
                sc = sc + wi[:, h:h + 1] * jnp.maximum(lg, 0.0)
        bits = pltpu.bitcast(sc, jnp.int32)
        bits = jnp.where(bits == INT_MIN, 0, bits)
        key = bits ^ ((bits >> 31) & 0x7FFFFFFF)
        key = jnp.where(off + lane_ch <= rowpos, key, NEG_INF_KEY)
        key_ref[:, pl.ds(off, ch)] = key
        return 0

    lax.fori_loop(0, nch, score_chunk, 0)

    def count(pred):
        def cb(c, acc):
            off = pl.multiple_of(c * ch, ch)
            blk = key_ref[:, pl.ds(off, ch)]
            for s_ in range(nsl):
                acc = acc + jnp.where(pred(blk[:, s_ * LANES:(s_ + 1) * LANES], off + s_ * LANES + lane_1), 1, 0)
            return acc
        acc = lax.fori_loop(0, nch, cb, jnp.zeros((tq, LANES), jnp.int32))
        return jnp.sum(acc.astype(F32), axis=-1, keepdims=True).astype(jnp.int32)

    def bisect_val(t, lo):
        cand = lo + (jnp.int32(1) << (31 - t))
        return jnp.where(count(lambda kv, idx: kv >= cand) >= topk, cand, lo)

    thr = lax.fori_loop(0, 32, bisect_val, jnp.full((tq, LANES), INT_MIN, jnp.int32))
    need = topk - count(lambda kv, idx: kv > thr)
    n_eq = count(lambda kv, idx: kv == thr)
    excess = (n_eq > need) & (thr[:, 0:1] > NEG_INF_KEY)
    any_excess = jnp.max(jnp.where(excess, 1.0, 0.0)) > 0.0

    def tie_path():
        def bisect_idx(t, lo):
            cand = lo + (jnp.int32(1) << (12 - t))
            return jnp.where(count(lambda kv, idx: (kv == thr) & (idx < cand)) < need, cand, lo)
        return lax.fori_loop(0, 13, bisect_idx, jnp.zeros((tq, LANES), jnp.int32))

    jmax = lax.cond(any_excess, tie_path, lambda: jnp.full((tq, LANES), seq, jnp.int32))

    def bias_chunk(c, _):
        off = pl.multiple_of(c * ch, ch)
        blk = key_ref[:, pl.ds(off, ch)]
        for s_ in range(nsl):
            kv = blk[:, s_ * LANES:(s_ + 1) * LANES]
            idx = off + s_ * LANES + lane_1
            keep = ((kv > thr) | ((kv == thr) & (idx <= jmax))) & (idx <= rowpos)
            bias_ref[:, pl.ds(pl.multiple_of(off + s_ * LANES, LANES), LANES)] = jnp.where(keep, 0.0, MASK_VALUE)
        return 0

    lax.fori_loop(0, nch, bias_chunk, 0)

    rows = GQA_GROUP * tq
    for g in range(B_KV_HEADS):
        gs = slice(g * HEAD_DIM, (g + 1) * HEAD_DIM)
        qg = jnp.concatenate(
            [q_ref[0, :, (g * GQA_GROUP + r) * HEAD_DIM:(g * GQA_GROUP + r + 1) * HEAD_DIM]
             for r in range(GQA_GROUP)], axis=0)

        def att(c, carry):
            off = pl.multiple_of(c * ch, ch)
            bias = bias_ref[:, pl.ds(off, ch)]

            def col_fn(s_, sc):
                return sc + jnp.concatenate([bias[:, s_ * LANES:(s_ + 1) * LANES]] * GQA_GROUP, axis=0)

            return _attn_tile(qg, k_ref[0, pl.ds(off, ch), gs], v_ref[0, pl.ds(off, ch), gs], *carry, col_fn)

        m, l, acc = lax.fori_loop(0, nch, att, _attn_init(rows))
        o = _attn_finish(l, acc)
        for r in range(GQA_GROUP):
            h = g * GQA_GROUP + r
            o_ref[0, :, h * HEAD_DIM:(h + 1) * HEAD_DIM] = o[r * tq:(r + 1) * tq].astype(o_ref.dtype)


def _mixer_dsa(xb2, bsz, seq, w_in, idx_g, idx_b, tq=256, ch=512):
    ch = min(ch, seq)
    pos = jnp.arange(seq)
    nqk = D_ATTN + KV_WIDTH
    qk = _proj(xb2, w_in[:, :nqk], pos, half=64, qcols=D_ATTN).reshape(bsz, seq, nqk)
    v = _proj(xb2, w_in[:, nqk:nqk + KV_WIDTH], pos).reshape(bsz, seq, KV_WIDTH)
    o_qi = nqk + KV_WIDTH
    n_qi = IDX_HEADS * IDX_DIM
    qi = _proj(xb2, w_in[:, o_qi:o_qi + n_qi], pos, half=IDX_DIM // 2).reshape(bsz, seq, n_qi)
    raw2 = _proj(xb2, _pad_cols(w_in[:, o_qi + n_qi:], LANES), pos, out_dtype=F32)
    kia, kib = _ki_prep(raw2, idx_g, idx_b, seq)
    topk = min(IDX_TOPK_MAX, seq // 4)
    body = functools.partial(_dsa_body, tq=tq, ch=ch, topk=topk, wscale=IDX_HEADS ** -0.5 * IDX_DIM ** -0.5)
    res = lambda b, i: (b, 0, 0)
    o = pl.pallas_call(
        body,
        grid=(bsz, seq // tq),
        in_specs=[pl.BlockSpec((1, tq, n_qi), lambda b, i: (b, i, 0)),
                  pl.BlockSpec((1, tq, LANES), lambda b, i: (b, i, 0)),
                  pl.BlockSpec((1, seq, LANES), res), pl.BlockSpec((1, seq, LANES), res),
                  pl.BlockSpec((1, tq, D_ATTN), lambda b, i: (b, i, 0)),
                  pl.BlockSpec((1, seq, KV_WIDTH), lambda b, i: (b, 0, D_ATTN // KV_WIDTH)),
                  pl.BlockSpec((1, seq, KV_WIDTH), res)],
        out_specs=pl.BlockSpec((1, tq, D_ATTN), lambda b, i: (b, i, 0)),
        out_shape=jax.ShapeDtypeStruct((bsz, seq, D_ATTN), BF16),
        scratch_shapes=[pltpu.VMEM((tq, seq), jnp.int32), pltpu.VMEM((tq, seq), F32)],
        compiler_params=_cparams(("parallel", "arbitrary")),
        name="dsa_select_attend",
    )(qi, raw2.reshape(bsz, seq, LANES), kia.reshape(bsz, seq, LANES), kib.reshape(bsz, seq, LANES), qk, qk, v)
    return o.reshape(bsz * seq, D_ATTN)


def _trunk(x, layers):
    bsz, seq, d = x.shape
    x2 = x.reshape(bsz * seq, d)
    xb2 = x2
    for kind, p in enumerate(layers):
        w_in = p["w_in"].astype(BF16)
        if kind == 0:
            a = _mixer_dilated(xb2, bsz, seq, w_in)
        elif kind == 1:
            a = _mixer_dsa(xb2, bsz, seq, w_in, p["idx_norm_g"], p["idx_norm_b"])
        elif kind == 2:
            a = _mixer_fox(xb2, bsz, seq, w_in, p["b_forget"])
        else:
            a = _mixer_moba(xb2, bsz, seq, w_in)
        x1, x1b, logits = _outln(a, p["w_out"].astype(BF16), x2, p["ln1_g"], p["ln1_b"],
                                 p["router_w"].astype(BF16), p["router_b"])
        x2, xb2 = _moe_layer(x1, x1b, logits, p["w_gu"], p["b_gu"], p["w_dn"], p["b_dn"], p["ln2_g"], p["ln2_b"])
    return x2.reshape(bsz, seq, d)


def kernel(x, l0_w_in, l0_w_out, l0_ln1_g, l0_ln1_b, l0_router_w, l0_router_b, l0_w_gu, l0_b_gu, l0_w_dn, l0_b_dn, l0_ln2_g, l0_ln2_b, l1_w_in, l1_idx_norm_g, l1_idx_norm_b, l1_w_out, l1_ln1_g, l1_ln1_b, l1_router_w, l1_router_b, l1_w_gu, l1_b_gu, l1_w_dn, l1_b_dn, l1_ln2_g, l1_ln2_b, l2_w_in, l2_b_forget, l2_w_out, l2_ln1_g, l2_ln1_b, l2_router_w, l2_router_b, l2_w_gu, l2_b_gu, l2_w_dn, l2_b_dn, l2_ln2_g, l2_ln2_b, l3_w_in, l3_w_out, l3_ln1_g, l3_ln1_b, l3_router_w, l3_router_b, l3_w_gu, l3_b_gu, l3_w_dn, l3_b_dn, l3_ln2_g, l3_ln2_b):
    names = ("w_out", "ln1_g", "ln1_b", "router_w", "router_b", "w_gu", "b_gu", "w_dn", "b_dn", "ln2_g", "ln2_b")
    l0 = dict(zip(("w_in",) + names, (l0_w_in, l0_w_out, l0_ln1_g, l0_ln1_b, l0_router_w, l0_router_b,
                                      l0_w_gu, l0_b_gu, l0_w_dn, l0_b_dn, l0_ln2_g, l0_ln2_b)))
    l1 = dict(zip(("w_in", "idx_norm_g", "idx_norm_b") + names,
                  (l1_w_in, l1_idx_norm_g, l1_idx_norm_b, l1_w_out, l1_ln1_g, l1_ln1_b, l1_router_w, l1_router_b,
                   l1_w_gu, l1_b_gu, l1_w_dn, l1_b_dn, l1_ln2_g, l1_ln2_b)))
    l2 = dict(zip(("w_in", "b_forget") + names,
                  (l2_w_in, l2_b_forget, l2_w_out, l2_ln1_g, l2_ln1_b, l2_router_w, l2_router_b,
                   l2_w_gu, l2_b_gu, l2_w_dn, l2_b_dn, l2_ln2_g, l2_ln2_b)))
    l3 = dict(zip(("w_in",) + names, (l3_w_in, l3_w_out, l3_ln1_g, l3_ln1_b, l3_router_w, l3_router_b,
                                      l3_w_gu, l3_b_gu, l3_w_dn, l3_b_dn, l3_ln2_g, l3_ln2_b)))
    return _trunk(x, (l0, l1, l2, l3))
```

```python
import functools

import jax
import jax.numpy as jnp
import numpy as np
from jax import lax
from jax.experimental import pallas as pl
from jax.experimental.pallas import tpu as pltpu

N_HEADS = 16
HEAD_DIM = 128
D_MODEL = 2048
D_ATTN = N_HEADS * HEAD_DIM
ROPE_THETA = 10000.0
MASK_VALUE = -1e30
LN_EPS = 1e-5
DILATED_GROUPS = ((128, 1), (512, 4), (2048, 16))
DIL_MAX = 16
B_KV_HEADS = 4
GQA_GROUP = N_HEADS // B_KV_HEADS
KV_WIDTH = B_KV_HEADS * HEAD_DIM
IDX_HEADS = 16
IDX_DIM = 64
IDX_TOPK_MAX = 256
MOBA_BLOCK = 256
MOBA_TOPK = 3
N_EXPERTS = 32
TOP_K = 4
D_EXPERT = 1024
SWIGLU_ALPHA = 1.702
SWIGLU_LIMIT = 7.0
DEPTH = 4
DEEPNORM_ALPHA = (2 * DEPTH) ** 0.25

LANES = 128
VMEM_LIMIT = 56 * 1024 * 1024
MOE_TM = 256
INT_MIN = -(2 ** 31)
NEG_INF_KEY = -2139095041
LOG2E = 1.4426950408889634
QSCALE = HEAD_DIM ** -0.5 * LOG2E

BF16 = jnp.bfloat16
F32 = jnp.float32
_DN_T = (((1,), (1,)), ((), ()))


def _cparams(sem):
    return pltpu.CompilerParams(dimension_semantics=sem, vmem_limit_bytes=VMEM_LIMIT)


def _rope_tables(pos, half):
    inv_freq = ROPE_THETA ** (-jnp.arange(half, dtype=F32) / half)
    ang = pos.astype(F32)[:, None] * inv_freq[None, :]
    cos = jnp.tile(jnp.cos(ang), (1, LANES // half))
    sin = jnp.tile(jnp.concatenate([-jnp.sin(ang), jnp.sin(ang)], -1), (1, LANES // (2 * half)))
    return cos, sin


def _rotate_half(t, half):
    if 2 * half == LANES:
        return pltpu.roll(t, half, 1)
    lane = lax.broadcasted_iota(jnp.int32, t.shape, 1)
    first = (lane % (2 * half)) < half
    return jnp.where(first, pltpu.roll(t, LANES - half, 1), pltpu.roll(t, half, 1))


def _proj_body(*refs, half, q_tiles):
    if half:
        x_ref, w_ref, cos_ref, sin_ref, o_ref = refs
    else:
        x_ref, w_ref, o_ref = refs
    acc = jnp.dot(x_ref[...].astype(BF16), w_ref[...], preferred_element_type=F32)
    if q_tiles:
        acc = acc * jnp.where(pl.program_id(1) < q_tiles, QSCALE, 1.0)
    if not half:
        o_ref[...] = acc.astype(o_ref.dtype)
        return
    cos = cos_ref[...]
    sin = sin_ref[...]
    for c in range(acc.shape[1] // LANES):
        t = acc[:, c * LANES:(c + 1) * LANES]
        o_ref[:, c * LANES:(c + 1) * LANES] = (t * cos + _rotate_half(t, half) * sin).astype(o_ref.dtype)


def _proj(x2, w, pos, half=0, out_dtype=BF16, tm=512, qcols=0):
    m, k = x2.shape
    n = w.shape[1]
    seq = pos.shape[0]
    tm = min(tm, seq)
    tn = 512 if n % 512 == 0 else (256 if n % 256 == 0 else LANES)
    assert qcols % tn == 0 and seq % tm == 0
    in_specs = [pl.BlockSpec((tm, k), lambda i, j: (i, 0)),
                pl.BlockSpec((k, tn), lambda i, j: (0, j))]
    args = [x2, w]
    if half:
        cos, sin = _rope_tables(pos, half)
        nsb = seq // tm
        in_specs += [pl.BlockSpec((tm, LANES), lambda i, j: (i % nsb, 0))] * 2
        args += [cos, sin]
    return pl.pallas_call(
        functools.partial(_proj_body, half=half, q_tiles=qcols // tn),
        grid=(m // tm, n // tn),
        in_specs=in_specs,
        out_specs=pl.BlockSpec((tm, tn), lambda i, j: (i, j)),
        out_shape=jax.ShapeDtypeStruct((m, n), out_dtype),
        compiler_params=_cparams(("parallel", "arbitrary")),
        name="proj_rope%d" % half,
    )(*args)


def _layer_norm_rows(z, g, b):
    mu = jnp.mean(z, axis=-1, keepdims=True)
    zc = z - mu
    var = jnp.mean(zc * zc, axis=-1, keepdims=True)
    return zc * lax.rsqrt(var + LN_EPS) * g + b


def _outln_body(a_ref, w_ref, x_ref, g_ref, b_ref, rw_ref, rb_ref, xo_ref, xb_ref, lg_ref):
    h = jnp.dot(a_ref[...], w_ref[...], preferred_element_type=F32)
    y = _layer_norm_rows(DEEPNORM_ALPHA * x_ref[...] + h, g_ref[...], b_ref[...])
    xo_ref[...] = y
    yb = y.astype(BF16)
    xb_ref[...] = yb
    lg_ref[...] = jnp.dot(yb, rw_ref[...], preferred_element_type=F32) + rb_ref[...]


def _outln(a, w_out, x2, g, b, rw, rb, tm=256):
    m, d = x2.shape
    row = lambda i: (i, 0)
    const = lambda i: (0, 0)
    return pl.pallas_call(
        _outln_body,
        grid=(m // tm,),
        in_specs=[pl.BlockSpec((tm, d), row), pl.BlockSpec((d, d), const), pl.BlockSpec((tm, d), row),
                  pl.BlockSpec((1, d), const), pl.BlockSpec((1, d), const),
                  pl.BlockSpec((d, N_EXPERTS), const), pl.BlockSpec((1, N_EXPERTS), const)],
        out_specs=[pl.BlockSpec((tm, d), row), pl.BlockSpec((tm, d), row),
                   pl.BlockSpec((tm, N_EXPERTS), row)],
        out_shape=[jax.ShapeDtypeStruct((m, d), F32), jax.ShapeDtypeStruct((m, d), BF16),
                   jax.ShapeDtypeStruct((m, N_EXPERTS), F32)],
        compiler_params=_cparams(("parallel",)),
        name="outproj_ln_router",
    )(a, w_out, x2, g.reshape(1, d), b.reshape(1, d), rw, rb.reshape(1, N_EXPERTS))


def _moe_body(be_ref, nb_ref, x_ref, wgu_ref, bgu_ref, wdn_ref, bdn_ref, o_ref, wgu_b, wdn_b):
    i = pl.program_id(0)
    used = i < nb_ref[0]
    new_expert = (i == 0) | (be_ref[i] != be_ref[jnp.maximum(i - 1, 0)])

    @pl.when(used & new_expert)
    def _():
        rows = 256
        for r in range(0, wgu_b.shape[0], rows):
            wgu_b[r:r + rows, :] = wgu_ref[0, r:r + rows, :].astype(BF16)
        for r in range(0, wdn_b.shape[0], rows):
            wdn_b[r:r + rows, :] = wdn_ref[0, r:r + rows, :].astype(BF16)

    @pl.when(used)
    def _():
        h = jnp.dot(x_ref[...], wgu_b[...], preferred_element_type=F32) + bgu_ref[0]
        glu = jnp.minimum(h[:, :D_EXPERT], SWIGLU_LIMIT)
        lin = jnp.clip(h[:, D_EXPERT:], -SWIGLU_LIMIT, SWIGLU_LIMIT)
        act = glu * jax.nn.sigmoid(SWIGLU_ALPHA * glu) * (lin + 1.0)
        y = jnp.dot(act.astype(BF16), wdn_b[...], preferred_element_type=F32) + bdn_ref[0]
        o_ref[...] = y.astype(o_ref.dtype)

    @pl.when(jnp.logical_not(used))
    def _():
        o_ref[...] = jnp.zeros_like(o_ref)


def _moe_experts(x_rows, block_exp, n_used, w_gu, b_gu, w_dn, b_dn):
    n_rows, d = x_rows.shape
    n_blocks = n_rows // MOE_TM
    once = pl.Buffered(1)
    grid_spec = pltpu.PrefetchScalarGridSpec(
        num_scalar_prefetch=2,
        grid=(n_blocks,),
        in_specs=[pl.BlockSpec((MOE_TM, d), lambda i, be, nb: (i, 0)),
                  pl.BlockSpec((1, d, 2 * D_EXPERT), lambda i, be, nb: (be[i], 0, 0), pipeline_mode=once),
                  pl.BlockSpec((1, 1, 2 * D_EXPERT), lambda i, be, nb: (be[i], 0, 0)),
                  pl.BlockSpec((1, D_EXPERT, d), lambda i, be, nb: (be[i], 0, 0), pipeline_mode=once),
                  pl.BlockSpec((1, 1, d), lambda i, be, nb: (be[i], 0, 0))],
        out_specs=pl.BlockSpec((MOE_TM, d), lambda i, be, nb: (i, 0)),
        scratch_shapes=[pltpu.VMEM((d, 2 * D_EXPERT), BF16), pltpu.VMEM((D_EXPERT, d), BF16)],
    )
    return pl.pallas_call(
        _moe_body,
        grid_spec=grid_spec,
        out_shape=jax.ShapeDtypeStruct((n_rows, d), BF16),
        compiler_params=_cparams(("arbitrary",)),
        name="moe_experts",
    )(block_exp, n_used, x_rows, w_gu, b_gu.reshape(N_EXPERTS, 1, -1), w_dn, b_dn.reshape(N_EXPERTS, 1, -1))


def _combine_body(*refs):
    y_refs = refs[:TOP_K]
    gt_ref, x_ref, g_ref, b_ref, xo_ref, xb_ref = refs[TOP_K:]
    gt = gt_ref[...]
    y = gt[:, 0:1] * y_refs[0][...].astype(F32)
    for k in range(1, TOP_K):
        y = y + gt[:, k:k + 1] * y_refs[k][...].astype(F32)
    out = _layer_norm_rows(DEEPNORM_ALPHA * x_ref[...] + y, g_ref[...], b_ref[...])
    xo_ref[...] = out
    xb_ref[...] = out.astype(BF16)


def _combine_ln(y4, gates, x2, g, b, tm=256):
    m, d = x2.shape
    nb = m // tm
    row = lambda i: (i, 0)
    const = lambda i: (0, 0)
    y_specs = [pl.BlockSpec((tm, d), functools.partial(lambda i, k: (k * nb + i, 0), k=k)) for k in range(TOP_K)]
    return pl.pallas_call(
        _combine_body,
        grid=(nb,),
        in_specs=y_specs + [pl.BlockSpec((tm, TOP_K), row),
                            pl.BlockSpec((tm, d), row), pl.BlockSpec((1, d), const), pl.BlockSpec((1, d), const)],
        out_specs=[pl.BlockSpec((tm, d), row), pl.BlockSpec((tm, d), row)],
        out_shape=[jax.ShapeDtypeStruct((m, d), F32), jax.ShapeDtypeStruct((m, d), BF16)],
        compiler_params=_cparams(("parallel",)),
        name="moe_combine_ln",
    )(*([y4] * TOP_K), gates, x2, g.reshape(1, d), b.reshape(1, d))


def _moe_layer(x1, x1b, logits, w_gu, b_gu, w_dn, b_dn, g, b):
    n_tok, d = x1.shape
    n_assign = n_tok * TOP_K
    top_logit, top_exp = lax.top_k(logits, TOP_K)
    gates = jax.nn.softmax(top_logit, axis=-1)
    flat_exp = top_exp.reshape(-1).astype(jnp.int32)
    order = jnp.argsort(flat_exp).astype(jnp.int32)
    rank = jnp.argsort(order).astype(jnp.int32)
    experts = jnp.arange(N_EXPERTS, dtype=jnp.int32)
    counts = jnp.sum((flat_exp[:, None] == experts[None, :]).astype(jnp.int32), axis=0)
    padded = (counts + MOE_TM - 1) // MOE_TM * MOE_TM
    start = jnp.cumsum(counts) - counts
    pad_end = jnp.cumsum(padded)
    pad_start = pad_end - padded
    shift = (pad_start - start).astype(jnp.int32)
    pos = rank + shift[flat_exp]
    n_blocks = n_assign // MOE_TM + N_EXPERTS
    n_rows = n_blocks * MOE_TM
    block_start = jnp.arange(n_blocks, dtype=jnp.int32) * MOE_TM
    block_exp = jnp.minimum(jnp.sum((pad_end[None, :] <= block_start[:, None]).astype(jnp.int32), axis=1),
                            N_EXPERTS - 1).astype(jnp.int32)
    n_used = (pad_end[-1] // MOE_TM).astype(jnp.int32).reshape(1)
    tok = jnp.arange(n_assign, dtype=jnp.int32) // TOP_K
    row_tok = jnp.zeros((n_rows,), jnp.int32).at[pos].set(tok)
    x_rows = x1b[row_tok]
    y_rows = _moe_experts(x_rows, block_exp, n_used, w_gu, b_gu, w_dn, b_dn)
    y4 = y_rows[pos.reshape(n_tok, TOP_K).T.reshape(-1)]
    return _combine_ln(y4, gates, x1, g, b)


def _attn_tile(q, k, v, m, l, acc, col_fn):
    s = lax.dot_general(q, k, _DN_T, preferred_element_type=F32)
    cols = [col_fn(c, s[:, c * LANES:(c + 1) * LANES]) for c in range(s.shape[1] // LANES)]
    m_new = jnp.maximum(m, jnp.max(functools.reduce(jnp.maximum, cols), axis=-1, keepdims=True))
    alpha = jnp.exp2(m - m_new)
    ps = [jnp.exp2(c - m_new) for c in cols]
    l = alpha * l + functools.reduce(jnp.add, ps)
    p = jnp.concatenate([x.astype(BF16) for x in ps], axis=1)
    acc = alpha * acc + jnp.dot(p, v, preferred_element_type=F32)
    return m_new, l, acc


def _attn_init(rows):
    return (jnp.full((rows, LANES), -jnp.inf, F32), jnp.zeros((rows, LANES), F32),
            jnp.zeros((rows, HEAD_DIM), F32))


def _attn_finish(l, acc):
    return acc / jnp.sum(l, axis=-1, keepdims=True)


def _band_body(*refs, has_prev, is_last, planes, rows):
    if has_prev:
        q_ref, kc_ref, kp_ref, vc_ref, vp_ref, oi_ref, li_ref = refs[:7]
        outs = refs[7:]
    else:
        q_ref, kc_ref, kp_ref, vc_ref, vp_ref = refs[:5]
        outs = refs[5:]
    o_ref = outs[0]
    j = pl.program_id(2)
    w = planes * rows

    def local(t):
        return (t % rows) * planes + t // rows

    tq = lax.broadcasted_iota(jnp.int32, (w, 2 * w), 0)
    tk = lax.broadcasted_iota(jnp.int32, (w, 2 * w), 1)
    cur = tk >= w
    dist = (w + local(tq)) - (local(tk % w) + jnp.where(cur, w, 0))
    mask = (dist >= 0) & (dist <= LANES) & (cur | (j > 0))
    lane = lax.broadcasted_iota(jnp.int32, (w, LANES), 1)
    lse_all = jnp.zeros((w, LANES), F32)

    def blk(ref, hs):
        return ref[0, :, 0, :, hs].reshape(w, HEAD_DIM)

    if has_prev:
        lse_in = li_ref[0, :, 0].reshape(w, LANES)

    for h in range(N_HEADS):
        hs = slice(h * HEAD_DIM, (h + 1) * HEAD_DIM)
        kcat = jnp.concatenate([blk(kp_ref, hs), blk(kc_ref, hs)], axis=0)
        vcat = jnp.concatenate([blk(vp_ref, hs), blk(vc_ref, hs)], axis=0)
        s = lax.dot_general(blk(q_ref, hs), kcat, _DN_T, preferred_element_type=F32)
        s = jnp.where(mask, s, MASK_VALUE)
        m = jnp.max(s, axis=-1, keepdims=True)
        p = jnp.exp2(s - m)
        l = jnp.sum(p, axis=-1, keepdims=True)
        o = jnp.dot((p / l).astype(BF16), vcat, preferred_element_type=F32)
        lse = m + jnp.log2(l)
        if has_prev:
            lse_prev = lse_in[:, h:h + 1]
            mx = jnp.maximum(lse, lse_prev)
            e_new = jnp.exp2(lse - mx)
            e_old = jnp.exp2(lse_prev - mx)
            tot = e_new + e_old
            o = (e_new * o + e_old * blk(oi_ref, hs)) / tot
            lse = mx + jnp.log2(tot)
        o_ref[0, :, 0, :, hs] = o.astype(o_ref.dtype).reshape(planes, rows, HEAD_DIM)
        lse_all = jnp.where(lane == h, lse, lse_all)
    if not is_last:
        outs[1][0, :, 0] = lse_all.reshape(planes, rows, LANES)


def _band_stage(qk, v, prev, dil, is_last):
    bsz, _, sub, _ = v.shape
    planes = DIL_MAX // dil
    groups = DIL_MAX // planes
    rows = (LANES if planes < DIL_MAX else 2 * LANES) // planes
    view = lambda a: a.reshape(bsz, planes, groups, sub, a.shape[-1])
    nb = sub // rows
    blk = lambda col, prevblk: pl.BlockSpec(
        (1, planes, 1, rows, D_ATTN),
        (lambda b, g, j: (b, 0, g, jnp.maximum(j - 1, 0), col)) if prevblk else (lambda b, g, j: (b, 0, g, j, col)))
    in_specs = [blk(0, False), blk(1, False), blk(1, True), blk(0, False), blk(0, True)]
    args = [view(qk)] * 3 + [view(v)] * 2
    o_spec = blk(0, False)
    l_spec = pl.BlockSpec((1, planes, 1, rows, LANES), lambda b, g, j: (b, 0, g, j, 0))
    if prev is not None:
        in_specs += [o_spec, l_spec]
        args += [view(prev[0]), view(prev[1])]
    out_specs = [o_spec] if is_last else [o_spec, l_spec]
    out_shape = [jax.ShapeDtypeStruct((bsz, planes, groups, sub, D_ATTN), BF16 if is_last else F32)]
    if not is_last:
        out_shape.append(jax.ShapeDtypeStruct((bsz, planes, groups, sub, LANES), F32))
    outs = pl.pallas_call(
        functools.partial(_band_body, has_prev=prev is not None, is_last=is_last, planes=planes, rows=rows),
        grid=(bsz, groups, nb),
        in_specs=in_specs, out_specs=out_specs, out_shape=out_shape,
        compiler_params=_cparams(("parallel", "parallel", "arbitrary")),
        name="dilated_band_d%d" % dil,
    )(*args)
    unview = lambda a: a.reshape(bsz, DIL_MAX, sub, a.shape[-1])
    if is_last:
        return unview(outs[0])
    return unview(outs[0]), unview(outs[1])


def _mixer_dilated(xb2, bsz, seq, w_in):
    d = xb2.shape[1]
    sub = seq // DIL_MAX
    assert seq % (DIL_MAX * 2 * LANES) == 0
    xp = xb2.reshape(bsz, sub, DIL_MAX, d).transpose(0, 2, 1, 3).reshape(bsz * seq, d)
    pos = (jnp.arange(sub)[None, :] * DIL_MAX + jnp.arange(DIL_MAX)[:, None]).reshape(-1)
    qk = _proj(xp, w_in[:, :2 * D_ATTN], pos, half=64, qcols=D_ATTN).reshape(bsz, DIL_MAX, sub, 2 * D_ATTN)
    v = _proj(xp, w_in[:, 2 * D_ATTN:], pos).reshape(bsz, DIL_MAX, sub, D_ATTN)
    prev = None
    for g, (window, dil) in enumerate(DILATED_GROUPS):
        assert window // dil == LANES and DIL_MAX % dil == 0
        prev = _band_stage(qk, v, prev, dil, g == len(DILATED_GROUPS) - 1)
    return prev.transpose(0, 2, 1, 3).reshape(bsz * seq, D_ATTN)


def _flash_body(*refs, kind, hp, tq):
    if kind == "fox":
        q_ref, k_ref, v_ref, cq_ref, ck_ref, o_ref = refs
    else:
        q_ref, k_ref, v_ref, o_ref, kmean_ref = refs
    hg = pl.program_id(1)
    i = pl.program_id(2)
    seq = k_ref.shape[1]
    nsl = tq // LANES
    rel = lax.broadcasted_iota(jnp.int32, (tq, LANES), 0) - lax.broadcasted_iota(jnp.int32, (tq, LANES), 1)
    causal = [rel >= c * LANES for c in range(nsl)]
    heads = []

    if kind == "moba":
        nblk = seq // MOBA_BLOCK
        spb = MOBA_BLOCK // LANES
        bpt = tq // MOBA_BLOCK

        @pl.when(i == 0)
        def _():
            kmean_ref[...] = jnp.zeros_like(kmean_ref)
            for hh in range(hp):
                hs = slice(hh * HEAD_DIM, (hh + 1) * HEAD_DIM)
                for n in range(nblk):
                    kb = k_ref[0, n * MOBA_BLOCK:(n + 1) * MOBA_BLOCK, hs].astype(F32)
                    kmean_ref[hh, n:n + 1, :] = jnp.sum(kb, axis=0, keepdims=True) / MOBA_BLOCK

        row_blk = lax.broadcasted_iota(jnp.int32, (tq, 1), 0) // MOBA_BLOCK
        own = (i * bpt + row_blk).astype(F32)
        blk_id = lax.broadcasted_iota(jnp.int32, (tq, LANES), 1).astype(F32)

    for hh in range(hp):
        hs = slice(hh * HEAD_DIM, (hh + 1) * HEAD_DIM)
        q = q_ref[0, :, hs]
        if kind == "fox":
            cq = jnp.broadcast_to(cq_ref[0, 0, :, hh:hh + 1], (tq, LANES))
            heads.append((hs, q, cq, hg * hp + hh))
        else:
            gate = lax.dot_general(q, kmean_ref[hh].astype(BF16), _DN_T, preferred_element_type=F32)
            gate = jnp.where(blk_id < own, gate, -jnp.inf)
            sel = jnp.zeros(gate.shape, F32)
            for _ in range(min(MOBA_TOPK, nblk)):
                mx = jnp.max(gate, axis=-1, keepdims=True)
                first = jnp.min(jnp.where(gate == mx, blk_id, float(LANES)), axis=-1, keepdims=True)
                pick = blk_id == first
                sel = jnp.where(pick & (first < own), 1.0, sel)
                gate = jnp.where(pick, -jnp.inf, gate)
            heads.append((hs, q, sel, None))

    def step(j, carry, diag):
        off = pl.multiple_of(j * tq, tq)
        out = []
        for (hs, q, aux, head), (m, l, acc) in zip(heads, carry):
            k = k_ref[0, pl.ds(off, tq), hs]
            v = v_ref[0, pl.ds(off, tq), hs]
            if kind == "fox":
                ck = ck_ref[0, pl.ds(head, 1), pl.ds(off, tq)]

                def col_fn(c, sc):
                    sc = sc + (aux - ck[:, c * LANES:(c + 1) * LANES])
                    return jnp.where(causal[c], sc, MASK_VALUE) if diag else sc
            else:
                rowsel = [jnp.max(jnp.where(blk_id == (j * bpt + n).astype(F32), aux, 0.0),
                                  axis=-1, keepdims=True) > 0.0 for n in range(bpt)]

                def col_fn(c, sc):
                    keep = rowsel[c // spb]
                    if diag:
                        keep = keep | ((row_blk == c // spb) & causal[c])
                    return jnp.where(keep, sc, MASK_VALUE)
            out.append(_attn_tile(q, k, v, m, l, acc, col_fn))
        return tuple(out)

    carry = tuple(_attn_init(tq) for _ in range(hp))
    carry = lax.fori_loop(0, i, functools.partial(step, diag=False), carry)
    carry = step(i, carry, True)
    for (hs, _, _, _), (m, l, acc) in zip(heads, carry):
        o_ref[0, :, hs] = _attn_finish(l, acc).astype(o_ref.dtype)


def _flash(kind, q_arr, q_off, k_arr, k_off, v_arr, v_off, extra, hp, tq):
    bsz, seq, _ = q_arr.shape
    wid = hp * HEAD_DIM
    hgs = N_HEADS // hp
    qo, ko, vo = q_off // hp, k_off // hp, v_off // hp
    in_specs = [pl.BlockSpec((1, tq, wid), lambda b, h, i: (b, i, qo + h)),
                pl.BlockSpec((1, seq, wid), lambda b, h, i: (b, 0, ko + h)),
                pl.BlockSpec((1, seq, wid), lambda b, h, i: (b, 0, vo + h))]
    args = [q_arr, k_arr, v_arr]
    scratch = []
    if kind == "fox":
        cum_col, cum_row = extra
        in_specs += [pl.BlockSpec((1, 1, tq, hp), lambda b, h, i: (b, h, i, 0)),
                     pl.BlockSpec((1, N_HEADS, seq), lambda b, h, i: (b, 0, 0))]
        args += [cum_col, cum_row]
    else:
        assert seq // MOBA_BLOCK <= LANES and tq % MOBA_BLOCK == 0
        scratch = [pltpu.VMEM((hp, LANES, HEAD_DIM), F32)]
    return pl.pallas_call(
        functools.partial(_flash_body, kind=kind, hp=hp, tq=tq),
        grid=(bsz, hgs, seq // tq),
        in_specs=in_specs,
        out_specs=pl.BlockSpec((1, tq, wid), lambda b, h, i: (b, i, h)),
        out_shape=jax.ShapeDtypeStruct((bsz, seq, D_ATTN), BF16),
        scratch_shapes=scratch,
        compiler_params=_cparams(("parallel", "parallel", "arbitrary")),
        name="flash_" + kind,
    )(*args)


def _cumsum_body(f_ref, b_ref, o_ref, carry_ref):
    j = pl.program_id(1)

    @pl.when(j == 0)
    def _():
        carry_ref[...] = jnp.zeros_like(carry_ref)

    ts = f_ref.shape[1]
    logf = jax.nn.log_sigmoid(f_ref[0] + b_ref[...])
    tri = (lax.broadcasted_iota(jnp.int32, (ts, ts), 0) >= lax.broadcasted_iota(jnp.int32, (ts, ts), 1)).astype(F32)
    cum = jnp.dot(tri, logf, precision=lax.Precision.HIGHEST, preferred_element_type=F32) + carry_ref[...]
    o_ref[0] = cum * LOG2E
    carry_ref[...] = cum[ts - 1:ts, :]


def _forget_cumsum(f_raw, b_pad, ts=256):
    bsz, seq, _ = f_raw.shape
    return pl.pallas_call(
        _cumsum_body,
        grid=(bsz, seq // ts),
        in_specs=[pl.BlockSpec((1, ts, LANES), lambda b, j: (b, j, 0)),
                  pl.BlockSpec((1, LANES), lambda b, j: (0, 0))],
        out_specs=pl.BlockSpec((1, ts, LANES), lambda b, j: (b, j, 0)),
        out_shape=jax.ShapeDtypeStruct((bsz, seq, LANES), F32),
        scratch_shapes=[pltpu.VMEM((1, LANES), F32)],
        compiler_params=_cparams(("parallel", "arbitrary")),
        name="forget_cumsum",
    )(f_raw, b_pad)


def _pad_cols(w, n):
    return jnp.pad(w, ((0, 0), (0, n - w.shape[1])))


def _mixer_fox(xb2, bsz, seq, w_in, b_forget, hp=2, tq=512):
    tq = min(tq, seq)
    pos = jnp.arange(seq)
    qkv = _proj(xb2, w_in[:, :3 * D_ATTN], pos, qcols=D_ATTN).reshape(bsz, seq, 3 * D_ATTN)
    f_raw = _proj(xb2, _pad_cols(w_in[:, 3 * D_ATTN:], LANES), pos, out_dtype=F32).reshape(bsz, seq, LANES)
    cum = _forget_cumsum(f_raw, _pad_cols(b_forget.reshape(1, N_HEADS), LANES))[..., :N_HEADS]
    cum_row = cum.transpose(0, 2, 1)
    cum_col = cum.reshape(bsz, seq, N_HEADS // hp, hp).transpose(0, 2, 1, 3)
    o = _flash("fox", qkv, 0, qkv, N_HEADS, qkv, 2 * N_HEADS, (cum_col, cum_row), hp, tq)
    return o.reshape(bsz * seq, D_ATTN)


def _mixer_moba(xb2, bsz, seq, w_in, hp=2, tq=512):
    tq = min(tq, seq)
    pos = jnp.arange(seq)
    qk = _proj(xb2, w_in[:, :2 * D_ATTN], pos, half=64, qcols=D_ATTN).reshape(bsz, seq, 2 * D_ATTN)
    v = _proj(xb2, w_in[:, 2 * D_ATTN:], pos).reshape(bsz, seq, D_ATTN)
    assert seq % MOBA_BLOCK == 0
    o = _flash("moba", qk, 0, qk, N_HEADS, v, 0, None, hp, tq)
    return o.reshape(bsz * seq, D_ATTN)


def _kiprep_body(r_ref, g_ref, b_ref, cos_ref, sin_ref, a_ref, b2_ref):
    x = r_ref[...]
    lane = lax.broadcasted_iota(jnp.int32, x.shape, 1)
    inside = lane < IDX_DIM
    mu = jnp.sum(jnp.where(inside, x, 0.0), axis=-1, keepdims=True) / IDX_DIM
    xc = jnp.where(inside, x - mu, 0.0)
    var = jnp.sum(xc * xc, axis=-1, keepdims=True) / IDX_DIM
    y = xc * lax.rsqrt(var + LN_EPS) * g_ref[...] + b_ref[...]
    y = y * cos_ref[...] + _rotate_half(y, IDX_DIM // 2) * sin_ref[...]
    y = jnp.where(inside, y, 0.0)
    a_ref[...] = y.astype(BF16)
    b2_ref[...] = pltpu.roll(y, IDX_DIM, 1).astype(BF16)


def _ki_prep(raw2, g, b, seq, tm=512):
    m = raw2.shape[0]
    tm = min(tm, seq)
    cos, sin = _rope_tables(jnp.arange(seq), IDX_DIM // 2)
    nsb = seq // tm
    row = lambda i: (i, 0)
    const = lambda i: (0, 0)
    return pl.pallas_call(
        _kiprep_body,
        grid=(m // tm,),
        in_specs=[pl.BlockSpec((tm, LANES), row), pl.BlockSpec((1, LANES), const), pl.BlockSpec((1, LANES), const),
                  pl.BlockSpec((tm, LANES), lambda i: (i % nsb, 0)), pl.BlockSpec((tm, LANES), lambda i: (i % nsb, 0))],
        out_specs=[pl.BlockSpec((tm, LANES), row)] * 2,
        out_shape=[jax.ShapeDtypeStruct((m, LANES), BF16)] * 2,
        compiler_params=_cparams(("parallel",)),
        name="dsa_ki_prep",
    )(raw2, _pad_cols(g.reshape(1, IDX_DIM), LANES), _pad_cols(b.reshape(1, IDX_DIM), LANES), cos, sin)


def _dsa_body(qi_ref, wr_ref, kia_ref, kib_ref, q_ref, k_ref, v_ref, o_ref, key_ref, bias_ref,
              *, tq, ch, topk, wscale):
    i = pl.program_id(1)
    seq = k_ref.shape[1]
    nch = (i * tq + tq + ch - 1) // ch
    nsl = ch // LANES
    wi = wr_ref[0][:, IDX_DIM:IDX_DIM + IDX_HEADS] * wscale
    rowpos = i * tq + lax.broadcasted_iota(jnp.int32, (tq, 1), 0)
    lane_ch = lax.broadcasted_iota(jnp.int32, (tq, ch), 1)
    lane_1 = lax.broadcasted_iota(jnp.int32, (tq, LANES), 1)

    def score_chunk(c, _):
        off = pl.multiple_of(c * ch, ch)
        ka = kia_ref[0, pl.ds(off, ch), :]
        kb = kib_ref[0, pl.ds(off, ch), :]
        sc = jnp.zeros((tq, ch), F32)
        for hpair in range(IDX_HEADS // 2):
            qp = qi_ref[0, :, hpair * LANES:(hpair + 1) * LANES]
            for t, kk in enumerate((ka, kb)):
                h = 2 * hpair + t
                lg = lax.dot_general(qp, kk, _DN_T, preferred_element_type=F32)
                sc = sc + wi[:, h:h + 1] * jnp.maximum(lg, 0.0)
        bits = pltpu.bitcast(sc, jnp.int32)
        bits = jnp.where(bits == INT_MIN, 0, bits)
        key = bits ^ ((bits >> 31) & 0x7FFFFFFF)
        key = jnp.where(off + lane_ch <= rowpos, key, NEG_INF_KEY)
        key_ref[:, pl.ds(off, ch)] = key
        return 0

    lax.fori_loop(0, nch, score_chunk, 0)

    def count(pred):
        def cb(c, acc):
            off = pl.multiple_of(c * ch, ch)
            blk = key_ref[:, pl.ds(off, ch)]
            for s_ in range(nsl):
                acc = acc + jnp.where(pred(blk[:, s_ * LANES:(s_ + 1) * LANES], off + s_ * LANES + lane_1), 1, 0)
            return acc
        acc = lax.fori_loop(0, nch, cb, jnp.zeros((tq, LANES), jnp.int32))
        return jnp.sum(acc.astype(F32), axis=-1, keepdims=True).astype(jnp.int32)

    def bisect_val(t, lo):
        cand = lo + (jnp.int32(1) << (31 - t))
        return jnp.where(count(lambda kv, idx: kv >= cand) >= topk, cand, lo)

    thr = lax.fori_loop(0, 32, bisect_val, jnp.full((tq, LANES), INT_MIN, jnp.int32))
    need = topk - count(lambda kv, idx: kv > thr)
    n_eq = count(lambda kv, idx: kv == thr)
    excess = (n_eq > need) & (thr[:, 0:1] > NEG_INF_KEY)
    any_excess = jnp.max(jnp.where(excess, 1.0, 0.0)) > 0.0

    def tie_path():
        def bisect_idx(t, lo):
            cand = lo + (jnp.int32(1) << (12 - t))
            return jnp.where(count(lambda kv, idx: (kv == thr) & (idx < cand)) < need, cand, lo)
        return lax.fori_loop(0, 13, bisect_idx, jnp.zeros((tq, LANES), jnp.int32))

    jmax = lax.cond(any_excess, tie_path, lambda: jnp.full((tq, LANES), seq, jnp.int32))

    def bias_chunk(c, _):
        off = pl.multiple_of(c * ch, ch)
        blk = key_ref[:, pl.ds(off, ch)]
        for s_ in range(nsl):
            kv = blk[:, s_ * LANES:(s_ + 1) * LANES]
            idx = off + s_ * LANES + lane_1
            keep = ((kv > thr) | ((kv == thr) & (idx <= jmax))) & (idx <= rowpos)
            bias_ref[:, pl.ds(pl.multiple_of(off + s_ * LANES, LANES), LANES)] = jnp.where(keep, 0.0, MASK_VALUE)
        return 0

    lax.fori_loop(0, nch, bias_chunk, 0)

    rows = GQA_GROUP * tq
    for g in range(B_KV_HEADS):
        gs = slice(g * HEAD_DIM, (g + 1) * HEAD_DIM)
        qg = jnp.concatenate(
            [q_ref[0, :, (g * GQA_GROUP + r) * HEAD_DIM:(g * GQA_GROUP + r + 1) * HEAD_DIM]
             for r in range(GQA_GROUP)], axis=0)

        def att(c, carry):
            off = pl.multiple_of(c * ch, ch)
            bias = bias_ref[:, pl.ds(off, ch)]

            def col_fn(s_, sc):
                return sc + jnp.concatenate([bias[:, s_ * LANES:(s_ + 1) * LANES]] * GQA_GROUP, axis=0)

            return _attn_tile(qg, k_ref[0, pl.ds(off, ch), gs], v_ref[0, pl.ds(off, ch), gs], *carry, col_fn)

        m, l, acc = lax.fori_loop(0, nch, att, _attn_init(rows))
        o = _attn_finish(l, acc)
        for r in range(GQA_GROUP):
            h = g * GQA_GROUP + r
            o_ref[0, :, h * HEAD_DIM:(h + 1) * HEAD_DIM] = o[r * tq:(r + 1) * tq].astype(o_ref.dtype)


def _mixer_dsa(xb2, bsz, seq, w_in, idx_g, idx_b, tq=256, ch=512):
    ch = min(ch, seq)
    pos = jnp.arange(seq)
    nqk = D_ATTN + KV_WIDTH
    qk = _proj(xb2, w_in[:, :nqk], pos, half=64, qcols=D_ATTN).reshape(bsz, seq, nqk)
    v = _proj(xb2, w_in[:, nqk:nqk + KV_WIDTH], pos).reshape(bsz, seq, KV_WIDTH)
    o_qi = nqk + KV_WIDTH
    n_qi = IDX_HEADS * IDX_DIM
    qi = _proj(xb2, w_in[:, o_qi:o_qi + n_qi], pos, half=IDX_DIM // 2).reshape(bsz, seq, n_qi)
    raw2 = _proj(xb2, _pad_cols(w_in[:, o_qi + n_qi:], LANES), pos, out_dtype=F32)
    kia, kib = _ki_prep(raw2, idx_g, idx_b, seq)
    topk = min(IDX_TOPK_MAX, seq // 4)
    body = functools.partial(_dsa_body, tq=tq, ch=ch, topk=topk, wscale=IDX_HEADS ** -0.5 * IDX_DIM ** -0.5)
    res = lambda b, i: (b, 0, 0)
    o = pl.pallas_call(
        body,
        grid=(bsz, seq // tq),
        in_specs=[pl.BlockSpec((1, tq, n_qi), lambda b, i: (b, i, 0)),
                  pl.BlockSpec((1, tq, LANES), lambda b, i: (b, i, 0)),
                  pl.BlockSpec((1, seq, LANES), res), pl.BlockSpec((1, seq, LANES), res),
                  pl.BlockSpec((1, tq, D_ATTN), lambda b, i: (b, i, 0)),
                  pl.BlockSpec((1, seq, KV_WIDTH), lambda b, i: (b, 0, D_ATTN // KV_WIDTH)),
                  pl.BlockSpec((1, seq, KV_WIDTH), res)],
        out_specs=pl.BlockSpec((1, tq, D_ATTN), lambda b, i: (b, i, 0)),
        out_shape=jax.ShapeDtypeStruct((bsz, seq, D_ATTN), BF16),
        scratch_shapes=[pltpu.VMEM((tq, seq), jnp.int32), pltpu.VMEM((tq, seq), F32)],
        compiler_params=_cparams(("parallel", "arbitrary")),
        name="dsa_select_attend",
    )(qi, raw2.reshape(bsz, seq, LANES), kia.reshape(bsz, seq, LANES), kib.reshape(bsz, seq, LANES), qk, qk, v)
    return o.reshape(bsz * seq, D_ATTN)


def _trunk(x, layers):
    bsz, seq, d = x.shape
    x2 = x.reshape(bsz * seq, d)
    xb2 = x2
    for kind, p in enumerate(layers):
        w_in = p["w_in"].astype(BF16)
        if kind == 0:
            a = _mixer_dilated(xb2, bsz, seq, w_in)
        elif kind == 1:
            a = _mixer_dsa(xb2, bsz, seq, w_in, p["idx_norm_g"], p["idx_norm_b"])
        elif kind == 2:
            a = _mixer_fox(xb2, bsz, seq, w_in, p["b_forget"])
        else:
            a = _mixer_moba(xb2, bsz, seq, w_in)
        x1, x1b, logits = _outln(a, p["w_out"].astype(BF16), x2, p["ln1_g"], p["ln1_b"],
                                 p["router_w"].astype(BF16), p["router_b"])
        x2, xb2 = _moe_layer(x1, x1b, logits, p["w_gu"], p["b_gu"], p["w_dn"], p["b_dn"], p["ln2_g"], p["ln2_b"])
    return x2.reshape(bsz, seq, d)


def kernel(x, l0_w_in, l0_w_out, l0_ln1_g, l0_ln1_b, l0_router_w, l0_router_b, l0_w_gu, l0_b_gu, l0_w_dn, l0_b_dn, l0_ln2_g, l0_ln2_b, l1_w_in, l1_idx_norm_g, l1_idx_norm_b, l1_w_out, l1_ln1_g, l1_ln1_b, l1_router_w, l1_router_b, l1_w_gu, l1_b_gu, l1_w_dn, l1_b_dn, l1_ln2_g, l1_ln2_b, l2_w_in, l2_b_forget, l2_w_out, l2_ln1_g, l2_ln1_b, l2_router_w, l2_router_b, l2_w_gu, l2_b_gu, l2_w_dn, l2_b_dn, l2_ln2_g, l2_ln2_b, l3_w_in, l3_w_out, l3_ln1_g, l3_ln1_b, l3_router_w, l3_router_b, l3_w_gu, l3_b_gu, l3_w_dn, l3_b_dn, l3_ln2_g, l3_ln2_b):
    names = ("w_out", "ln1_g", "ln1_b", "router_w", "router_b", "w_gu", "b_gu", "w_dn", "b_dn", "ln2_g", "ln2_b")
    l0 = dict(zip(("w_in",) + names, (l0_w_in, l0_w_out, l0_ln1_g, l0_ln1_b, l0_router_w, l0_router_b,
                                      l0_w_gu, l0_b_gu, l0_w_dn, l0_b_dn, l0_ln2_g, l0_ln2_b)))
    l1 = dict(zip(("w_in", "idx_norm_g", "idx_norm_b") + names,
                  (l1_w_in, l1_idx_norm_g, l1_idx_norm_b, l1_w_out, l1_ln1_g, l1_ln1_b, l1_router_w, l1_router_b,
                   l1_w_gu, l1_b_gu, l1_w_dn, l1_b_dn, l1_ln2_g, l1_ln2_b)))
    l2 = dict(zip(("w_in", "b_forget") + names,
                  (l2_w_in, l2_b_forget, l2_w_out, l2_ln1_g, l2_ln1_b, l2_router_w, l2_router_b,
                   l2_w_gu, l2_b_gu, l2_w_dn, l2_b_dn, l2_ln2_g, l2_ln2_b)))
    l3 = dict(zip(("w_in",) + names, (l3_w_in, l3_w_out, l3_ln1_g, l3_ln1_b, l3_router_w, l3_router_b,
                                      l3_w_gu, l3_b_gu, l3_w_dn, l3_b_dn, l3_ln2_g, l3_ln2_b)))
    return _trunk(x, (l0, l1, l2, l3))
```

```python
import functools

import jax
import jax.numpy as jnp
import numpy as np
from jax import lax
from jax.experimental import pallas as pl
from jax.experimental.pallas import tpu as pltpu

N_HEADS = 16
HEAD_DIM = 128
D_MODEL = 2048
D_ATTN = N_HEADS * HEAD_DIM
ROPE_THETA = 10000.0
MASK_VALUE = -1e30
LN_EPS = 1e-5
DILATED_GROUPS = ((128, 1), (512, 4), (2048, 16))
DIL_MAX = 16
B_KV_HEADS = 4
GQA_GROUP = N_HEADS // B_KV_HEADS
KV_WIDTH = B_KV_HEADS * HEAD_DIM
IDX_HEADS = 16
IDX_DIM = 64
IDX_TOPK_MAX = 256
MOBA_BLOCK = 256
MOBA_TOPK = 3
N_EXPERTS = 32
TOP_K = 4
D_EXPERT = 1024
SWIGLU_ALPHA = 1.702
SWIGLU_LIMIT = 7.0
DEPTH = 4
DEEPNORM_ALPHA = (2 * DEPTH) ** 0.25

LANES = 128
VMEM_LIMIT = 56 * 1024 * 1024
MOE_TM = 256
INT_MIN = -(2 ** 31)
NEG_INF_KEY = -2139095041
LOG2E = 1.4426950408889634
QSCALE = HEAD_DIM ** -0.5 * LOG2E

BF16 = jnp.bfloat16
F32 = jnp.float32
_DN_T = (((1,), (1,)), ((), ()))


def _cparams(sem):
    return pltpu.CompilerParams(dimension_semantics=sem, vmem_limit_bytes=VMEM_LIMIT)


def _rope_tables(pos, half):
    inv_freq = ROPE_THETA ** (-jnp.arange(half, dtype=F32) / half)
    ang = pos.astype(F32)[:, None] * inv_freq[None, :]
    cos = jnp.tile(jnp.cos(ang), (1, LANES // half))
    sin = jnp.tile(jnp.concatenate([-jnp.sin(ang), jnp.sin(ang)], -1), (1, LANES // (2 * half)))
    return cos, sin


def _rotate_half(t, half):
    if 2 * half == LANES:
        return pltpu.roll(t, half, 1)
    lane = lax.broadcasted_iota(jnp.int32, t.shape, 1)
    first = (lane % (2 * half)) < half
    return jnp.where(first, pltpu.roll(t, LANES - half, 1), pltpu.roll(t, half, 1))


def _proj_body(*refs, half, q_tiles):
    if half:
        x_ref, w_ref, cos_ref, sin_ref, o_ref = refs
    else:
        x_ref, w_ref, o_ref = refs
    acc = jnp.dot(x_ref[...].astype(BF16), w_ref[...], preferred_element_type=F32)
    if q_tiles:
        acc = acc * jnp.where(pl.program_id(1) < q_tiles, QSCALE, 1.0)
    if not half:
        o_ref[...] = acc.astype(o_ref.dtype)
        return
    cos = cos_ref[...]
    sin = sin_ref[...]
    for c in range(acc.shape[1] // LANES):
        t = acc[:, c * LANES:(c + 1) * LANES]
        o_ref[:, c * LANES:(c + 1) * LANES] = (t * cos + _rotate_half(t, half) * sin).astype(o_ref.dtype)


def _proj(x2, w, pos, half=0, out_dtype=BF16, tm=512, qcols=0):
    m, k = x2.shape
    n = w.shape[1]
    seq = pos.shape[0]
    tm = min(tm, seq)
    tn = 512 if n % 512 == 0 else (256 if n % 256 == 0 else LANES)
    assert qcols % tn == 0 and seq % tm == 0
    in_specs = [pl.BlockSpec((tm, k), lambda i, j: (i, 0)),
                pl.BlockSpec((k, tn), lambda i, j: (0, j))]
    args = [x2, w]
    if half:
        cos, sin = _rope_tables(pos, half)
        nsb = seq // tm
        in_specs += [pl.BlockSpec((tm, LANES), lambda i, j: (i % nsb, 0))] * 2
        args += [cos, sin]
    return pl.pallas_call(
        functools.partial(_proj_body, half=half, q_tiles=qcols // tn),
        grid=(m // tm, n // tn),
        in_specs=in_specs,
        out_specs=pl.BlockSpec((tm, tn), lambda i, j: (i, j)),
        out_shape=jax.ShapeDtypeStruct((m, n), out_dtype),
        compiler_params=_cparams(("parallel", "arbitrary")),
        name="proj_rope%d" % half,
    )(*args)


def _layer_norm_rows(z, g, b):
    mu = jnp.mean(z, axis=-1, keepdims=True)
    zc = z - mu
    var = jnp.mean(zc * zc, axis=-1, keepdims=True)
    return zc * lax.rsqrt(var + LN_EPS) * g + b


def _outln_body(a_ref, w_ref, x_ref, g_ref, b_ref, rw_ref, rb_ref, xo_ref, xb_ref, lg_ref):
    h = jnp.dot(a_ref[...], w_ref[...], preferred_element_type=F32)
    y = _layer_norm_rows(DEEPNORM_ALPHA * x_ref[...] + h, g_ref[...], b_ref[...])
    xo_ref[...] = y
    yb = y.astype(BF16)
    xb_ref[...] = yb
    lg_ref[...] = jnp.dot(yb, rw_ref[...], preferred_element_type=F32) + rb_ref[...]


def _outln(a, w_out, x2, g, b, rw, rb, tm=256):
    m, d = x2.shape
    row = lambda i: (i, 0)
    const = lambda i: (0, 0)
    return pl.pallas_call(
        _outln_body,
        grid=(m // tm,),
        in_specs=[pl.BlockSpec((tm, d), row), pl.BlockSpec((d, d), const), pl.BlockSpec((tm, d), row),
                  pl.BlockSpec((1, d), const), pl.BlockSpec((1, d), const),
                  pl.BlockSpec((d, N_EXPERTS), const), pl.BlockSpec((1, N_EXPERTS), const)],
        out_specs=[pl.BlockSpec((tm, d), row), pl.BlockSpec((tm, d), row),
                   pl.BlockSpec((tm, N_EXPERTS), row)],
        out_shape=[jax.ShapeDtypeStruct((m, d), F32), jax.ShapeDtypeStruct((m, d), BF16),
                   jax.ShapeDtypeStruct((m, N_EXPERTS), F32)],
        compiler_params=_cparams(("parallel",)),
        name="outproj_ln_router",
    )(a, w_out, x2, g.reshape(1, d), b.reshape(1, d), rw, rb.reshape(1, N_EXPERTS))


def _moe_body(be_ref, nb_ref, nx_ref, x_ref, wgu_hbm, bgu_ref, wdn_hbm, bdn_ref, o_ref,
              wgu_f, wdn_f, wgu_b, wdn_b, sem):
    i = pl.program_id(0)
    used = i < nb_ref[0]
    expert = be_ref[i]
    new_expert = (i == 0) | (expert != be_ref[jnp.maximum(i - 1, 0)])

    def weight_copies(e):
        return (pltpu.make_async_copy(wgu_hbm.at[e], wgu_f, sem.at[0]),
                pltpu.make_async_copy(wdn_hbm.at[e], wdn_f, sem.at[1]))

    @pl.when(used & (i == 0))
    def _():
        for c in weight_copies(expert):
            c.start()

    @pl.when(used & new_expert)
    def _():
        for c in weight_copies(expert):
            c.wait()
        rows = 64

        def cast(src, dst):
            def body(r, _):
                sl = pl.ds(pl.multiple_of(r * rows, rows), rows)
                dst[sl, :] = src[sl, :].astype(BF16)
                return 0
            lax.fori_loop(0, src.shape[0] // rows, body, 0)

        cast(wgu_f, wgu_b)
        cast(wdn_f, wdn_b)

        @pl.when(nx_ref[i] >= 0)
        def _():
            for c in weight_copies(nx_ref[i]):
                c.start()

    @pl.when(used)
    def _():
        h = jnp.dot(x_ref[...], wgu_b[...], preferred_element_type=F32) + bgu_ref[0]
        glu = jnp.minimum(h[:, :D_EXPERT], SWIGLU_LIMIT)
        lin = jnp.clip(h[:, D_EXPERT:], -SWIGLU_LIMIT, SWIGLU_LIMIT)
        act = glu * jax.nn.sigmoid(SWIGLU_ALPHA * glu) * (lin + 1.0)
        y = jnp.dot(act.astype(BF16), wdn_b[...], preferred_element_type=F32) + bdn_ref[0]
        o_ref[...] = y.astype(o_ref.dtype)

    @pl.when(jnp.logical_not(used))
    def _():
        o_ref[...] = jnp.zeros_like(o_ref)


def _moe_experts(x_rows, block_exp, n_used, next_exp, w_gu, b_gu, w_dn, b_dn):
    n_rows, d = x_rows.shape
    n_blocks = n_rows // MOE_TM
    grid_spec = pltpu.PrefetchScalarGridSpec(
        num_scalar_prefetch=3,
        grid=(n_blocks,),
        in_specs=[pl.BlockSpec((MOE_TM, d), lambda i, be, nb, nx: (i, 0)),
                  pl.BlockSpec(memory_space=pl.ANY),
                  pl.BlockSpec((1, 1, 2 * D_EXPERT), lambda i, be, nb, nx: (be[i], 0, 0)),
                  pl.BlockSpec(memory_space=pl.ANY),
                  pl.BlockSpec((1, 1, d), lambda i, be, nb, nx: (be[i], 0, 0))],
        out_specs=pl.BlockSpec((MOE_TM, d), lambda i, be, nb, nx: (i, 0)),
        scratch_shapes=[pltpu.VMEM((d, 2 * D_EXPERT), F32), pltpu.VMEM((D_EXPERT, d), F32),
                        pltpu.VMEM((d, 2 * D_EXPERT), BF16), pltpu.VMEM((D_EXPERT, d), BF16),
                        pltpu.SemaphoreType.DMA((2,))],
    )
    return pl.pallas_call(
        _moe_body,
        grid_spec=grid_spec,
        out_shape=jax.ShapeDtypeStruct((n_rows, d), F32),
        compiler_params=_cparams(("arbitrary",)),
        name="moe_experts",
    )(block_exp, n_used, next_exp, x_rows, w_gu, b_gu.reshape(N_EXPERTS, 1, -1), w_dn,
      b_dn.reshape(N_EXPERTS, 1, -1))


def _combine_body(*refs):
    y_refs = refs[:TOP_K]
    gt_ref, x_ref, g_ref, b_ref, xo_ref, xb_ref = refs[TOP_K:]
    gt = gt_ref[...]
    y = gt[:, 0:1] * y_refs[0][...].astype(F32)
    for k in range(1, TOP_K):
        y = y + gt[:, k:k + 1] * y_refs[k][...].astype(F32)
    out = _layer_norm_rows(DEEPNORM_ALPHA * x_ref[...] + y, g_ref[...], b_ref[...])
    xo_ref[...] = out
    xb_ref[...] = out.astype(BF16)


def _combine_ln(y4, gates, x2, g, b, tm=256):
    m, d = x2.shape
    nb = m // tm
    row = lambda i: (i, 0)
    const = lambda i: (0, 0)
    y_specs = [pl.BlockSpec((tm, d), functools.partial(lambda i, k: (k * nb + i, 0), k=k)) for k in range(TOP_K)]
    return pl.pallas_call(
        _combine_body,
        grid=(nb,),
        in_specs=y_specs + [pl.BlockSpec((tm, TOP_K), row),
                            pl.BlockSpec((tm, d), row), pl.BlockSpec((1, d), const), pl.BlockSpec((1, d), const)],
        out_specs=[pl.BlockSpec((tm, d), row), pl.BlockSpec((tm, d), row)],
        out_shape=[jax.ShapeDtypeStruct((m, d), F32), jax.ShapeDtypeStruct((m, d), BF16)],
        compiler_params=_cparams(("parallel",)),
        name="moe_combine_ln",
    )(*([y4] * TOP_K), gates, x2, g.reshape(1, d), b.reshape(1, d))


def _moe_layer(x1, x1b, logits, w_gu, b_gu, w_dn, b_dn, g, b):
    n_tok, d = x1.shape
    n_assign = n_tok * TOP_K
    top_logit, top_exp = lax.top_k(logits, TOP_K)
    gates = jax.nn.softmax(top_logit, axis=-1)
    flat_exp = top_exp.reshape(-1).astype(jnp.int32)
    experts = jnp.arange(N_EXPERTS, dtype=jnp.int32)
    counts = jnp.sum((flat_exp[:, None] == experts[None, :]).astype(jnp.int32), axis=0)
    padded = (counts + MOE_TM - 1) // MOE_TM * MOE_TM
    pad_end = jnp.cumsum(padded)
    n_blocks = n_assign // MOE_TM + N_EXPERTS
    n_rows = n_blocks * MOE_TM
    block_start = jnp.arange(n_blocks, dtype=jnp.int32) * MOE_TM
    block_exp = jnp.minimum(jnp.sum((pad_end[None, :] <= block_start[:, None]).astype(jnp.int32), axis=1),
                            N_EXPERTS - 1).astype(jnp.int32)
    n_used = (pad_end[-1] // MOE_TM).astype(jnp.int32).reshape(1)
    later = (experts[None, :] > experts[:, None]) & (counts[None, :] > 0)
    nxt = jnp.min(jnp.where(later, experts[None, :], N_EXPERTS), axis=1)
    nxt = jnp.where(nxt == N_EXPERTS, -1, nxt)
    next_exp = jnp.sum(jnp.where(block_exp[:, None] == experts[None, :], nxt[None, :], 0), axis=1).astype(jnp.int32)
    idx_bits = (n_assign + N_EXPERTS * MOE_TM - 1).bit_length()
    a_idx = jnp.arange(n_assign, dtype=jnp.int32)
    f_idx = n_assign + jnp.arange(N_EXPERTS * MOE_TM, dtype=jnp.int32)
    f_num = jnp.arange(MOE_TM, dtype=jnp.int32)[None, :]
    f_major = jnp.where(f_num < (padded - counts)[:, None], 2 * experts[:, None] + 1, 2 * N_EXPERTS).reshape(-1)
    keys = jnp.concatenate([(2 * flat_exp << idx_bits) | a_idx, (f_major << idx_bits) | f_idx])
    src = jnp.sort(keys) & ((1 << idx_bits) - 1)
    row_tok = jnp.where(src < n_assign, src // TOP_K, 0)[:n_rows]
    _, inv = lax.sort_key_val(src, jnp.arange(src.shape[0], dtype=jnp.int32))
    pos = inv[:n_assign]
    x_rows = x1b[row_tok]
    y_rows = _moe_experts(x_rows, block_exp, n_used, next_exp, w_gu, b_gu, w_dn, b_dn)
    y4 = y_rows[pos.reshape(n_tok, TOP_K).T.reshape(-1)]
    return _combine_ln(y4, gates, x1, g, b)


def _attn_tile(q, k, v, m, l, acc, col_fn):
    s = lax.dot_general(q, k, _DN_T, preferred_element_type=F32)
    cols = [col_fn(c, s[:, c * LANES:(c + 1) * LANES]) for c in range(s.shape[1] // LANES)]
    m_new = jnp.maximum(m, jnp.max(functools.reduce(jnp.maximum, cols), axis=-1, keepdims=True))
    alpha = jnp.exp2(m - m_new)
    ps = [jnp.exp2(c - m_new) for c in cols]
    l = alpha * l + functools.reduce(jnp.add, ps)
    p = jnp.concatenate([x.astype(BF16) for x in ps], axis=1)
    acc = alpha * acc + jnp.dot(p, v, preferred_element_type=F32)
    return m_new, l, acc


def _attn_init(rows):
    return (jnp.full((rows, LANES), -jnp.inf, F32), jnp.zeros((rows, LANES), F32),
            jnp.zeros((rows, HEAD_DIM), F32))


def _attn_finish(l, acc):
    return acc / jnp.sum(l, axis=-1, keepdims=True)


def _band_body(*refs, has_prev, is_last, planes, rows):
    if has_prev:
        q_ref, kc_ref, kp_ref, vc_ref, vp_ref, oi_ref, li_ref = refs[:7]
        outs = refs[7:]
    else:
        q_ref, kc_ref, kp_ref, vc_ref, vp_ref = refs[:5]
        outs = refs[5:]
    o_ref = outs[0]
    j = pl.program_id(2)
    w = planes * rows

    def local(t):
        return (t % rows) * planes + t // rows

    tq = lax.broadcasted_iota(jnp.int32, (w, 2 * w), 0)
    tk = lax.broadcasted_iota(jnp.int32, (w, 2 * w), 1)
    cur = tk >= w
    dist = (w + local(tq)) - (local(tk % w) + jnp.where(cur, w, 0))
    mask = (dist >= 0) & (dist <= LANES) & (cur | (j > 0))
    lane = lax.broadcasted_iota(jnp.int32, (w, LANES), 1)
    lse_all = jnp.zeros((w, LANES), F32)

    def blk(ref, hs):
        return ref[0, :, 0, :, hs].reshape(w, HEAD_DIM)

    if has_prev:
        lse_in = li_ref[0, :, 0].reshape(w, LANES)

    for h in range(N_HEADS):
        hs = slice(h * HEAD_DIM, (h + 1) * HEAD_DIM)
        kcat = jnp.concatenate([blk(kp_ref, hs), blk(kc_ref, hs)], axis=0)
        vcat = jnp.concatenate([blk(vp_ref, hs), blk(vc_ref, hs)], axis=0)
        s = lax.dot_general(blk(q_ref, hs), kcat, _DN_T, preferred_element_type=F32)
        s = jnp.where(mask, s, MASK_VALUE)
        m = jnp.max(s, axis=-1, keepdims=True)
        p = jnp.exp2(s - m)
        l = jnp.sum(p, axis=-1, keepdims=True)
        o = jnp.dot((p / l).astype(BF16), vcat, preferred_element_type=F32)
        lse = m + jnp.log2(l)
        if has_prev:
            lse_prev = lse_in[:, h:h + 1]
            mx = jnp.maximum(lse, lse_prev)
            e_new = jnp.exp2(lse - mx)
            e_old = jnp.exp2(lse_prev - mx)
            tot = e_new + e_old
            o = (e_new * o + e_old * blk(oi_ref, hs)) / tot
            lse = mx + jnp.log2(tot)
        o_ref[0, :, 0, :, hs] = o.astype(o_ref.dtype).reshape(planes, rows, HEAD_DIM)
        lse_all = jnp.where(lane == h, lse, lse_all)
    if not is_last:
        outs[1][0, :, 0] = lse_all.reshape(planes, rows, LANES)


def _band_stage(qk, v, prev, dil, is_last):
    bsz, _, sub, _ = v.shape
    planes = DIL_MAX // dil
    groups = DIL_MAX // planes
    rows = (LANES if planes < DIL_MAX else 2 * LANES) // planes
    view = lambda a: a.reshape(bsz, planes, groups, sub, a.shape[-1])
    nb = sub // rows
    blk = lambda col, prevblk: pl.BlockSpec(
        (1, planes, 1, rows, D_ATTN),
        (lambda b, g, j: (b, 0, g, jnp.maximum(j - 1, 0), col)) if prevblk else (lambda b, g, j: (b, 0, g, j, col)))
    in_specs = [blk(0, False), blk(1, False), blk(1, True), blk(0, False), blk(0, True)]
    args = [view(qk)] * 3 + [view(v)] * 2
    o_spec = blk(0, False)
    l_spec = pl.BlockSpec((1, planes, 1, rows, LANES), lambda b, g, j: (b, 0, g, j, 0))
    if prev is not None:
        in_specs += [o_spec, l_spec]
        args += [view(prev[0]), view(prev[1])]
    out_specs = [o_spec] if is_last else [o_spec, l_spec]
    out_shape = [jax.ShapeDtypeStruct((bsz, planes, groups, sub, D_ATTN), BF16 if is_last else F32)]
    if not is_last:
        out_shape.append(jax.ShapeDtypeStruct((bsz, planes, groups, sub, LANES), F32))
    outs = pl.pallas_call(
        functools.partial(_band_body, has_prev=prev is not None, is_last=is_last, planes=planes, rows=rows),
        grid=(bsz, groups, nb),
        in_specs=in_specs, out_specs=out_specs, out_shape=out_shape,
        compiler_params=_cparams(("parallel", "parallel", "arbitrary")),
        name="dilated_band_d%d" % dil,
    )(*args)
    unview = lambda a: a.reshape(bsz, DIL_MAX, sub, a.shape[-1])
    if is_last:
        return unview(outs[0])
    return unview(outs[0]), unview(outs[1])


def _mixer_dilated(xb2, bsz, seq, w_in):
    d = xb2.shape[1]
    sub = seq // DIL_MAX
    assert seq % (DIL_MAX * 2 * LANES) == 0
    xp = xb2.reshape(bsz, sub, DIL_MAX, d).transpose(0, 2, 1, 3).reshape(bsz * seq, d)
    pos = (jnp.arange(sub)[None, :] * DIL_MAX + jnp.arange(DIL_MAX)[:, None]).reshape(-1)
    qk = _proj(xp, w_in[:, :2 * D_ATTN], pos, half=64, qcols=D_ATTN).reshape(bsz, DIL_MAX, sub, 2 * D_ATTN)
    v = _proj(xp, w_in[:, 2 * D_ATTN:], pos).reshape(bsz, DIL_MAX, sub, D_ATTN)
    prev = None
    for g, (window, dil) in enumerate(DILATED_GROUPS):
        assert window // dil == LANES and DIL_MAX % dil == 0
        prev = _band_stage(qk, v, prev, dil, g == len(DILATED_GROUPS) - 1)
    return prev.transpose(0, 2, 1, 3).reshape(bsz * seq, D_ATTN)


def _flash_body(*refs, kind, hp, tq):
    if kind == "fox":
        q_ref, k_ref, v_ref, cq_ref, ck_ref, o_ref = refs
    else:
        q_ref, k_ref, v_ref, o_ref, kmean_ref = refs
    hg = pl.program_id(1)
    i = pl.program_id(2)
    seq = k_ref.shape[1]
    nsl = tq // LANES
    rel = lax.broadcasted_iota(jnp.int32, (tq, LANES), 0) - lax.broadcasted_iota(jnp.int32, (tq, LANES), 1)
    causal = [rel >= c * LANES for c in range(nsl)]
    heads = []

    if kind == "moba":
        nblk = seq // MOBA_BLOCK
        spb = MOBA_BLOCK // LANES
        bpt = tq // MOBA_BLOCK

        @pl.when(i == 0)
        def _():
            kmean_ref[...] = jnp.zeros_like(kmean_ref)
            for hh in range(hp):
                hs = slice(hh * HEAD_DIM, (hh + 1) * HEAD_DIM)
                for n in range(nblk):
                    kb = k_ref[0, n * MOBA_BLOCK:(n + 1) * MOBA_BLOCK, hs].astype(F32)
                    kmean_ref[hh, n:n + 1, :] = jnp.sum(kb, axis=0, keepdims=True) / MOBA_BLOCK

        row_blk = lax.broadcasted_iota(jnp.int32, (tq, 1), 0) // MOBA_BLOCK
        own = (i * bpt + row_blk).astype(F32)
        blk_id = lax.broadcasted_iota(jnp.int32, (tq, LANES), 1).astype(F32)

    for hh in range(hp):
        hs = slice(hh * HEAD_DIM, (hh + 1) * HEAD_DIM)
        q = q_ref[0, :, hs]
        if kind == "fox":
            cq = jnp.broadcast_to(cq_ref[0, 0, :, hh:hh + 1], (tq, LANES))
            heads.append((hs, q, cq, hg * hp + hh))
        else:
            gate = lax.dot_general(q, kmean_ref[hh].astype(BF16), _DN_T, preferred_element_type=F32)
            gate = jnp.where(blk_id < own, gate, -jnp.inf)
            sel = jnp.zeros(gate.shape, F32)
            for _ in range(min(MOBA_TOPK, nblk)):
                mx = jnp.max(gate, axis=-1, keepdims=True)
                first = jnp.min(jnp.where(gate == mx, blk_id, float(LANES)), axis=-1, keepdims=True)
                pick = blk_id == first
                sel = jnp.where(pick & (first < own), 1.0, sel)
                gate = jnp.where(pick, -jnp.inf, gate)
            heads.append((hs, q, sel, None))

    def step(j, carry, diag):
        off = pl.multiple_of(j * tq, tq)
        out = []
        for (hs, q, aux, head), (m, l, acc) in zip(heads, carry):
            k = k_ref[0, pl.ds(off, tq), hs]
            v = v_ref[0, pl.ds(off, tq), hs]
            if kind == "fox":
                ck = ck_ref[0, pl.ds(head, 1), pl.ds(off, tq)]

                def col_fn(c, sc):
                    sc = sc + (aux - ck[:, c * LANES:(c + 1) * LANES])
                    return jnp.where(causal[c], sc, MASK_VALUE) if diag else sc
            else:
                rowsel = [jnp.max(jnp.where(blk_id == (j * bpt + n).astype(F32), aux, 0.0),
                                  axis=-1, keepdims=True) > 0.0 for n in range(bpt)]

                def col_fn(c, sc):
                    keep = rowsel[c // spb]
                    if diag:
                        keep = keep | ((row_blk == c // spb) & causal[c])
                    return jnp.where(keep, sc, MASK_VALUE)
            out.append(_attn_tile(q, k, v, m, l, acc, col_fn))
        return tuple(out)

    carry = tuple(_attn_init(tq) for _ in range(hp))
    carry = lax.fori_loop(0, i, functools.partial(step, diag=False), carry)
    carry = step(i, carry, True)
    for (hs, _, _, _), (m, l, acc) in zip(heads, carry):
        o_ref[0, :, hs] = _attn_finish(l, acc).astype(o_ref.dtype)


def _flash(kind, q_arr, q_off, k_arr, k_off, v_arr, v_off, extra, hp, tq):
    bsz, seq, _ = q_arr.shape
    wid = hp * HEAD_DIM
    hgs = N_HEADS // hp
    qo, ko, vo = q_off // hp, k_off // hp, v_off // hp
    in_specs = [pl.BlockSpec((1, tq, wid), lambda b, h, i: (b, i, qo + h)),
                pl.BlockSpec((1, seq, wid), lambda b, h, i: (b, 0, ko + h)),
                pl.BlockSpec((1, seq, wid), lambda b, h, i: (b, 0, vo + h))]
    args = [q_arr, k_arr, v_arr]
    scratch = []
    if kind == "fox":
        cum_col, cum_row = extra
        in_specs += [pl.BlockSpec((1, 1, tq, hp), lambda b, h, i: (b, h, i, 0)),
                     pl.BlockSpec((1, N_HEADS, seq), lambda b, h, i: (b, 0, 0))]
        args += [cum_col, cum_row]
    else:
        assert seq // MOBA_BLOCK <= LANES and tq % MOBA_BLOCK == 0
        scratch = [pltpu.VMEM((hp, LANES, HEAD_DIM), F32)]
    return pl.pallas_call(
        functools.partial(_flash_body, kind=kind, hp=hp, tq=tq),
        grid=(bsz, hgs, seq // tq),
        in_specs=in_specs,
        out_specs=pl.BlockSpec((1, tq, wid), lambda b, h, i: (b, i, h)),
        out_shape=jax.ShapeDtypeStruct((bsz, seq, D_ATTN), BF16),
        scratch_shapes=scratch,
        compiler_params=_cparams(("parallel", "parallel", "arbitrary")),
        name="flash_" + kind,
    )(*args)


def _cumsum_body(f_ref, b_ref, o_ref, carry_ref):
    j = pl.program_id(1)

    @pl.when(j == 0)
    def _():
        carry_ref[...] = jnp.zeros_like(carry_ref)

    ts = f_ref.shape[1]
    logf = jax.nn.log_sigmoid(f_ref[0] + b_ref[...])
    tri = (lax.broadcasted_iota(jnp.int32, (ts, ts), 0) >= lax.broadcasted_iota(jnp.int32, (ts, ts), 1)).astype(F32)
    cum = jnp.dot(tri, logf, precision=lax.Precision.HIGHEST, preferred_element_type=F32) + carry_ref[...]
    o_ref[0] = cum * LOG2E
    carry_ref[...] = cum[ts - 1:ts, :]


def _forget_cumsum(f_raw, b_pad, ts=256):
    bsz, seq, _ = f_raw.shape
    return pl.pallas_call(
        _cumsum_body,
        grid=(bsz, seq // ts),
        in_specs=[pl.BlockSpec((1, ts, LANES), lambda b, j: (b, j, 0)),
                  pl.BlockSpec((1, LANES), lambda b, j: (0, 0))],
        out_specs=pl.BlockSpec((1, ts, LANES), lambda b, j: (b, j, 0)),
        out_shape=jax.ShapeDtypeStruct((bsz, seq, LANES), F32),
        scratch_shapes=[pltpu.VMEM((1, LANES), F32)],
        compiler_params=_cparams(("parallel", "arbitrary")),
        name="forget_cumsum",
    )(f_raw, b_pad)


def _pad_cols(w, n):
    return jnp.pad(w, ((0, 0), (0, n - w.shape[1])))


def _mixer_fox(xb2, bsz, seq, w_in, b_forget, hp=2, tq=512):
    tq = min(tq, seq)
    pos = jnp.arange(seq)
    qkv = _proj(xb2, w_in[:, :3 * D_ATTN], pos, qcols=D_ATTN).reshape(bsz, seq, 3 * D_ATTN)
    f_raw = _proj(xb2, _pad_cols(w_in[:, 3 * D_ATTN:], LANES), pos, out_dtype=F32).reshape(bsz, seq, LANES)
    cum = _forget_cumsum(f_raw, _pad_cols(b_forget.reshape(1, N_HEADS), LANES))[..., :N_HEADS]
    cum_row = cum.transpose(0, 2, 1)
    cum_col = cum.reshape(bsz, seq, N_HEADS // hp, hp).transpose(0, 2, 1, 3)
    o = _flash("fox", qkv, 0, qkv, N_HEADS, qkv, 2 * N_HEADS, (cum_col, cum_row), hp, tq)
    return o.reshape(bsz * seq, D_ATTN)


def _mixer_moba(xb2, bsz, seq, w_in, hp=2, tq=512):
    tq = min(tq, seq)
    pos = jnp.arange(seq)
    qk = _proj(xb2, w_in[:, :2 * D_ATTN], pos, half=64, qcols=D_ATTN).reshape(bsz, seq, 2 * D_ATTN)
    v = _proj(xb2, w_in[:, 2 * D_ATTN:], pos).reshape(bsz, seq, D_ATTN)
    assert seq % MOBA_BLOCK == 0
    o = _flash("moba", qk, 0, qk, N_HEADS, v, 0, None, hp, tq)
    return o.reshape(bsz * seq, D_ATTN)


def _kiprep_body(r_ref, g_ref, b_ref, cos_ref, sin_ref, a_ref, b2_ref):
    x = r_ref[...]
    lane = lax.broadcasted_iota(jnp.int32, x.shape, 1)
    inside = lane < IDX_DIM
    mu = jnp.sum(jnp.where(inside, x, 0.0), axis=-1, keepdims=True) / IDX_DIM
    xc = jnp.where(inside, x - mu, 0.0)
    var = jnp.sum(xc * xc, axis=-1, keepdims=True) / IDX_DIM
    y = xc * lax.rsqrt(var + LN_EPS) * g_ref[...] + b_ref[...]
    y = y * cos_ref[...] + _rotate_half(y, IDX_DIM // 2) * sin_ref[...]
    y = jnp.where(inside, y, 0.0)
    a_ref[...] = y.astype(BF16)
    b2_ref[...] = pltpu.roll(y, IDX_DIM, 1).astype(BF16)


def _ki_prep(raw2, g, b, seq, tm=512):
    m = raw2.shape[0]
    tm = min(tm, seq)
    cos, sin = _rope_tables(jnp.arange(seq), IDX_DIM // 2)
    nsb = seq // tm
    row = lambda i: (i, 0)
    const = lambda i: (0, 0)
    return pl.pallas_call(
        _kiprep_body,
        grid=(m // tm,),
        in_specs=[pl.BlockSpec((tm, LANES), row), pl.BlockSpec((1, LANES), const), pl.BlockSpec((1, LANES), const),
                  pl.BlockSpec((tm, LANES), lambda i: (i % nsb, 0)), pl.BlockSpec((tm, LANES), lambda i: (i % nsb, 0))],
        out_specs=[pl.BlockSpec((tm, LANES), row)] * 2,
        out_shape=[jax.ShapeDtypeStruct((m, LANES), BF16)] * 2,
        compiler_params=_cparams(("parallel",)),
        name="dsa_ki_prep",
    )(raw2, _pad_cols(g.reshape(1, IDX_DIM), LANES), _pad_cols(b.reshape(1, IDX_DIM), LANES), cos, sin)


def _dsa_body(qi_ref, wr_ref, kia_ref, kib_ref, q_ref, k_ref, v_ref, o_ref, key_ref, bias_ref,
              *, tq, ch, topk, wscale):
    i = pl.program_id(1)
    seq = k_ref.shape[1]
    nch = (i * tq + tq + ch - 1) // ch
    nsl = ch // LANES
    wi = wr_ref[0][:, IDX_DIM:IDX_DIM + IDX_HEADS] * wscale
    rowpos = i * tq + lax.broadcasted_iota(jnp.int32, (tq, 1), 0)
    lane_ch = lax.broadcasted_iota(jnp.int32, (tq, ch), 1)
    lane_1 = lax.broadcasted_iota(jnp.int32, (tq, LANES), 1)

    def score_chunk(c, _):
        off = pl.multiple_of(c * ch, ch)
        ka = kia_ref[0, pl.ds(off, ch), :]
        kb = kib_ref[0, pl.ds(off, ch), :]
        sc = jnp.zeros((tq, ch), F32)
        for hpair in range(IDX_HEADS // 2):
            qp = qi_ref[0, :, hpair * LANES:(hpair + 1) * LANES]
            for t, kk in enumerate((ka, kb)):
                h = 2 * hpair + t
                lg = lax.dot_general(qp, kk, _DN_T, preferred_element_type=F32)
                sc = sc + wi[:, h:h + 1] * jnp.maximum(lg, 0.0)
        bits = pltpu.bitcast(sc, jnp.int32)
        bits = jnp.where(bits == INT_MIN, 0, bits)
        key = bits ^ ((bits >> 31) & 0x7FFFFFFF)
        key = jnp.where(off + lane_ch <= rowpos, key, NEG_INF_KEY)
        key_ref[:, pl.ds(off, ch)] = key
        return 0

    lax.fori_loop(0, nch, score_chunk, 0)

    def count(pred):
        def cb(c, acc):
            off = pl.multiple_of(c * ch, ch)
            blk = key_ref[:, pl.ds(off, ch)]
            for s_ in range(nsl):
                acc = acc + jnp.where(pred(blk[:, s_ * LANES:(s_ + 1) * LANES], off + s_ * LANES + lane_1), 1, 0)
            return acc
        acc = lax.fori_loop(0, nch, cb, jnp.zeros((tq, LANES), jnp.int32))
        return jnp.sum(acc.astype(F32), axis=-1, keepdims=True).astype(jnp.int32)

    def bisect_val(t, lo):
        cand = lo + (jnp.int32(1) << (31 - t))
        return jnp.where(count(lambda kv, idx: kv >= cand) >= topk, cand, lo)

    thr = lax.fori_loop(0, 32, bisect_val, jnp.full((tq, LANES), INT_MIN, jnp.int32))
    need = topk - count(lambda kv, idx: kv > thr)
    n_eq = count(lambda kv, idx: kv == thr)
    excess = (n_eq > need) & (thr[:, 0:1] > NEG_INF_KEY)
    any_excess = jnp.max(jnp.where(excess, 1.0, 0.0)) > 0.0

    def tie_path():
        def bisect_idx(t, lo):
            cand = lo + (jnp.int32(1) << (12 - t))
            return jnp.where(count(lambda kv, idx: (kv == thr) & (idx < cand)) < need, cand, lo)
        return lax.fori_loop(0, 13, bisect_idx, jnp.zeros((tq, LANES), jnp.int32))

    jmax = lax.cond(any_excess, tie_path, lambda: jnp.full((tq, LANES), seq, jnp.int32))

    def bias_chunk(c, _):
        off = pl.multiple_of(c * ch, ch)
        blk = key_ref[:, pl.ds(off, ch)]
        for s_ in range(nsl):
            kv = blk[:, s_ * LANES:(s_ + 1) * LANES]
            idx = off + s_ * LANES + lane_1
            keep = ((kv > thr) | ((kv == thr) & (idx <= jmax))) & (idx <= rowpos)
            bias_ref[:, pl.ds(pl.multiple_of(off + s_ * LANES, LANES), LANES)] = jnp.where(keep, 0.0, MASK_VALUE)
        return 0

    lax.fori_loop(0, nch, bias_chunk, 0)

    rows = GQA_GROUP * tq
    for g in range(B_KV_HEADS):
        gs = slice(g * HEAD_DIM, (g + 1) * HEAD_DIM)
        qg = jnp.concatenate(
            [q_ref[0, :, (g * GQA_GROUP + r) * HEAD_DIM:(g * GQA_GROUP + r + 1) * HEAD_DIM]
             for r in range(GQA_GROUP)], axis=0)

        def att(c, carry):
            off = pl.multiple_of(c * ch, ch)
            bias = bias_ref[:, pl.ds(off, ch)]

            def col_fn(s_, sc):
                return sc + jnp.concatenate([bias[:, s_ * LANES:(s_ + 1) * LANES]] * GQA_GROUP, axis=0)

            return _attn_tile(qg, k_ref[0, pl.ds(off, ch), gs], v_ref[0, pl.ds(off, ch), gs], *carry, col_fn)

        m, l, acc = lax.fori_loop(0, nch, att, _attn_init(rows))
        o = _attn_finish(l, acc)
        for r in range(GQA_GROUP):
            h = g * GQA_GROUP + r
            o_ref[0, :, h * HEAD_DIM:(h + 1) * HEAD_DIM] = o[r * tq:(r + 1) * tq].astype(o_ref.dtype)


def _mixer_dsa(xb2, bsz, seq, w_in, idx_g, idx_b, tq=256, ch=512):
    ch = min(ch, seq)
    pos = jnp.arange(seq)
    nqk = D_ATTN + KV_WIDTH
    qk = _proj(xb2, w_in[:, :nqk], pos, half=64, qcols=D_ATTN).reshape(bsz, seq, nqk)
    v = _proj(xb2, w_in[:, nqk:nqk + KV_WIDTH], pos).reshape(bsz, seq, KV_WIDTH)
    o_qi = nqk + KV_WIDTH
    n_qi = IDX_HEADS * IDX_DIM
    qi = _proj(xb2, w_in[:, o_qi:o_qi + n_qi], pos, half=IDX_DIM // 2).reshape(bsz, seq, n_qi)
    raw2 = _proj(xb2, _pad_cols(w_in[:, o_qi + n_qi:], LANES), pos, out_dtype=F32)
    kia, kib = _ki_prep(raw2, idx_g, idx_b, seq)
    topk = min(IDX_TOPK_MAX, seq // 4)
    body = functools.partial(_dsa_body, tq=tq, ch=ch, topk=topk, wscale=IDX_HEADS ** -0.5 * IDX_DIM ** -0.5)
    res = lambda b, i: (b, 0, 0)
    o = pl.pallas_call(
        body,
        grid=(bsz, seq // tq),
        in_specs=[pl.BlockSpec((1, tq, n_qi), lambda b, i: (b, i, 0)),
                  pl.BlockSpec((1, tq, LANES), lambda b, i: (b, i, 0)),
                  pl.BlockSpec((1, seq, LANES), res), pl.BlockSpec((1, seq, LANES), res),
                  pl.BlockSpec((1, tq, D_ATTN), lambda b, i: (b, i, 0)),
                  pl.BlockSpec((1, seq, KV_WIDTH), lambda b, i: (b, 0, D_ATTN // KV_WIDTH)),
                  pl.BlockSpec((1, seq, KV_WIDTH), res)],
        out_specs=pl.BlockSpec((1, tq, D_ATTN), lambda b, i: (b, i, 0)),
        out_shape=jax.ShapeDtypeStruct((bsz, seq, D_ATTN), BF16),
        scratch_shapes=[pltpu.VMEM((tq, seq), jnp.int32), pltpu.VMEM((tq, seq), F32)],
        compiler_params=_cparams(("parallel", "arbitrary")),
        name="dsa_select_attend",
    )(qi, raw2.reshape(bsz, seq, LANES), kia.reshape(bsz, seq, LANES), kib.reshape(bsz, seq, LANES), qk, qk, v)
    return o.reshape(bsz * seq, D_ATTN)


def _trunk(x, layers):
    bsz, seq, d = x.shape
    x2 = x.reshape(bsz * seq, d)
    xb2 = x2
    for kind, p in enumerate(layers):
        w_in = p["w_in"].astype(BF16)
        if kind == 0:
            a = _mixer_dilated(xb2, bsz, seq, w_in)
        elif kind == 1:
            a = _mixer_dsa(xb2, bsz, seq, w_in, p["idx_norm_g"], p["idx_norm_b"])
        elif kind == 2:
            a = _mixer_fox(xb2, bsz, seq, w_in, p["b_forget"])
        else:
            a = _mixer_moba(xb2, bsz, seq, w_in)
        x1, x1b, logits = _outln(a, p["w_out"].astype(BF16), x2, p["ln1_g"], p["ln1_b"],
                                 p["router_w"].astype(BF16), p["router_b"])
        x2, xb2 = _moe_layer(x1, x1b, logits, p["w_gu"], p["b_gu"], p["w_dn"], p["b_dn"], p["ln2_g"], p["ln2_b"])
    return x2.reshape(bsz, seq, d)


def kernel(x, l0_w_in, l0_w_out, l0_ln1_g, l0_ln1_b, l0_router_w, l0_router_b, l0_w_gu, l0_b_gu, l0_w_dn, l0_b_dn, l0_ln2_g, l0_ln2_b, l1_w_in, l1_idx_norm_g, l1_idx_norm_b, l1_w_out, l1_ln1_g, l1_ln1_b, l1_router_w, l1_router_b, l1_w_gu, l1_b_gu, l1_w_dn, l1_b_dn, l1_ln2_g, l1_ln2_b, l2_w_in, l2_b_forget, l2_w_out, l2_ln1_g, l2_ln1_b, l2_router_w, l2_router_b, l2_w_gu, l2_b_gu, l2_w_dn, l2_b_dn, l2_ln2_g, l2_ln2_b, l3_w_in, l3_w_out, l3_ln1_g, l3_ln1_b, l3_router_w, l3_router_b, l3_w_gu, l3_b_gu, l3_w_dn, l3_b_dn, l3_ln2_g, l3_ln2_b):
    names = ("w_out", "ln1_g", "ln1_b", "router_w", "router_b", "w_gu", "b_gu", "w_dn", "b_dn", "ln2_g", "ln2_b")
    l0 = dict(zip(("w_in",) + names, (l0_w_in, l0_w_out, l0_ln1_g, l0_ln1_b, l0_router_w, l0_router_b,
                                      l0_w_gu, l0_b_gu, l0_w_dn, l0_b_dn, l0_ln2_g, l0_ln2_b)))
    l1 = dict(zip(("w_in", "idx_norm_g", "idx_norm_b") + names,
                  (l1_w_in, l1_idx_norm_g, l1_idx_norm_b, l1_w_out, l1_ln1_g, l1_ln1_b, l1_router_w, l1_router_b,
                   l1_w_gu, l1_b_gu, l1_w_dn, l1_b_dn, l1_ln2_g, l1_ln2_b)))
    l2 = dict(zip(("w_in", "b_forget") + names,
                  (l2_w_in, l2_b_forget, l2_w_out, l2_ln1_g, l2_ln1_b, l2_router_w, l2_router_b,
                   l2_w_gu, l2_b_gu, l2_w_dn, l2_b_dn, l2_ln2_g, l2_ln2_b)))
    l3 = dict(zip(("w_in",) + names, (l3_w_in, l3_w_out, l3_ln1_g, l3_ln1_b, l3_router_w, l3_router_b,
                                      l3_w_gu, l3_b_gu, l3_w_dn, l3_b_dn, l3_ln2_g, l3_ln2_b)))
    return _trunk(x, (l0, l1, l2, l3))
```

```python
import functools

import jax
import jax.numpy as jnp
import numpy as np
from jax import lax
from jax.experimental import pallas as pl
from jax.experimental.pallas import tpu as pltpu

N_HEADS = 16
HEAD_DIM = 128
D_MODEL = 2048
D_ATTN = N_HEADS * HEAD_DIM
ROPE_THETA = 10000.0
MASK_VALUE = -1e30
LN_EPS = 1e-5
DILATED_GROUPS = ((128, 1), (512, 4), (2048, 16))
DIL_MAX = 16
B_KV_HEADS = 4
GQA_GROUP = N_HEADS // B_KV_HEADS
KV_WIDTH = B_KV_HEADS * HEAD_DIM
IDX_HEADS = 16
IDX_DIM = 64
IDX_TOPK_MAX = 256
MOBA_BLOCK = 256
MOBA_TOPK = 3
N_EXPERTS = 32
TOP_K = 4
D_EXPERT = 1024
SWIGLU_ALPHA = 1.702
SWIGLU_LIMIT = 7.0
DEPTH = 4
DEEPNORM_ALPHA = (2 * DEPTH) ** 0.25

LANES = 128
VMEM_LIMIT = 56 * 1024 * 1024
MOE_TM = 256
INT_MIN = -(2 ** 31)
NEG_INF_KEY = -2139095041
LOG2E = 1.4426950408889634
QSCALE = HEAD_DIM ** -0.5 * LOG2E

BF16 = jnp.bfloat16
F32 = jnp.float32
_DN_T = (((1,), (1,)), ((), ()))


def _cparams(sem):
    return pltpu.CompilerParams(dimension_semantics=sem, vmem_limit_bytes=VMEM_LIMIT)


def _rope_tables(pos, half):
    inv_freq = ROPE_THETA ** (-jnp.arange(half, dtype=F32) / half)
    ang = pos.astype(F32)[:, None] * inv_freq[None, :]
    cos = jnp.tile(jnp.cos(ang), (1, LANES // half))
    sin = jnp.tile(jnp.concatenate([-jnp.sin(ang), jnp.sin(ang)], -1), (1, LANES // (2 * half)))
    return cos, sin


def _rotate_half(t, half):
    if 2 * half == LANES:
        return pltpu.roll(t, half, 1)
    lane = lax.broadcasted_iota(jnp.int32, t.shape, 1)
    first = (lane % (2 * half)) < half
    return jnp.where(first, pltpu.roll(t, LANES - half, 1), pltpu.roll(t, half, 1))


def _proj_body(*refs, half, q_tiles):
    if half:
        x_ref, w_ref, cos_ref, sin_ref, o_ref = refs
    else:
        x_ref, w_ref, o_ref = refs
    acc = jnp.dot(x_ref[...].astype(BF16), w_ref[...], preferred_element_type=F32)
    if q_tiles:
        acc = acc * jnp.where(pl.program_id(1) < q_tiles, QSCALE, 1.0)
    if not half:
        o_ref[...] = acc.astype(o_ref.dtype)
        return
    cos = cos_ref[...]
    sin = sin_ref[...]
    for c in range(acc.shape[1] // LANES):
        t = acc[:, c * LANES:(c + 1) * LANES]
        o_ref[:, c * LANES:(c + 1) * LANES] = (t * cos + _rotate_half(t, half) * sin).astype(o_ref.dtype)


def _proj(x2, w, pos, half=0, out_dtype=BF16, tm=512, qcols=0):
    m, k = x2.shape
    n = w.shape[1]
    seq = pos.shape[0]
    tm = min(tm, seq)
    tn = 512 if n % 512 == 0 else (256 if n % 256 == 0 else LANES)
    assert qcols % tn == 0 and seq % tm == 0
    in_specs = [pl.BlockSpec((tm, k), lambda i, j: (i, 0)),
                pl.BlockSpec((k, tn), lambda i, j: (0, j))]
    args = [x2, w]
    if half:
        cos, sin = _rope_tables(pos, half)
        nsb = seq // tm
        in_specs += [pl.BlockSpec((tm, LANES), lambda i, j: (i % nsb, 0))] * 2
        args += [cos, sin]
    return pl.pallas_call(
        functools.partial(_proj_body, half=half, q_tiles=qcols // tn),
        grid=(m // tm, n // tn),
        in_specs=in_specs,
        out_specs=pl.BlockSpec((tm, tn), lambda i, j: (i, j)),
        out_shape=jax.ShapeDtypeStruct((m, n), out_dtype),
        compiler_params=_cparams(("parallel", "arbitrary")),
        name="proj_rope%d" % half,
    )(*args)


def _layer_norm_rows(z, g, b):
    mu = jnp.mean(z, axis=-1, keepdims=True)
    zc = z - mu
    var = jnp.mean(zc * zc, axis=-1, keepdims=True)
    return zc * lax.rsqrt(var + LN_EPS) * g + b


def _outln_body(a_ref, w_ref, x_ref, g_ref, b_ref, rw_ref, rb_ref, xo_ref, lg_ref):
    h = jnp.dot(a_ref[...], w_ref[...], preferred_element_type=F32)
    y = _layer_norm_rows(DEEPNORM_ALPHA * x_ref[...] + h, g_ref[...], b_ref[...])
    xo_ref[...] = y
    lg_ref[...] = jnp.dot(y.astype(BF16), rw_ref[...], preferred_element_type=F32) + rb_ref[...]


def _outln(a, w_out, x2, g, b, rw, rb, tm=256):
    m, d = x2.shape
    row = lambda i: (i, 0)
    const = lambda i: (0, 0)
    return pl.pallas_call(
        _outln_body,
        grid=(m // tm,),
        in_specs=[pl.BlockSpec((tm, d), row), pl.BlockSpec((d, d), const), pl.BlockSpec((tm, d), row),
                  pl.BlockSpec((1, d), const), pl.BlockSpec((1, d), const),
                  pl.BlockSpec((d, N_EXPERTS), const), pl.BlockSpec((1, N_EXPERTS), const)],
        out_specs=[pl.BlockSpec((tm, d), row), pl.BlockSpec((tm, N_EXPERTS), row)],
        out_shape=[jax.ShapeDtypeStruct((m, d), F32), jax.ShapeDtypeStruct((m, N_EXPERTS), F32)],
        compiler_params=_cparams(("parallel",)),
        name="outproj_ln_router",
    )(a, w_out, x2, g.reshape(1, d), b.reshape(1, d), rw, rb.reshape(1, N_EXPERTS))


def _moe_body(be_ref, nb_ref, nx_ref, x_ref, wgu_hbm, bgu_ref, wdn_hbm, bdn_ref, o_ref,
              wgu_f, wdn_f, wgu_b, wdn_b, sem):
    i = pl.program_id(0)
    used = i < nb_ref[0]
    expert = be_ref[i]
    new_expert = (i == 0) | (expert != be_ref[jnp.maximum(i - 1, 0)])

    def weight_copies(e):
        return (pltpu.make_async_copy(wgu_hbm.at[e], wgu_f, sem.at[0]),
                pltpu.make_async_copy(wdn_hbm.at[e], wdn_f, sem.at[1]))

    @pl.when(used & (i == 0))
    def _():
        for c in weight_copies(expert):
            c.start()

    @pl.when(used & new_expert)
    def _():
        for c in weight_copies(expert):
            c.wait()
        rows = 64

        def cast(src, dst):
            def body(r, _):
                sl = pl.ds(pl.multiple_of(r * rows, rows), rows)
                dst[sl, :] = src[sl, :].astype(BF16)
                return 0
            lax.fori_loop(0, src.shape[0] // rows, body, 0)

        cast(wgu_f, wgu_b)
        cast(wdn_f, wdn_b)

        @pl.when(nx_ref[i] >= 0)
        def _():
            for c in weight_copies(nx_ref[i]):
                c.start()

    @pl.when(used)
    def _():
        h = jnp.dot(x_ref[...].astype(BF16), wgu_b[...], preferred_element_type=F32) + bgu_ref[0]
        glu = jnp.minimum(h[:, :D_EXPERT], SWIGLU_LIMIT)
        lin = jnp.clip(h[:, D_EXPERT:], -SWIGLU_LIMIT, SWIGLU_LIMIT)
        act = glu * jax.nn.sigmoid(SWIGLU_ALPHA * glu) * (lin + 1.0)
        y = jnp.dot(act.astype(BF16), wdn_b[...], preferred_element_type=F32) + bdn_ref[0]
        o_ref[...] = y.astype(o_ref.dtype)

    @pl.when(jnp.logical_not(used))
    def _():
        o_ref[...] = jnp.zeros_like(o_ref)


def _moe_experts(x_rows, block_exp, n_used, next_exp, w_gu, b_gu, w_dn, b_dn):
    n_rows, d = x_rows.shape
    n_blocks = n_rows // MOE_TM
    grid_spec = pltpu.PrefetchScalarGridSpec(
        num_scalar_prefetch=3,
        grid=(n_blocks,),
        in_specs=[pl.BlockSpec((MOE_TM, d), lambda i, be, nb, nx: (i, 0)),
                  pl.BlockSpec(memory_space=pl.ANY),
                  pl.BlockSpec((1, 1, 2 * D_EXPERT), lambda i, be, nb, nx: (be[i], 0, 0)),
                  pl.BlockSpec(memory_space=pl.ANY),
                  pl.BlockSpec((1, 1, d), lambda i, be, nb, nx: (be[i], 0, 0))],
        out_specs=pl.BlockSpec((MOE_TM, d), lambda i, be, nb, nx: (i, 0)),
        scratch_shapes=[pltpu.VMEM((d, 2 * D_EXPERT), F32), pltpu.VMEM((D_EXPERT, d), F32),
                        pltpu.VMEM((d, 2 * D_EXPERT), BF16), pltpu.VMEM((D_EXPERT, d), BF16),
                        pltpu.SemaphoreType.DMA((2,))],
    )
    return pl.pallas_call(
        _moe_body,
        grid_spec=grid_spec,
        out_shape=jax.ShapeDtypeStruct((n_rows, d), F32),
        compiler_params=_cparams(("arbitrary",)),
        name="moe_experts",
    )(block_exp, n_used, next_exp, x_rows, w_gu, b_gu.reshape(N_EXPERTS, 1, -1), w_dn,
      b_dn.reshape(N_EXPERTS, 1, -1))


def _combine_body(*refs):
    y_refs = refs[:TOP_K]
    gt_ref, x_ref, g_ref, b_ref, xo_ref, xb_ref = refs[TOP_K:]
    gt = gt_ref[...]
    y = gt[:, 0:1] * y_refs[0][...].astype(F32)
    for k in range(1, TOP_K):
        y = y + gt[:, k:k + 1] * y_refs[k][...].astype(F32)
    out = _layer_norm_rows(DEEPNORM_ALPHA * x_ref[...] + y, g_ref[...], b_ref[...])
    xo_ref[...] = out
    xb_ref[...] = out.astype(BF16)


def _combine_ln(y4, gates, x2, g, b, tm=256):
    m, d = x2.shape
    nb = m // tm
    row = lambda i: (i, 0)
    const = lambda i: (0, 0)
    y_specs = [pl.BlockSpec((tm, d), functools.partial(lambda i, k: (k * nb + i, 0), k=k)) for k in range(TOP_K)]
    return pl.pallas_call(
        _combine_body,
        grid=(nb,),
        in_specs=y_specs + [pl.BlockSpec((tm, TOP_K), row),
                            pl.BlockSpec((tm, d), row), pl.BlockSpec((1, d), const), pl.BlockSpec((1, d), const)],
        out_specs=[pl.BlockSpec((tm, d), row), pl.BlockSpec((tm, d), row)],
        out_shape=[jax.ShapeDtypeStruct((m, d), F32), jax.ShapeDtypeStruct((m, d), BF16)],
        compiler_params=_cparams(("parallel",)),
        name="moe_combine_ln",
    )(*([y4] * TOP_K), gates, x2, g.reshape(1, d), b.reshape(1, d))


def _moe_layer(x1, logits, w_gu, b_gu, w_dn, b_dn, g, b):
    n_tok, d = x1.shape
    n_assign = n_tok * TOP_K
    top_logit, top_exp = lax.top_k(logits, TOP_K)
    gates = jax.nn.softmax(top_logit, axis=-1)
    flat_exp = top_exp.reshape(-1).astype(jnp.int32)
    experts = jnp.arange(N_EXPERTS, dtype=jnp.int32)
    counts = jnp.sum((flat_exp[:, None] == experts[None, :]).astype(jnp.int32), axis=0)
    padded = (counts + MOE_TM - 1) // MOE_TM * MOE_TM
    pad_end = jnp.cumsum(padded)
    n_blocks = n_assign // MOE_TM + N_EXPERTS
    n_rows = n_blocks * MOE_TM
    block_start = jnp.arange(n_blocks, dtype=jnp.int32) * MOE_TM
    block_exp = jnp.minimum(jnp.sum((pad_end[None, :] <= block_start[:, None]).astype(jnp.int32), axis=1),
                            N_EXPERTS - 1).astype(jnp.int32)
    n_used = (pad_end[-1] // MOE_TM).astype(jnp.int32).reshape(1)
    later = (experts[None, :] > experts[:, None]) & (counts[None, :] > 0)
    nxt = jnp.min(jnp.where(later, experts[None, :], N_EXPERTS), axis=1)
    nxt = jnp.where(nxt == N_EXPERTS, -1, nxt)
    next_exp = jnp.sum(jnp.where(block_exp[:, None] == experts[None, :], nxt[None, :], 0), axis=1).astype(jnp.int32)
    idx_bits = (n_assign + N_EXPERTS * MOE_TM - 1).bit_length()
    a_idx = jnp.arange(n_assign, dtype=jnp.int32)
    f_idx = n_assign + jnp.arange(N_EXPERTS * MOE_TM, dtype=jnp.int32)
    f_num = jnp.arange(MOE_TM, dtype=jnp.int32)[None, :]
    f_major = jnp.where(f_num < (padded - counts)[:, None], 2 * experts[:, None] + 1, 2 * N_EXPERTS).reshape(-1)
    keys = jnp.concatenate([(2 * flat_exp << idx_bits) | a_idx, (f_major << idx_bits) | f_idx])
    src = jnp.sort(keys) & ((1 << idx_bits) - 1)
    row_tok = jnp.where(src < n_assign, src // TOP_K, 0)[:n_rows]
    _, inv = lax.sort_key_val(src, jnp.arange(src.shape[0], dtype=jnp.int32))
    pos = inv[:n_assign]
    x_rows = x1[row_tok]
    y_rows = _moe_experts(x_rows, block_exp, n_used, next_exp, w_gu, b_gu, w_dn, b_dn)
    y4 = y_rows[pos.reshape(n_tok, TOP_K).T.reshape(-1)]
    return _combine_ln(y4, gates, x1, g, b)


def _attn_tile(q, k, v, m, l, acc, col_fn):
    s = lax.dot_general(q, k, _DN_T, preferred_element_type=F32)
    cols = [col_fn(c, s[:, c * LANES:(c + 1) * LANES]) for c in range(s.shape[1] // LANES)]
    m_new = jnp.maximum(m, jnp.max(functools.reduce(jnp.maximum, cols), axis=-1, keepdims=True))
    alpha = jnp.exp2(m - m_new)
    ps = [jnp.exp2(c - m_new) for c in cols]
    l = alpha * l + functools.reduce(jnp.add, ps)
    p = jnp.concatenate([x.astype(BF16) for x in ps], axis=1)
    acc = alpha * acc + jnp.dot(p, v, preferred_element_type=F32)
    return m_new, l, acc


def _attn_init(rows):
    return (jnp.full((rows, LANES), -jnp.inf, F32), jnp.zeros((rows, LANES), F32),
            jnp.zeros((rows, HEAD_DIM), F32))


def _attn_finish(l, acc):
    return acc / jnp.sum(l, axis=-1, keepdims=True)


def _band_body(*refs, has_prev, is_last, planes, rows):
    if has_prev:
        q_ref, kc_ref, kp_ref, vc_ref, vp_ref, oi_ref, li_ref = refs[:7]
        outs = refs[7:]
    else:
        q_ref, kc_ref, kp_ref, vc_ref, vp_ref = refs[:5]
        outs = refs[5:]
    o_ref = outs[0]
    j = pl.program_id(2)
    w = planes * rows

    def local(t):
        return (t % rows) * planes + t // rows

    tq = lax.broadcasted_iota(jnp.int32, (w, 2 * w), 0)
    tk = lax.broadcasted_iota(jnp.int32, (w, 2 * w), 1)
    cur = tk >= w
    dist = (w + local(tq)) - (local(tk % w) + jnp.where(cur, w, 0))
    mask = (dist >= 0) & (dist <= LANES) & (cur | (j > 0))
    lane = lax.broadcasted_iota(jnp.int32, (w, LANES), 1)
    lse_all = jnp.zeros((w, LANES), F32)

    def blk(ref, hs):
        return ref[0, :, 0, :, hs].reshape(w, HEAD_DIM)

    if has_prev:
        lse_in = li_ref[0, :, 0].reshape(w, LANES)

    for h in range(N_HEADS):
        hs = slice(h * HEAD_DIM, (h + 1) * HEAD_DIM)
        kcat = jnp.concatenate([blk(kp_ref, hs), blk(kc_ref, hs)], axis=0)
        vcat = jnp.concatenate([blk(vp_ref, hs), blk(vc_ref, hs)], axis=0)
        s = lax.dot_general(blk(q_ref, hs), kcat, _DN_T, preferred_element_type=F32)
        s = jnp.where(mask, s, MASK_VALUE)
        m = jnp.max(s, axis=-1, keepdims=True)
        p = jnp.exp2(s - m)
        l = jnp.sum(p, axis=-1, keepdims=True)
        o = jnp.dot((p / l).astype(BF16), vcat, preferred_element_type=F32)
        lse = m + jnp.log2(l)
        if has_prev:
            lse_prev = lse_in[:, h:h + 1]
            mx = jnp.maximum(lse, lse_prev)
            e_new = jnp.exp2(lse - mx)
            e_old = jnp.exp2(lse_prev - mx)
            tot = e_new + e_old
            o = (e_new * o + e_old * blk(oi_ref, hs)) / tot
            lse = mx + jnp.log2(tot)
        o_ref[0, :, 0, :, hs] = o.astype(o_ref.dtype).reshape(planes, rows, HEAD_DIM)
        lse_all = jnp.where(lane == h, lse, lse_all)
    if not is_last:
        outs[1][0, :, 0] = lse_all.reshape(planes, rows, LANES)


def _band_stage(qk, v, prev, dil, is_last):
    bsz, _, sub, _ = v.shape
    planes = DIL_MAX // dil
    groups = DIL_MAX // planes
    rows = (LANES if planes < DIL_MAX else 2 * LANES) // planes
    view = lambda a: a.reshape(bsz, planes, groups, sub, a.shape[-1])
    nb = sub // rows
    blk = lambda col, prevblk: pl.BlockSpec(
        (1, planes, 1, rows, D_ATTN),
        (lambda b, g, j: (b, 0, g, jnp.maximum(j - 1, 0), col)) if prevblk else (lambda b, g, j: (b, 0, g, j, col)))
    in_specs = [blk(0, False), blk(1, False), blk(1, True), blk(0, False), blk(0, True)]
    args = [view(qk)] * 3 + [view(v)] * 2
    o_spec = blk(0, False)
    l_spec = pl.BlockSpec((1, planes, 1, rows, LANES), lambda b, g, j: (b, 0, g, j, 0))
    if prev is not None:
        in_specs += [o_spec, l_spec]
        args += [view(prev[0]), view(prev[1])]
    out_specs = [o_spec] if is_last else [o_spec, l_spec]
    out_shape = [jax.ShapeDtypeStruct((bsz, planes, groups, sub, D_ATTN), BF16 if is_last else F32)]
    if not is_last:
        out_shape.append(jax.ShapeDtypeStruct((bsz, planes, groups, sub, LANES), F32))
    outs = pl.pallas_call(
        functools.partial(_band_body, has_prev=prev is not None, is_last=is_last, planes=planes, rows=rows),
        grid=(bsz, groups, nb),
        in_specs=in_specs, out_specs=out_specs, out_shape=out_shape,
        compiler_params=_cparams(("parallel", "parallel", "arbitrary")),
        name="dilated_band_d%d" % dil,
    )(*args)
    unview = lambda a: a.reshape(bsz, DIL_MAX, sub, a.shape[-1])
    if is_last:
        return unview(outs[0])
    return unview(outs[0]), unview(outs[1])


def _mixer_dilated(xb2, bsz, seq, w_in):
    d = xb2.shape[1]
    sub = seq // DIL_MAX
    assert seq % (DIL_MAX * 2 * LANES) == 0
    xp = xb2.reshape(bsz, sub, DIL_MAX, d).transpose(0, 2, 1, 3).reshape(bsz * seq, d)
    pos = (jnp.arange(sub)[None, :] * DIL_MAX + jnp.arange(DIL_MAX)[:, None]).reshape(-1)
    qk = _proj(xp, w_in[:, :2 * D_ATTN], pos, half=64, qcols=D_ATTN).reshape(bsz, DIL_MAX, sub, 2 * D_ATTN)
    v = _proj(xp, w_in[:, 2 * D_ATTN:], pos).reshape(bsz, DIL_MAX, sub, D_ATTN)
    prev = None
    for g, (window, dil) in enumerate(DILATED_GROUPS):
        assert window // dil == LANES and DIL_MAX % dil == 0
        prev = _band_stage(qk, v, prev, dil, g == len(DILATED_GROUPS) - 1)
    return prev.transpose(0, 2, 1, 3).reshape(bsz * seq, D_ATTN)


def _flash_body(*refs, kind, hp, tq):
    if kind == "fox":
        q_ref, k_ref, v_ref, cq_ref, ck_ref, o_ref = refs
    else:
        q_ref, k_ref, v_ref, o_ref, kmean_ref = refs
    hg = pl.program_id(1)
    i = pl.program_id(2)
    seq = k_ref.shape[1]
    nsl = tq // LANES
    rel = lax.broadcasted_iota(jnp.int32, (tq, LANES), 0) - lax.broadcasted_iota(jnp.int32, (tq, LANES), 1)
    causal = [rel >= c * LANES for c in range(nsl)]
    heads = []

    if kind == "moba":
        nblk = seq // MOBA_BLOCK
        spb = MOBA_BLOCK // LANES
        bpt = tq // MOBA_BLOCK

        @pl.when(i == 0)
        def _():
            kmean_ref[...] = jnp.zeros_like(kmean_ref)
            for hh in range(hp):
                hs = slice(hh * HEAD_DIM, (hh + 1) * HEAD_DIM)
                for n in range(nblk):
                    kb = k_ref[0, n * MOBA_BLOCK:(n + 1) * MOBA_BLOCK, hs].astype(F32)
                    kmean_ref[hh, n:n + 1, :] = jnp.sum(kb, axis=0, keepdims=True) / MOBA_BLOCK

        row_blk = lax.broadcasted_iota(jnp.int32, (tq, 1), 0) // MOBA_BLOCK
        own = (i * bpt + row_blk).astype(F32)
        blk_id = lax.broadcasted_iota(jnp.int32, (tq, LANES), 1).astype(F32)

    for hh in range(hp):
        hs = slice(hh * HEAD_DIM, (hh + 1) * HEAD_DIM)
        q = q_ref[0, :, hs]
        if kind == "fox":
            cq = jnp.broadcast_to(cq_ref[0, 0, :, hh:hh + 1], (tq, LANES))
            heads.append((hs, q, cq, hg * hp + hh))
        else:
            gate = lax.dot_general(q, kmean_ref[hh].astype(BF16), _DN_T, preferred_element_type=F32)
            gate = jnp.where(blk_id < own, gate, -jnp.inf)
            sel = jnp.zeros(gate.shape, F32)
            for _ in range(min(MOBA_TOPK, nblk)):
                mx = jnp.max(gate, axis=-1, keepdims=True)
                first = jnp.min(jnp.where(gate == mx, blk_id, float(LANES)), axis=-1, keepdims=True)
                pick = blk_id == first
                sel = jnp.where(pick & (first < own), 1.0, sel)
                gate = jnp.where(pick, -jnp.inf, gate)
            heads.append((hs, q, sel, None))

    def step(j, carry, diag):
        off = pl.multiple_of(j * tq, tq)
        out = []
        for (hs, q, aux, head), (m, l, acc) in zip(heads, carry):
            k = k_ref[0, pl.ds(off, tq), hs]
            v = v_ref[0, pl.ds(off, tq), hs]
            if kind == "fox":
                ck = ck_ref[0, pl.ds(head, 1), pl.ds(off, tq)]

                def col_fn(c, sc):
                    sc = sc + (aux - ck[:, c * LANES:(c + 1) * LANES])
                    return jnp.where(causal[c], sc, MASK_VALUE) if diag else sc
            else:
                rowsel = [jnp.max(jnp.where(blk_id == (j * bpt + n).astype(F32), aux, 0.0),
                                  axis=-1, keepdims=True) > 0.0 for n in range(bpt)]

                def col_fn(c, sc):
                    keep = rowsel[c // spb]
                    if diag:
                        keep = keep | ((row_blk == c // spb) & causal[c])
                    return jnp.where(keep, sc, MASK_VALUE)
            out.append(_attn_tile(q, k, v, m, l, acc, col_fn))
        return tuple(out)

    carry = tuple(_attn_init(tq) for _ in range(hp))
    carry = lax.fori_loop(0, i, functools.partial(step, diag=False), carry)
    carry = step(i, carry, True)
    for (hs, _, _, _), (m, l, acc) in zip(heads, carry):
        o_ref[0, :, hs] = _attn_finish(l, acc).astype(o_ref.dtype)


def _flash(kind, q_arr, q_off, k_arr, k_off, v_arr, v_off, extra, hp, tq):
    bsz, seq, _ = q_arr.shape
    wid = hp * HEAD_DIM
    hgs = N_HEADS // hp
    qo, ko, vo = q_off // hp, k_off // hp, v_off // hp
    in_specs = [pl.BlockSpec((1, tq, wid), lambda b, h, i: (b, i, qo + h)),
                pl.BlockSpec((1, seq, wid), lambda b, h, i: (b, 0, ko + h)),
                pl.BlockSpec((1, seq, wid), lambda b, h, i: (b, 0, vo + h))]
    args = [q_arr, k_arr, v_arr]
    scratch = []
    if kind == "fox":
        cum_col, cum_row = extra
        in_specs += [pl.BlockSpec((1, 1, tq, hp), lambda b, h, i: (b, h, i, 0)),
                     pl.BlockSpec((1, N_HEADS, seq), lambda b, h, i: (b, 0, 0))]
        args += [cum_col, cum_row]
    else:
        assert seq // MOBA_BLOCK <= LANES and tq % MOBA_BLOCK == 0
        scratch = [pltpu.VMEM((hp, LANES, HEAD_DIM), F32)]
    return pl.pallas_call(
        functools.partial(_flash_body, kind=kind, hp=hp, tq=tq),
        grid=(bsz, hgs, seq // tq),
        in_specs=in_specs,
        out_specs=pl.BlockSpec((1, tq, wid), lambda b, h, i: (b, i, h)),
        out_shape=jax.ShapeDtypeStruct((bsz, seq, D_ATTN), BF16),
        scratch_shapes=scratch,
        compiler_params=_cparams(("parallel", "parallel", "arbitrary")),
        name="flash_" + kind,
    )(*args)


def _cumsum_body(f_ref, b_ref, o_ref, carry_ref):
    j = pl.program_id(1)

    @pl.when(j == 0)
    def _():
        carry_ref[...] = jnp.zeros_like(carry_ref)

    ts = f_ref.shape[1]
    logf = jax.nn.log_sigmoid(f_ref[0] + b_ref[...])
    tri = (lax.broadcasted_iota(jnp.int32, (ts, ts), 0) >= lax.broadcasted_iota(jnp.int32, (ts, ts), 1)).astype(F32)
    cum = jnp.dot(tri, logf, precision=lax.Precision.HIGHEST, preferred_element_type=F32) + carry_ref[...]
    o_ref[0] = cum * LOG2E
    carry_ref[...] = cum[ts - 1:ts, :]


def _forget_cumsum(f_raw, b_pad, ts=256):
    bsz, seq, _ = f_raw.shape
    return pl.pallas_call(
        _cumsum_body,
        grid=(bsz, seq // ts),
        in_specs=[pl.BlockSpec((1, ts, LANES), lambda b, j: (b, j, 0)),
                  pl.BlockSpec((1, LANES), lambda b, j: (0, 0))],
        out_specs=pl.BlockSpec((1, ts, LANES), lambda b, j: (b, j, 0)),
        out_shape=jax.ShapeDtypeStruct((bsz, seq, LANES), F32),
        scratch_shapes=[pltpu.VMEM((1, LANES), F32)],
        compiler_params=_cparams(("parallel", "arbitrary")),
        name="forget_cumsum",
    )(f_raw, b_pad)


def _pad_cols(w, n):
    return jnp.pad(w, ((0, 0), (0, n - w.shape[1])))


def _mixer_fox(xb2, bsz, seq, w_in, b_forget, hp=2, tq=512):
    tq = min(tq, seq)
    pos = jnp.arange(seq)
    qkv = _proj(xb2, w_in[:, :3 * D_ATTN], pos, qcols=D_ATTN).reshape(bsz, seq, 3 * D_ATTN)
    f_raw = _proj(xb2, _pad_cols(w_in[:, 3 * D_ATTN:], LANES), pos, out_dtype=F32).reshape(bsz, seq, LANES)
    cum = _forget_cumsum(f_raw, _pad_cols(b_forget.reshape(1, N_HEADS), LANES))[..., :N_HEADS]
    cum_row = cum.transpose(0, 2, 1)
    cum_col = cum.reshape(bsz, seq, N_HEADS // hp, hp).transpose(0, 2, 1, 3)
    o = _flash("fox", qkv, 0, qkv, N_HEADS, qkv, 2 * N_HEADS, (cum_col, cum_row), hp, tq)
    return o.reshape(bsz * seq, D_ATTN)


def _mixer_moba(xb2, bsz, seq, w_in, hp=2, tq=512):
    tq = min(tq, seq)
    pos = jnp.arange(seq)
    qk = _proj(xb2, w_in[:, :2 * D_ATTN], pos, half=64, qcols=D_ATTN).reshape(bsz, seq, 2 * D_ATTN)
    v = _proj(xb2, w_in[:, 2 * D_ATTN:], pos).reshape(bsz, seq, D_ATTN)
    assert seq % MOBA_BLOCK == 0
    o = _flash("moba", qk, 0, qk, N_HEADS, v, 0, None, hp, tq)
    return o.reshape(bsz * seq, D_ATTN)


def _kiprep_body(r_ref, g_ref, b_ref, cos_ref, sin_ref, a_ref, b2_ref):
    x = r_ref[...]
    lane = lax.broadcasted_iota(jnp.int32, x.shape, 1)
    inside = lane < IDX_DIM
    mu = jnp.sum(jnp.where(inside, x, 0.0), axis=-1, keepdims=True) / IDX_DIM
    xc = jnp.where(inside, x - mu, 0.0)
    var = jnp.sum(xc * xc, axis=-1, keepdims=True) / IDX_DIM
    y = xc * lax.rsqrt(var + LN_EPS) * g_ref[...] + b_ref[...]
    y = y * cos_ref[...] + _rotate_half(y, IDX_DIM // 2) * sin_ref[...]
    y = jnp.where(inside, y, 0.0)
    a_ref[...] = y.astype(BF16)
    b2_ref[...] = pltpu.roll(y, IDX_DIM, 1).astype(BF16)


def _ki_prep(raw2, g, b, seq, tm=512):
    m = raw2.shape[0]
    tm = min(tm, seq)
    cos, sin = _rope_tables(jnp.arange(seq), IDX_DIM // 2)
    nsb = seq // tm
    row = lambda i: (i, 0)
    const = lambda i: (0, 0)
    return pl.pallas_call(
        _kiprep_body,
        grid=(m // tm,),
        in_specs=[pl.BlockSpec((tm, LANES), row), pl.BlockSpec((1, LANES), const), pl.BlockSpec((1, LANES), const),
                  pl.BlockSpec((tm, LANES), lambda i: (i % nsb, 0)), pl.BlockSpec((tm, LANES), lambda i: (i % nsb, 0))],
        out_specs=[pl.BlockSpec((tm, LANES), row)] * 2,
        out_shape=[jax.ShapeDtypeStruct((m, LANES), BF16)] * 2,
        compiler_params=_cparams(("parallel",)),
        name="dsa_ki_prep",
    )(raw2, _pad_cols(g.reshape(1, IDX_DIM), LANES), _pad_cols(b.reshape(1, IDX_DIM), LANES), cos, sin)


def _dsa_body(qi_ref, wr_ref, kia_ref, kib_ref, q_ref, k_ref, v_ref, o_ref, key_ref, bias_ref,
              *, tq, ch, topk, wscale):
    i = pl.program_id(1)
    seq = k_ref.shape[1]
    nch = (i * tq + tq + ch - 1) // ch
    nsl = ch // LANES
    wi = wr_ref[0][:, IDX_DIM:IDX_DIM + IDX_HEADS] * wscale
    rowpos = i * tq + lax.broadcasted_iota(jnp.int32, (tq, 1), 0)
    lane_ch = lax.broadcasted_iota(jnp.int32, (tq, ch), 1)
    lane_1 = lax.broadcasted_iota(jnp.int32, (tq, LANES), 1)

    def score_chunk(c, _):
        off = pl.multiple_of(c * ch, ch)
        ka = kia_ref[0, pl.ds(off, ch), :]
        kb = kib_ref[0, pl.ds(off, ch), :]
        sc = jnp.zeros((tq, ch), F32)
        for hpair in range(IDX_HEADS // 2):
            qp = qi_ref[0, :, hpair * LANES:(hpair + 1) * LANES]
            for t, kk in enumerate((ka, kb)):
                h = 2 * hpair + t
                lg = lax.dot_general(qp, kk, _DN_T, preferred_element_type=F32)
                sc = sc + wi[:, h:h + 1] * jnp.maximum(lg, 0.0)
        bits = pltpu.bitcast(sc, jnp.int32)
        bits = jnp.where(bits == INT_MIN, 0, bits)
        key = bits ^ ((bits >> 31) & 0x7FFFFFFF)
        key = jnp.where(off + lane_ch <= rowpos, key, NEG_INF_KEY)
        key_ref[:, pl.ds(off, ch)] = key
        return 0

    lax.fori_loop(0, nch, score_chunk, 0)

    def count(pred):
        def cb(c, acc):
            off = pl.multiple_of(c * ch, ch)
            blk = key_ref[:, pl.ds(off, ch)]
            for s_ in range(nsl):
                acc = acc + jnp.where(pred(blk[:, s_ * LANES:(s_ + 1) * LANES], off + s_ * LANES + lane_1), 1, 0)
            return acc
        acc = lax.fori_loop(0, nch, cb, jnp.zeros((tq, LANES), jnp.int32))
        return jnp.sum(acc.astype(F32), axis=-1, keepdims=True).astype(jnp.int32)

    def bisect_val(t, lo):
        cand = lo + (jnp.int32(1) << (31 - t))
        return jnp.where(count(lambda kv, idx: kv >= cand) >= topk, cand, lo)

    thr = lax.fori_loop(0, 32, bisect_val, jnp.full((tq, LANES), INT_MIN, jnp.int32))
    need = topk - count(lambda kv, idx: kv > thr)
    n_eq = count(lambda kv, idx: kv == thr)
    excess = (n_eq > need) & (thr[:, 0:1] > NEG_INF_KEY)
    any_excess = jnp.max(jnp.where(excess, 1.0, 0.0)) > 0.0

    def tie_path():
        def bisect_idx(t, lo):
            cand = lo + (jnp.int32(1) << (12 - t))
            return jnp.where(count(lambda kv, idx: (kv == thr) & (idx < cand)) < need, cand, lo)
        return lax.fori_loop(0, 13, bisect_idx, jnp.zeros((tq, LANES), jnp.int32))

    jmax = lax.cond(any_excess, tie_path, lambda: jnp.full((tq, LANES), seq, jnp.int32))

    def bias_chunk(c, _):
        off = pl.multiple_of(c * ch, ch)
        blk = key_ref[:, pl.ds(off, ch)]
        for s_ in range(nsl):
            kv = blk[:, s_ * LANES:(s_ + 1) * LANES]
            idx = off + s_ * LANES + lane_1
            keep = ((kv > thr) | ((kv == thr) & (idx <= jmax))) & (idx <= rowpos)
            bias_ref[:, pl.ds(pl.multiple_of(off + s_ * LANES, LANES), LANES)] = jnp.where(keep, 0.0, MASK_VALUE)
        return 0

    lax.fori_loop(0, nch, bias_chunk, 0)

    rows = GQA_GROUP * tq
    for g in range(B_KV_HEADS):
        gs = slice(g * HEAD_DIM, (g + 1) * HEAD_DIM)
        qg = jnp.concatenate(
            [q_ref[0, :, (g * GQA_GROUP + r) * HEAD_DIM:(g * GQA_GROUP + r + 1) * HEAD_DIM]
             for r in range(GQA_GROUP)], axis=0)

        def att(c, carry):
            off = pl.multiple_of(c * ch, ch)
            bias = bias_ref[:, pl.ds(off, ch)]

            def col_fn(s_, sc):
                return sc + jnp.concatenate([bias[:, s_ * LANES:(s_ + 1) * LANES]] * GQA_GROUP, axis=0)

            return _attn_tile(qg, k_ref[0, pl.ds(off, ch), gs], v_ref[0, pl.ds(off, ch), gs], *carry, col_fn)

        m, l, acc = lax.fori_loop(0, nch, att, _attn_init(rows))
        o = _attn_finish(l, acc)
        for r in range(GQA_GROUP):
            h = g * GQA_GROUP + r
            o_ref[0, :, h * HEAD_DIM:(h + 1) * HEAD_DIM] = o[r * tq:(r + 1) * tq].astype(o_ref.dtype)


def _mixer_dsa(xb2, bsz, seq, w_in, idx_g, idx_b, tq=256, ch=512):
    ch = min(ch, seq)
    pos = jnp.arange(seq)
    nqk = D_ATTN + KV_WIDTH
    qk = _proj(xb2, w_in[:, :nqk], pos, half=64, qcols=D_ATTN).reshape(bsz, seq, nqk)
    v = _proj(xb2, w_in[:, nqk:nqk + KV_WIDTH], pos).reshape(bsz, seq, KV_WIDTH)
    o_qi = nqk + KV_WIDTH
    n_qi = IDX_HEADS * IDX_DIM
    qi = _proj(xb2, w_in[:, o_qi:o_qi + n_qi], pos, half=IDX_DIM // 2).reshape(bsz, seq, n_qi)
    raw2 = _proj(xb2, _pad_cols(w_in[:, o_qi + n_qi:], LANES), pos, out_dtype=F32)
    kia, kib = _ki_prep(raw2, idx_g, idx_b, seq)
    topk = min(IDX_TOPK_MAX, seq // 4)
    body = functools.partial(_dsa_body, tq=tq, ch=ch, topk=topk, wscale=IDX_HEADS ** -0.5 * IDX_DIM ** -0.5)
    res = lambda b, i: (b, 0, 0)
    o = pl.pallas_call(
        body,
        grid=(bsz, seq // tq),
        in_specs=[pl.BlockSpec((1, tq, n_qi), lambda b, i: (b, i, 0)),
                  pl.BlockSpec((1, tq, LANES), lambda b, i: (b, i, 0)),
                  pl.BlockSpec((1, seq, LANES), res), pl.BlockSpec((1, seq, LANES), res),
                  pl.BlockSpec((1, tq, D_ATTN), lambda b, i: (b, i, 0)),
                  pl.BlockSpec((1, seq, KV_WIDTH), lambda b, i: (b, 0, D_ATTN // KV_WIDTH)),
                  pl.BlockSpec((1, seq, KV_WIDTH), res)],
        out_specs=pl.BlockSpec((1, tq, D_ATTN), lambda b, i: (b, i, 0)),
        out_shape=jax.ShapeDtypeStruct((bsz, seq, D_ATTN), BF16),
        scratch_shapes=[pltpu.VMEM((tq, seq), jnp.int32), pltpu.VMEM((tq, seq), F32)],
        compiler_params=_cparams(("parallel", "arbitrary")),
        name="dsa_select_attend",
    )(qi, raw2.reshape(bsz, seq, LANES), kia.reshape(bsz, seq, LANES), kib.reshape(bsz, seq, LANES), qk, qk, v)
    return o.reshape(bsz * seq, D_ATTN)


def _trunk(x, layers):
    bsz, seq, d = x.shape
    x2 = x.reshape(bsz * seq, d)
    xb2 = x2
    for kind, p in enumerate(layers):
        w_in = p["w_in"].astype(BF16)
        if kind == 0:
            a = _mixer_dilated(xb2, bsz, seq, w_in)
        elif kind == 1:
            a = _mixer_dsa(xb2, bsz, seq, w_in, p["idx_norm_g"], p["idx_norm_b"])
        elif kind == 2:
            a = _mixer_fox(xb2, bsz, seq, w_in, p["b_forget"])
        else:
            a = _mixer_moba(xb2, bsz, seq, w_in)
        x1, logits = _outln(a, p["w_out"].astype(BF16), x2, p["ln1_g"], p["ln1_b"],
                            p["router_w"].astype(BF16), p["router_b"])
        x2, xb2 = _moe_layer(x1, logits, p["w_gu"], p["b_gu"], p["w_dn"], p["b_dn"], p["ln2_g"], p["ln2_b"])
    return x2.reshape(bsz, seq, d)


def kernel(x, l0_w_in, l0_w_out, l0_ln1_g, l0_ln1_b, l0_router_w, l0_router_b, l0_w_gu, l0_b_gu, l0_w_dn, l0_b_dn, l0_ln2_g, l0_ln2_b, l1_w_in, l1_idx_norm_g, l1_idx_norm_b, l1_w_out, l1_ln1_g, l1_ln1_b, l1_router_w, l1_router_b, l1_w_gu, l1_b_gu, l1_w_dn, l1_b_dn, l1_ln2_g, l1_ln2_b, l2_w_in, l2_b_forget, l2_w_out, l2_ln1_g, l2_ln1_b, l2_router_w, l2_router_b, l2_w_gu, l2_b_gu, l2_w_dn, l2_b_dn, l2_ln2_g, l2_ln2_b, l3_w_in, l3_w_out, l3_ln1_g, l3_ln1_b, l3_router_w, l3_router_b, l3_w_gu, l3_b_gu, l3_w_dn, l3_b_dn, l3_ln2_g, l3_ln2_b):
    names = ("w_out", "ln1_g", "ln1_b", "router_w", "router_b", "w_gu", "b_gu", "w_dn", "b_dn", "ln2_g", "ln2_b")
    l0 = dict(zip(("w_in",) + names, (l0_w_in, l0_w_out, l0_ln1_g, l0_ln1_b, l0_router_w, l0_router_b,
                                      l0_w_gu, l0_b_gu, l0_w_dn, l0_b_dn, l0_ln2_g, l0_ln2_b)))
    l1 = dict(zip(("w_in", "idx_norm_g", "idx_norm_b") + names,
                  (l1_w_in, l1_idx_norm_g, l1_idx_norm_b, l1_w_out, l1_ln1_g, l1_ln1_b, l1_router_w, l1_router_b,
                   l1_w_gu, l1_b_gu, l1_w_dn, l1_b_dn, l1_ln2_g, l1_ln2_b)))
    l2 = dict(zip(("w_in", "b_forget") + names,
                  (l2_w_in, l2_b_forget, l2_w_out, l2_ln1_g, l2_ln1_b, l2_router_w, l2_router_b,
                   l2_w_gu, l2_b_gu, l2_w_dn, l2_b_dn, l2_ln2_g, l2_ln2_b)))
    l3 = dict(zip(("w_in",) + names, (l3_w_in, l3_w_out, l3_ln1_g, l3_ln1_b, l3_router_w, l3_router_b,
                                      l3_w_gu, l3_b_gu, l3_w_dn, l3_b_dn, l3_ln2_g, l3_ln2_b)))
    return _trunk(x, (l0, l1, l2, l3))
```

```python
import functools

import jax
import jax.numpy as jnp
import numpy as np
from jax import lax
from jax.experimental import pallas as pl
from jax.experimental.pallas import tpu as pltpu

N_HEADS = 16
HEAD_DIM = 128
D_MODEL = 2048
D_ATTN = N_HEADS * HEAD_DIM
ROPE_THETA = 10000.0
MASK_VALUE = -1e30
LN_EPS = 1e-5
DILATED_GROUPS = ((128, 1), (512, 4), (2048, 16))
DIL_MAX = 16
B_KV_HEADS = 4
GQA_GROUP = N_HEADS // B_KV_HEADS
KV_WIDTH = B_KV_HEADS * HEAD_DIM
IDX_HEADS = 16
IDX_DIM = 64
IDX_TOPK_MAX = 256
MOBA_BLOCK = 256
MOBA_TOPK = 3
N_EXPERTS = 32
TOP_K = 4
D_EXPERT = 1024
SWIGLU_ALPHA = 1.702
SWIGLU_LIMIT = 7.0
DEPTH = 4
DEEPNORM_ALPHA = (2 * DEPTH) ** 0.25

LANES = 128
VMEM_LIMIT = 56 * 1024 * 1024
MOE_TM = 256
INT_MIN = -(2 ** 31)
NEG_INF_KEY = -2139095041
LOG2E = 1.4426950408889634
QSCALE = HEAD_DIM ** -0.5 * LOG2E

BF16 = jnp.bfloat16
F32 = jnp.float32
_DN_T = (((1,), (1,)), ((), ()))


def _cparams(sem):
    return pltpu.CompilerParams(dimension_semantics=sem, vmem_limit_bytes=VMEM_LIMIT)


def _rope_tables(pos, half):
    inv_freq = ROPE_THETA ** (-jnp.arange(half, dtype=F32) / half)
    ang = pos.astype(F32)[:, None] * inv_freq[None, :]
    cos = jnp.tile(jnp.cos(ang), (1, LANES // half))
    sin = jnp.tile(jnp.concatenate([-jnp.sin(ang), jnp.sin(ang)], -1), (1, LANES // (2 * half)))
    return cos, sin


def _rotate_half(t, half):
    if 2 * half == LANES:
        return pltpu.roll(t, half, 1)
    lane = lax.broadcasted_iota(jnp.int32, t.shape, 1)
    first = (lane % (2 * half)) < half
    return jnp.where(first, pltpu.roll(t, LANES - half, 1), pltpu.roll(t, half, 1))


def _proj_body(*refs, half, q_tiles):
    if half:
        x_ref, w_ref, cos_ref, sin_ref, o_ref = refs
    else:
        x_ref, w_ref, o_ref = refs
    acc = jnp.dot(x_ref[...].astype(BF16), w_ref[...], preferred_element_type=F32)
    if q_tiles:
        acc = acc * jnp.where(pl.program_id(1) < q_tiles, QSCALE, 1.0)
    if not half:
        o_ref[...] = acc.astype(o_ref.dtype)
        return
    cos = cos_ref[...]
    sin = sin_ref[...]
    for c in range(acc.shape[1] // LANES):
        t = acc[:, c * LANES:(c + 1) * LANES]
        o_ref[:, c * LANES:(c + 1) * LANES] = (t * cos + _rotate_half(t, half) * sin).astype(o_ref.dtype)


def _proj(x2, w, pos, half=0, out_dtype=BF16, tm=512, qcols=0):
    m, k = x2.shape
    n = w.shape[1]
    seq = pos.shape[0]
    tm = min(tm, seq)
    tn = 512 if n % 512 == 0 else (256 if n % 256 == 0 else LANES)
    assert qcols % tn == 0 and seq % tm == 0
    in_specs = [pl.BlockSpec((tm, k), lambda i, j: (i, 0)),
                pl.BlockSpec((k, tn), lambda i, j: (0, j))]
    args = [x2, w]
    if half:
        cos, sin = _rope_tables(pos, half)
        nsb = seq // tm
        in_specs += [pl.BlockSpec((tm, LANES), lambda i, j: (i % nsb, 0))] * 2
        args += [cos, sin]
    return pl.pallas_call(
        functools.partial(_proj_body, half=half, q_tiles=qcols // tn),
        grid=(m // tm, n // tn),
        in_specs=in_specs,
        out_specs=pl.BlockSpec((tm, tn), lambda i, j: (i, j)),
        out_shape=jax.ShapeDtypeStruct((m, n), out_dtype),
        compiler_params=_cparams(("parallel", "arbitrary")),
        name="proj_rope%d" % half,
    )(*args)


def _layer_norm_rows(z, g, b):
    mu = jnp.mean(z, axis=-1, keepdims=True)
    zc = z - mu
    var = jnp.mean(zc * zc, axis=-1, keepdims=True)
    return zc * lax.rsqrt(var + LN_EPS) * g + b


def _outln_body(a_ref, w_ref, x_ref, g_ref, b_ref, rw_ref, rb_ref, xo_ref, gt_ref, te_ref):
    h = jnp.dot(a_ref[...], w_ref[...], preferred_element_type=F32)
    y = _layer_norm_rows(DEEPNORM_ALPHA * x_ref[...] + h, g_ref[...], b_ref[...])
    xo_ref[...] = y
    work = jnp.dot(y.astype(BF16), rw_ref[...], preferred_element_type=F32) + rb_ref[...]
    lane = lax.broadcasted_iota(jnp.int32, work.shape, 1).astype(F32)
    vals, idxs = [], []
    for _ in range(TOP_K):
        mx = jnp.max(work, axis=-1, keepdims=True)
        first = jnp.min(jnp.where(work == mx, lane, float(LANES)), axis=-1, keepdims=True)
        vals.append(mx)
        idxs.append(first)
        work = jnp.where(lane == first, -jnp.inf, work)
    es = [jnp.exp(v - vals[0]) for v in vals]
    denom = functools.reduce(jnp.add, es)
    gates = jnp.zeros(work.shape, F32)
    experts = jnp.zeros(work.shape, F32)
    for k in range(TOP_K):
        gates = jnp.where(lane == k, es[k] / denom, gates)
        experts = jnp.where(lane == k, idxs[k], experts)
    gt_ref[...] = gates
    te_ref[...] = experts.astype(jnp.int32)


def _outln(a, w_out, x2, g, b, rw, rb, tm=256):
    m, d = x2.shape
    row = lambda i: (i, 0)
    const = lambda i: (0, 0)
    rw_pad = _pad_cols(rw, LANES)
    rb_pad = jnp.pad(rb.reshape(1, N_EXPERTS), ((0, 0), (0, LANES - N_EXPERTS)), constant_values=-jnp.inf)
    return pl.pallas_call(
        _outln_body,
        grid=(m // tm,),
        in_specs=[pl.BlockSpec((tm, d), row), pl.BlockSpec((d, d), const), pl.BlockSpec((tm, d), row),
                  pl.BlockSpec((1, d), const), pl.BlockSpec((1, d), const),
                  pl.BlockSpec((d, LANES), const), pl.BlockSpec((1, LANES), const)],
        out_specs=[pl.BlockSpec((tm, d), row), pl.BlockSpec((tm, LANES), row), pl.BlockSpec((tm, LANES), row)],
        out_shape=[jax.ShapeDtypeStruct((m, d), F32), jax.ShapeDtypeStruct((m, LANES), F32),
                   jax.ShapeDtypeStruct((m, LANES), jnp.int32)],
        compiler_params=_cparams(("parallel",)),
        name="outproj_ln_router",
    )(a, w_out, x2, g.reshape(1, d), b.reshape(1, d), rw_pad, rb_pad)


def _moe_body(be_ref, nb_ref, nx_ref, x_ref, wgu_hbm, bgu_ref, wdn_hbm, bdn_ref, o_ref,
              wgu_f, wdn_f, wgu_b, wdn_b, sem):
    i = pl.program_id(0)
    used = i < nb_ref[0]
    expert = be_ref[i]
    new_expert = (i == 0) | (expert != be_ref[jnp.maximum(i - 1, 0)])

    def weight_copies(e):
        return (pltpu.make_async_copy(wgu_hbm.at[e], wgu_f, sem.at[0]),
                pltpu.make_async_copy(wdn_hbm.at[e], wdn_f, sem.at[1]))

    @pl.when(used & (i == 0))
    def _():
        for c in weight_copies(expert):
            c.start()

    @pl.when(used & new_expert)
    def _():
        for c in weight_copies(expert):
            c.wait()
        rows = 64

        def cast(src, dst):
            def body(r, _):
                sl = pl.ds(pl.multiple_of(r * rows, rows), rows)
                dst[sl, :] = src[sl, :].astype(BF16)
                return 0
            lax.fori_loop(0, src.shape[0] // rows, body, 0)

        cast(wgu_f, wgu_b)
        cast(wdn_f, wdn_b)

        @pl.when(nx_ref[i] >= 0)
        def _():
            for c in weight_copies(nx_ref[i]):
                c.start()

    @pl.when(used)
    def _():
        h = jnp.dot(x_ref[...].astype(BF16), wgu_b[...], preferred_element_type=F32) + bgu_ref[0]
        glu = jnp.minimum(h[:, :D_EXPERT], SWIGLU_LIMIT)
        lin = jnp.clip(h[:, D_EXPERT:], -SWIGLU_LIMIT, SWIGLU_LIMIT)
        act = glu * jax.nn.sigmoid(SWIGLU_ALPHA * glu) * (lin + 1.0)
        y = jnp.dot(act.astype(BF16), wdn_b[...], preferred_element_type=F32) + bdn_ref[0]
        o_ref[...] = y.astype(o_ref.dtype)

    @pl.when(jnp.logical_not(used))
    def _():
        o_ref[...] = jnp.zeros_like(o_ref)


def _moe_experts(x_rows, block_exp, n_used, next_exp, w_gu, b_gu, w_dn, b_dn):
    n_rows, d = x_rows.shape
    n_blocks = n_rows // MOE_TM
    grid_spec = pltpu.PrefetchScalarGridSpec(
        num_scalar_prefetch=3,
        grid=(n_blocks,),
        in_specs=[pl.BlockSpec((MOE_TM, d), lambda i, be, nb, nx: (i, 0)),
                  pl.BlockSpec(memory_space=pl.ANY),
                  pl.BlockSpec((1, 1, 2 * D_EXPERT), lambda i, be, nb, nx: (be[i], 0, 0)),
                  pl.BlockSpec(memory_space=pl.ANY),
                  pl.BlockSpec((1, 1, d), lambda i, be, nb, nx: (be[i], 0, 0))],
        out_specs=pl.BlockSpec((MOE_TM, d), lambda i, be, nb, nx: (i, 0)),
        scratch_shapes=[pltpu.VMEM((d, 2 * D_EXPERT), F32), pltpu.VMEM((D_EXPERT, d), F32),
                        pltpu.VMEM((d, 2 * D_EXPERT), BF16), pltpu.VMEM((D_EXPERT, d), BF16),
                        pltpu.SemaphoreType.DMA((2,))],
    )
    return pl.pallas_call(
        _moe_body,
        grid_spec=grid_spec,
        out_shape=jax.ShapeDtypeStruct((n_rows, d), F32),
        compiler_params=_cparams(("arbitrary",)),
        name="moe_experts",
    )(block_exp, n_used, next_exp, x_rows, w_gu, b_gu.reshape(N_EXPERTS, 1, -1), w_dn,
      b_dn.reshape(N_EXPERTS, 1, -1))


def _combine_body(*refs):
    y_refs = refs[:TOP_K]
    gt_ref, x_ref, g_ref, b_ref, xo_ref, xb_ref = refs[TOP_K:]
    gt = gt_ref[...]
    y = gt[:, 0:1] * y_refs[0][...].astype(F32)
    for k in range(1, TOP_K):
        y = y + gt[:, k:k + 1] * y_refs[k][...].astype(F32)
    out = _layer_norm_rows(DEEPNORM_ALPHA * x_ref[...] + y, g_ref[...], b_ref[...])
    xo_ref[...] = out
    xb_ref[...] = out.astype(BF16)


def _combine_ln(y4, gates, x2, g, b, tm=256):
    m, d = x2.shape
    nb = m // tm
    row = lambda i: (i, 0)
    const = lambda i: (0, 0)
    y_specs = [pl.BlockSpec((tm, d), functools.partial(lambda i, k: (k * nb + i, 0), k=k)) for k in range(TOP_K)]
    return pl.pallas_call(
        _combine_body,
        grid=(nb,),
        in_specs=y_specs + [pl.BlockSpec((tm, TOP_K), row),
                            pl.BlockSpec((tm, d), row), pl.BlockSpec((1, d), const), pl.BlockSpec((1, d), const)],
        out_specs=[pl.BlockSpec((tm, d), row), pl.BlockSpec((tm, d), row)],
        out_shape=[jax.ShapeDtypeStruct((m, d), F32), jax.ShapeDtypeStruct((m, d), BF16)],
        compiler_params=_cparams(("parallel",)),
        name="moe_combine_ln",
    )(*([y4] * TOP_K), gates, x2, g.reshape(1, d), b.reshape(1, d))


def _moe_layer(x1, gates, top_exp, w_gu, b_gu, w_dn, b_dn, g, b):
    n_tok, d = x1.shape
    n_assign = n_tok * TOP_K
    flat_exp = top_exp.reshape(-1)
    experts = jnp.arange(N_EXPERTS, dtype=jnp.int32)
    counts = jnp.sum((flat_exp[:, None] == experts[None, :]).astype(jnp.int32), axis=0)
    padded = (counts + MOE_TM - 1) // MOE_TM * MOE_TM
    pad_end = jnp.cumsum(padded)
    n_blocks = n_assign // MOE_TM + N_EXPERTS
    n_rows = n_blocks * MOE_TM
    block_start = jnp.arange(n_blocks, dtype=jnp.int32) * MOE_TM
    block_exp = jnp.minimum(jnp.sum((pad_end[None, :] <= block_start[:, None]).astype(jnp.int32), axis=1),
                            N_EXPERTS - 1).astype(jnp.int32)
    n_used = (pad_end[-1] // MOE_TM).astype(jnp.int32).reshape(1)
    later = (experts[None, :] > experts[:, None]) & (counts[None, :] > 0)
    nxt = jnp.min(jnp.where(later, experts[None, :], N_EXPERTS), axis=1)
    nxt = jnp.where(nxt == N_EXPERTS, -1, nxt)
    next_exp = jnp.sum(jnp.where(block_exp[:, None] == experts[None, :], nxt[None, :], 0), axis=1).astype(jnp.int32)
    idx_bits = (n_assign + N_EXPERTS * MOE_TM - 1).bit_length()
    a_idx = jnp.arange(n_assign, dtype=jnp.int32)
    f_idx = n_assign + jnp.arange(N_EXPERTS * MOE_TM, dtype=jnp.int32)
    f_num = jnp.arange(MOE_TM, dtype=jnp.int32)[None, :]
    f_major = jnp.where(f_num < (padded - counts)[:, None], 2 * experts[:, None] + 1, 2 * N_EXPERTS).reshape(-1)
    keys = jnp.concatenate([(2 * flat_exp << idx_bits) | a_idx, (f_major << idx_bits) | f_idx])
    src = jnp.sort(keys) & ((1 << idx_bits) - 1)
    row_tok = jnp.where(src < n_assign, src // TOP_K, 0)[:n_rows]
    _, inv = lax.sort_key_val(src, jnp.arange(src.shape[0], dtype=jnp.int32))
    pos = inv[:n_assign]
    x_rows = x1[row_tok]
    y_rows = _moe_experts(x_rows, block_exp, n_used, next_exp, w_gu, b_gu, w_dn, b_dn)
    y4 = y_rows[pos.reshape(n_tok, TOP_K).T.reshape(-1)]
    return _combine_ln(y4, gates, x1, g, b)


def _attn_tile(q, k, v, m, l, acc, col_fn):
    s = lax.dot_general(q, k, _DN_T, preferred_element_type=F32)
    cols = [col_fn(c, s[:, c * LANES:(c + 1) * LANES]) for c in range(s.shape[1] // LANES)]
    m_new = jnp.maximum(m, jnp.max(functools.reduce(jnp.maximum, cols), axis=-1, keepdims=True))
    alpha = jnp.exp2(m - m_new)
    ps = [jnp.exp2(c - m_new) for c in cols]
    l = alpha * l + functools.reduce(jnp.add, ps)
    p = jnp.concatenate([x.astype(BF16) for x in ps], axis=1)
    acc = alpha * acc + jnp.dot(p, v, preferred_element_type=F32)
    return m_new, l, acc


def _attn_init(rows):
    return (jnp.full((rows, LANES), -jnp.inf, F32), jnp.zeros((rows, LANES), F32),
            jnp.zeros((rows, HEAD_DIM), F32))


def _attn_finish(l, acc):
    return acc / jnp.sum(l, axis=-1, keepdims=True)


def _band_body(*refs, has_prev, is_last, planes, rows):
    if has_prev:
        q_ref, kc_ref, kp_ref, vc_ref, vp_ref, oi_ref, li_ref = refs[:7]
        outs = refs[7:]
    else:
        q_ref, kc_ref, kp_ref, vc_ref, vp_ref = refs[:5]
        outs = refs[5:]
    o_ref = outs[0]
    j = pl.program_id(2)
    w = planes * rows

    def local(t):
        return (t % rows) * planes + t // rows

    tq = lax.broadcasted_iota(jnp.int32, (w, 2 * w), 0)
    tk = lax.broadcasted_iota(jnp.int32, (w, 2 * w), 1)
    cur = tk >= w
    dist = (w + local(tq)) - (local(tk % w) + jnp.where(cur, w, 0))
    mask = (dist >= 0) & (dist <= LANES) & (cur | (j > 0))

    def blk(ref, hs):
        return ref[0, :, 0, :, hs].reshape(w, HEAD_DIM)

    group = N_HEADS if w <= LANES else N_HEADS // 2
    for h0 in range(0, N_HEADS, group):
        hss = [slice(h * HEAD_DIM, (h + 1) * HEAD_DIM) for h in range(h0, h0 + group)]
        vcats = [jnp.concatenate([blk(vp_ref, hs), blk(vc_ref, hs)], axis=0) for hs in hss]
        s = jnp.stack([lax.dot_general(blk(q_ref, hs),
                                       jnp.concatenate([blk(kp_ref, hs), blk(kc_ref, hs)], axis=0),
                                       _DN_T, preferred_element_type=F32) for hs in hss])
        s = jnp.where(mask[None], s, MASK_VALUE)
        m = jnp.max(s, axis=-1, keepdims=True)
        p = jnp.exp2(s - m)
        l = jnp.sum(p, axis=-1, keepdims=True)
        pn = (p / l).astype(BF16)
        o = jnp.stack([jnp.dot(pn[g], vcats[g], preferred_element_type=F32) for g in range(group)])
        lse = jnp.broadcast_to(m + jnp.log2(l), o.shape)
        if has_prev:
            lse_prev = jnp.stack([blk(li_ref, hs) for hs in hss])
            o_prev = jnp.stack([blk(oi_ref, hs) for hs in hss])
            mx = jnp.maximum(lse, lse_prev)
            e_new = jnp.exp2(lse - mx)
            e_old = jnp.exp2(lse_prev - mx)
            tot = e_new + e_old
            o = (e_new * o + e_old * o_prev) / tot
            lse = mx + jnp.log2(tot)
        for g, hs in enumerate(hss):
            o_ref[0, :, 0, :, hs] = o[g].astype(o_ref.dtype).reshape(planes, rows, HEAD_DIM)
            if not is_last:
                outs[1][0, :, 0, :, hs] = lse[g].reshape(planes, rows, HEAD_DIM)


def _band_stage(qk, v, prev, dil, is_last):
    bsz, _, sub, _ = v.shape
    planes = DIL_MAX // dil
    groups = DIL_MAX // planes
    rows = (LANES if planes < DIL_MAX else 2 * LANES) // planes
    view = lambda a: a.reshape(bsz, planes, groups, sub, a.shape[-1])
    nb = sub // rows
    blk = lambda col, prevblk: pl.BlockSpec(
        (1, planes, 1, rows, D_ATTN),
        (lambda b, g, j: (b, 0, g, jnp.maximum(j - 1, 0), col)) if prevblk else (lambda b, g, j: (b, 0, g, j, col)))
    in_specs = [blk(0, False), blk(1, False), blk(1, True), blk(0, False), blk(0, True)]
    args = [view(qk)] * 3 + [view(v)] * 2
    o_spec = blk(0, False)
    if prev is not None:
        in_specs += [o_spec, o_spec]
        args += [view(prev[0]), view(prev[1])]
    out_specs = [o_spec] if is_last else [o_spec, o_spec]
    out_shape = [jax.ShapeDtypeStruct((bsz, planes, groups, sub, D_ATTN), BF16 if is_last else F32)]
    if not is_last:
        out_shape.append(jax.ShapeDtypeStruct((bsz, planes, groups, sub, D_ATTN), F32))
    outs = pl.pallas_call(
        functools.partial(_band_body, has_prev=prev is not None, is_last=is_last, planes=planes, rows=rows),
        grid=(bsz, groups, nb),
        in_specs=in_specs, out_specs=out_specs, out_shape=out_shape,
        compiler_params=_cparams(("parallel", "parallel", "arbitrary")),
        name="dilated_band_d%d" % dil,
    )(*args)
    unview = lambda a: a.reshape(bsz, DIL_MAX, sub, a.shape[-1])
    if is_last:
        return unview(outs[0])
    return unview(outs[0]), unview(outs[1])


def _mixer_dilated(xb2, bsz, seq, w_in):
    d = xb2.shape[1]
    sub = seq // DIL_MAX
    assert seq % (DIL_MAX * 2 * LANES) == 0
    xp = xb2.reshape(bsz, sub, DIL_MAX, d).transpose(0, 2, 1, 3).reshape(bsz * seq, d)
    pos = (jnp.arange(sub)[None, :] * DIL_MAX + jnp.arange(DIL_MAX)[:, None]).reshape(-1)
    qk = _proj(xp, w_in[:, :2 * D_ATTN], pos, half=64, qcols=D_ATTN).reshape(bsz, DIL_MAX, sub, 2 * D_ATTN)
    v = _proj(xp, w_in[:, 2 * D_ATTN:], pos).reshape(bsz, DIL_MAX, sub, D_ATTN)
    prev = None
    for g, (window, dil) in enumerate(DILATED_GROUPS):
        assert window // dil == LANES and DIL_MAX % dil == 0
        prev = _band_stage(qk, v, prev, dil, g == len(DILATED_GROUPS) - 1)
    return prev.transpose(0, 2, 1, 3).reshape(bsz * seq, D_ATTN)


def _flash_body(*refs, kind, hp, tq):
    if kind == "fox":
        q_ref, k_ref, v_ref, cq_ref, ck_ref, o_ref = refs
    else:
        q_ref, k_ref, v_ref, o_ref, kmean_ref = refs
    hg = pl.program_id(1)
    i = pl.program_id(2)
    seq = k_ref.shape[1]
    nsl = tq // LANES
    rel = lax.broadcasted_iota(jnp.int32, (tq, LANES), 0) - lax.broadcasted_iota(jnp.int32, (tq, LANES), 1)
    causal = [rel >= c * LANES for c in range(nsl)]
    heads = []

    if kind == "moba":
        nblk = seq // MOBA_BLOCK
        spb = MOBA_BLOCK // LANES
        bpt = tq // MOBA_BLOCK

        @pl.when(i == 0)
        def _():
            kmean_ref[...] = jnp.zeros_like(kmean_ref)
            for hh in range(hp):
                hs = slice(hh * HEAD_DIM, (hh + 1) * HEAD_DIM)
                for n in range(nblk):
                    kb = k_ref[0, n * MOBA_BLOCK:(n + 1) * MOBA_BLOCK, hs].astype(F32)
                    kmean_ref[hh, n:n + 1, :] = jnp.sum(kb, axis=0, keepdims=True) / MOBA_BLOCK

        row_blk = lax.broadcasted_iota(jnp.int32, (tq, 1), 0) // MOBA_BLOCK
        own = (i * bpt + row_blk).astype(F32)
        blk_id = lax.broadcasted_iota(jnp.int32, (tq, LANES), 1).astype(F32)

    for hh in range(hp):
        hs = slice(hh * HEAD_DIM, (hh + 1) * HEAD_DIM)
        q = q_ref[0, :, hs]
        if kind == "fox":
            cq = jnp.broadcast_to(cq_ref[0, 0, :, hh:hh + 1], (tq, LANES))
            heads.append((hs, q, cq, hg * hp + hh))
        else:
            gate = lax.dot_general(q, kmean_ref[hh].astype(BF16), _DN_T, preferred_element_type=F32)
            gate = jnp.where(blk_id < own, gate, -jnp.inf)
            sel = jnp.zeros(gate.shape, F32)
            for _ in range(min(MOBA_TOPK, nblk)):
                mx = jnp.max(gate, axis=-1, keepdims=True)
                first = jnp.min(jnp.where(gate == mx, blk_id, float(LANES)), axis=-1, keepdims=True)
                pick = blk_id == first
                sel = jnp.where(pick & (first < own), 1.0, sel)
                gate = jnp.where(pick, -jnp.inf, gate)
            heads.append((hs, q, sel, None))

    def step(j, carry, diag):
        off = pl.multiple_of(j * tq, tq)
        out = []
        for (hs, q, aux, head), (m, l, acc) in zip(heads, carry):
            k = k_ref[0, pl.ds(off, tq), hs]
            v = v_ref[0, pl.ds(off, tq), hs]
            if kind == "fox":
                ck = ck_ref[0, pl.ds(head, 1), pl.ds(off, tq)]

                def col_fn(c, sc):
                    sc = sc + (aux - ck[:, c * LANES:(c + 1) * LANES])
                    return jnp.where(causal[c], sc, MASK_VALUE) if diag else sc
            else:
                rowsel = [jnp.max(jnp.where(blk_id == (j * bpt + n).astype(F32), aux, 0.0),
                                  axis=-1, keepdims=True) > 0.0 for n in range(bpt)]

                def col_fn(c, sc):
                    keep = rowsel[c // spb]
                    if diag:
                        keep = keep | ((row_blk == c // spb) & causal[c])
                    return jnp.where(keep, sc, MASK_VALUE)
            out.append(_attn_tile(q, k, v, m, l, acc, col_fn))
        return tuple(out)

    carry = tuple(_attn_init(tq) for _ in range(hp))
    carry = lax.fori_loop(0, i, functools.partial(step, diag=False), carry)
    carry = step(i, carry, True)
    for (hs, _, _, _), (m, l, acc) in zip(heads, carry):
        o_ref[0, :, hs] = _attn_finish(l, acc).astype(o_ref.dtype)


def _flash(kind, q_arr, q_off, k_arr, k_off, v_arr, v_off, extra, hp, tq):
    bsz, seq, _ = q_arr.shape
    wid = hp * HEAD_DIM
    hgs = N_HEADS // hp
    qo, ko, vo = q_off // hp, k_off // hp, v_off // hp
    in_specs = [pl.BlockSpec((1, tq, wid), lambda b, h, i: (b, i, qo + h)),
                pl.BlockSpec((1, seq, wid), lambda b, h, i: (b, 0, ko + h)),
                pl.BlockSpec((1, seq, wid), lambda b, h, i: (b, 0, vo + h))]
    args = [q_arr, k_arr, v_arr]
    scratch = []
    if kind == "fox":
        cum_col, cum_row = extra
        in_specs += [pl.BlockSpec((1, 1, tq, hp), lambda b, h, i: (b, h, i, 0)),
                     pl.BlockSpec((1, N_HEADS, seq), lambda b, h, i: (b, 0, 0))]
        args += [cum_col, cum_row]
    else:
        assert seq // MOBA_BLOCK <= LANES and tq % MOBA_BLOCK == 0
        scratch = [pltpu.VMEM((hp, LANES, HEAD_DIM), F32)]
    return pl.pallas_call(
        functools.partial(_flash_body, kind=kind, hp=hp, tq=tq),
        grid=(bsz, hgs, seq // tq),
        in_specs=in_specs,
        out_specs=pl.BlockSpec((1, tq, wid), lambda b, h, i: (b, i, h)),
        out_shape=jax.ShapeDtypeStruct((bsz, seq, D_ATTN), BF16),
        scratch_shapes=scratch,
        compiler_params=_cparams(("parallel", "parallel", "arbitrary")),
        name="flash_" + kind,
    )(*args)


def _cumsum_body(f_ref, b_ref, o_ref, carry_ref):
    j = pl.program_id(1)

    @pl.when(j == 0)
    def _():
        carry_ref[...] = jnp.zeros_like(carry_ref)

    ts = f_ref.shape[1]
    logf = jax.nn.log_sigmoid(f_ref[0] + b_ref[...])
    tri = (lax.broadcasted_iota(jnp.int32, (ts, ts), 0) >= lax.broadcasted_iota(jnp.int32, (ts, ts), 1)).astype(F32)
    cum = jnp.dot(tri, logf, precision=lax.Precision.HIGHEST, preferred_element_type=F32) + carry_ref[...]
    o_ref[0] = cum * LOG2E
    carry_ref[...] = cum[ts - 1:ts, :]


def _forget_cumsum(f_raw, b_pad, ts=256):
    bsz, seq, _ = f_raw.shape
    return pl.pallas_call(
        _cumsum_body,
        grid=(bsz, seq // ts),
        in_specs=[pl.BlockSpec((1, ts, LANES), lambda b, j: (b, j, 0)),
                  pl.BlockSpec((1, LANES), lambda b, j: (0, 0))],
        out_specs=pl.BlockSpec((1, ts, LANES), lambda b, j: (b, j, 0)),
        out_shape=jax.ShapeDtypeStruct((bsz, seq, LANES), F32),
        scratch_shapes=[pltpu.VMEM((1, LANES), F32)],
        compiler_params=_cparams(("parallel", "arbitrary")),
        name="forget_cumsum",
    )(f_raw, b_pad)


def _pad_cols(w, n):
    return jnp.pad(w, ((0, 0), (0, n - w.shape[1])))


def _mixer_fox(xb2, bsz, seq, w_in, b_forget, hp=2, tq=512):
    tq = min(tq, seq)
    pos = jnp.arange(seq)
    qkv = _proj(xb2, w_in[:, :3 * D_ATTN], pos, qcols=D_ATTN).reshape(bsz, seq, 3 * D_ATTN)
    f_raw = _proj(xb2, _pad_cols(w_in[:, 3 * D_ATTN:], LANES), pos, out_dtype=F32).reshape(bsz, seq, LANES)
    cum = _forget_cumsum(f_raw, _pad_cols(b_forget.reshape(1, N_HEADS), LANES))[..., :N_HEADS]
    cum_row = cum.transpose(0, 2, 1)
    cum_col = cum.reshape(bsz, seq, N_HEADS // hp, hp).transpose(0, 2, 1, 3)
    o = _flash("fox", qkv, 0, qkv, N_HEADS, qkv, 2 * N_HEADS, (cum_col, cum_row), hp, tq)
    return o.reshape(bsz * seq, D_ATTN)


def _mixer_moba(xb2, bsz, seq, w_in, hp=2, tq=512):
    tq = min(tq, seq)
    pos = jnp.arange(seq)
    qk = _proj(xb2, w_in[:, :2 * D_ATTN], pos, half=64, qcols=D_ATTN).reshape(bsz, seq, 2 * D_ATTN)
    v = _proj(xb2, w_in[:, 2 * D_ATTN:], pos).reshape(bsz, seq, D_ATTN)
    assert seq % MOBA_BLOCK == 0
    o = _flash("moba", qk, 0, qk, N_HEADS, v, 0, None, hp, tq)
    return o.reshape(bsz * seq, D_ATTN)


def _kiprep_body(r_ref, g_ref, b_ref, cos_ref, sin_ref, a_ref, b2_ref):
    x = r_ref[...]
    lane = lax.broadcasted_iota(jnp.int32, x.shape, 1)
    inside = lane < IDX_DIM
    mu = jnp.sum(jnp.where(inside, x, 0.0), axis=-1, keepdims=True) / IDX_DIM
    xc = jnp.where(inside, x - mu, 0.0)
    var = jnp.sum(xc * xc, axis=-1, keepdims=True) / IDX_DIM
    y = xc * lax.rsqrt(var + LN_EPS) * g_ref[...] + b_ref[...]
    y = y * cos_ref[...] + _rotate_half(y, IDX_DIM // 2) * sin_ref[...]
    y = jnp.where(inside, y, 0.0)
    a_ref[...] = y.astype(BF16)
    b2_ref[...] = pltpu.roll(y, IDX_DIM, 1).astype(BF16)


def _ki_prep(raw2, g, b, seq, tm=512):
    m = raw2.shape[0]
    tm = min(tm, seq)
    cos, sin = _rope_tables(jnp.arange(seq), IDX_DIM // 2)
    nsb = seq // tm
    row = lambda i: (i, 0)
    const = lambda i: (0, 0)
    return pl.pallas_call(
        _kiprep_body,
        grid=(m // tm,),
        in_specs=[pl.BlockSpec((tm, LANES), row), pl.BlockSpec((1, LANES), const), pl.BlockSpec((1, LANES), const),
                  pl.BlockSpec((tm, LANES), lambda i: (i % nsb, 0)), pl.BlockSpec((tm, LANES), lambda i: (i % nsb, 0))],
        out_specs=[pl.BlockSpec((tm, LANES), row)] * 2,
        out_shape=[jax.ShapeDtypeStruct((m, LANES), BF16)] * 2,
        compiler_params=_cparams(("parallel",)),
        name="dsa_ki_prep",
    )(raw2, _pad_cols(g.reshape(1, IDX_DIM), LANES), _pad_cols(b.reshape(1, IDX_DIM), LANES), cos, sin)


def _dsa_body(qi_ref, wr_ref, kia_ref, kib_ref, q_ref, k_ref, v_ref, o_ref, key_ref, bias_ref,
              *, tq, ch, topk, wscale):
    i = pl.program_id(1)
    seq = k_ref.shape[1]
    nch = (i * tq + tq + ch - 1) // ch
    nsl = ch // LANES
    wi = wr_ref[0][:, IDX_DIM:IDX_DIM + IDX_HEADS] * wscale
    rowpos = i * tq + lax.broadcasted_iota(jnp.int32, (tq, 1), 0)
    lane_ch = lax.broadcasted_iota(jnp.int32, (tq, ch), 1)
    lane_1 = lax.broadcasted_iota(jnp.int32, (tq, LANES), 1)

    def score_chunk(c, _):
        off = pl.multiple_of(c * ch, ch)
        ka = kia_ref[0, pl.ds(off, ch), :]
        kb = kib_ref[0, pl.ds(off, ch), :]
        sc = jnp.zeros((tq, ch), F32)
        for hpair in range(IDX_HEADS // 2):
            qp = qi_ref[0, :, hpair * LANES:(hpair + 1) * LANES]
            for t, kk in enumerate((ka, kb)):
                h = 2 * hpair + t
                lg = lax.dot_general(qp, kk, _DN_T, preferred_element_type=F32)
                sc = sc + wi[:, h:h + 1] * jnp.maximum(lg, 0.0)
        bits = pltpu.bitcast(sc, jnp.int32)
        bits = jnp.where(bits == INT_MIN, 0, bits)
        key = bits ^ ((bits >> 31) & 0x7FFFFFFF)
        key = jnp.where(off + lane_ch <= rowpos, key, NEG_INF_KEY)
        key_ref[:, pl.ds(off, ch)] = key
        return 0

    lax.fori_loop(0, nch, score_chunk, 0)

    def count(pred):
        def cb(c, acc):
            off = pl.multiple_of(c * ch, ch)
            blk = key_ref[:, pl.ds(off, ch)]
            for s_ in range(nsl):
                acc = acc + jnp.where(pred(blk[:, s_ * LANES:(s_ + 1) * LANES], off + s_ * LANES + lane_1), 1, 0)
            return acc
        acc = lax.fori_loop(0, nch, cb, jnp.zeros((tq, LANES), jnp.int32))
        return jnp.sum(acc.astype(F32), axis=-1, keepdims=True).astype(jnp.int32)

    def bisect_val(t, lo):
        cand = lo + (jnp.int32(1) << (31 - t))
        return jnp.where(count(lambda kv, idx: kv >= cand) >= topk, cand, lo)

    thr = lax.fori_loop(0, 32, bisect_val, jnp.full((tq, LANES), INT_MIN, jnp.int32))
    need = topk - count(lambda kv, idx: kv > thr)
    n_eq = count(lambda kv, idx: kv == thr)
    excess = (n_eq > need) & (thr[:, 0:1] > NEG_INF_KEY)
    any_excess = jnp.max(jnp.where(excess, 1.0, 0.0)) > 0.0

    def tie_path():
        def bisect_idx(t, lo):
            cand = lo + (jnp.int32(1) << (12 - t))
            return jnp.where(count(lambda kv, idx: (kv == thr) & (idx < cand)) < need, cand, lo)
        return lax.fori_loop(0, 13, bisect_idx, jnp.zeros((tq, LANES), jnp.int32))

    jmax = lax.cond(any_excess, tie_path, lambda: jnp.full((tq, LANES), seq, jnp.int32))

    def bias_chunk(c, _):
        off = pl.multiple_of(c * ch, ch)
        blk = key_ref[:, pl.ds(off, ch)]
        for s_ in range(nsl):
            kv = blk[:, s_ * LANES:(s_ + 1) * LANES]
            idx = off + s_ * LANES + lane_1
            keep = ((kv > thr) | ((kv == thr) & (idx <= jmax))) & (idx <= rowpos)
            bias_ref[:, pl.ds(pl.multiple_of(off + s_ * LANES, LANES), LANES)] = jnp.where(keep, 0.0, MASK_VALUE)
        return 0

    lax.fori_loop(0, nch, bias_chunk, 0)

    rows = GQA_GROUP * tq
    for g in range(B_KV_HEADS):
        gs = slice(g * HEAD_DIM, (g + 1) * HEAD_DIM)
        qg = jnp.concatenate(
            [q_ref[0, :, (g * GQA_GROUP + r) * HEAD_DIM:(g * GQA_GROUP + r + 1) * HEAD_DIM]
             for r in range(GQA_GROUP)], axis=0)

        def att(c, carry):
            off = pl.multiple_of(c * ch, ch)
            bias = bias_ref[:, pl.ds(off, ch)]

            def col_fn(s_, sc):
                return sc + jnp.concatenate([bias[:, s_ * LANES:(s_ + 1) * LANES]] * GQA_GROUP, axis=0)

            return _attn_tile(qg, k_ref[0, pl.ds(off, ch), gs], v_ref[0, pl.ds(off, ch), gs], *carry, col_fn)

        m, l, acc = lax.fori_loop(0, nch, att, _attn_init(rows))
        o = _attn_finish(l, acc)
        for r in range(GQA_GROUP):
            h = g * GQA_GROUP + r
            o_ref[0, :, h * HEAD_DIM:(h + 1) * HEAD_DIM] = o[r * tq:(r + 1) * tq].astype(o_ref.dtype)


def _mixer_dsa(xb2, bsz, seq, w_in, idx_g, idx_b, tq=256, ch=512):
    ch = min(ch, seq)
    pos = jnp.arange(seq)
    nqk = D_ATTN + KV_WIDTH
    qk = _proj(xb2, w_in[:, :nqk], pos, half=64, qcols=D_ATTN).reshape(bsz, seq, nqk)
    v = _proj(xb2, w_in[:, nqk:nqk + KV_WIDTH], pos).reshape(bsz, seq, KV_WIDTH)
    o_qi = nqk + KV_WIDTH
    n_qi = IDX_HEADS * IDX_DIM
    qi = _proj(xb2, w_in[:, o_qi:o_qi + n_qi], pos, half=IDX_DIM // 2).reshape(bsz, seq, n_qi)
    raw2 = _proj(xb2, _pad_cols(w_in[:, o_qi + n_qi:], LANES), pos, out_dtype=F32)
    kia, kib = _ki_prep(raw2, idx_g, idx_b, seq)
    topk = min(IDX_TOPK_MAX, seq // 4)
    body = functools.partial(_dsa_body, tq=tq, ch=ch, topk=topk, wscale=IDX_HEADS ** -0.5 * IDX_DIM ** -0.5)
    res = lambda b, i: (b, 0, 0)
    o = pl.pallas_call(
        body,
        grid=(bsz, seq // tq),
        in_specs=[pl.BlockSpec((1, tq, n_qi), lambda b, i: (b, i, 0)),
                  pl.BlockSpec((1, tq, LANES), lambda b, i: (b, i, 0)),
                  pl.BlockSpec((1, seq, LANES), res), pl.BlockSpec((1, seq, LANES), res),
                  pl.BlockSpec((1, tq, D_ATTN), lambda b, i: (b, i, 0)),
                  pl.BlockSpec((1, seq, KV_WIDTH), lambda b, i: (b, 0, D_ATTN // KV_WIDTH)),
                  pl.BlockSpec((1, seq, KV_WIDTH), res)],
        out_specs=pl.BlockSpec((1, tq, D_ATTN), lambda b, i: (b, i, 0)),
        out_shape=jax.ShapeDtypeStruct((bsz, seq, D_ATTN), BF16),
        scratch_shapes=[pltpu.VMEM((tq, seq), jnp.int32), pltpu.VMEM((tq, seq), F32)],
        compiler_params=_cparams(("parallel", "arbitrary")),
        name="dsa_select_attend",
    )(qi, raw2.reshape(bsz, seq, LANES), kia.reshape(bsz, seq, LANES), kib.reshape(bsz, seq, LANES), qk, qk, v)
    return o.reshape(bsz * seq, D_ATTN)


def _trunk(x, layers):
    bsz, seq, d = x.shape
    x2 = x.reshape(bsz * seq, d)
    xb2 = x2
    for kind, p in enumerate(layers):
        w_in = p["w_in"].astype(BF16)
        if kind == 0:
            a = _mixer_dilated(xb2, bsz, seq, w_in)
        elif kind == 1:
            a = _mixer_dsa(xb2, bsz, seq, w_in, p["idx_norm_g"], p["idx_norm_b"])
        elif kind == 2:
            a = _mixer_fox(xb2, bsz, seq, w_in, p["b_forget"])
        else:
            a = _mixer_moba(xb2, bsz, seq, w_in)
        x1, gates, top_exp = _outln(a, p["w_out"].astype(BF16), x2, p["ln1_g"], p["ln1_b"],
                                    p["router_w"].astype(BF16), p["router_b"])
        x2, xb2 = _moe_layer(x1, gates[:, :TOP_K], top_exp[:, :TOP_K], p["w_gu"], p["b_gu"], p["w_dn"], p["b_dn"],
                             p["ln2_g"], p["ln2_b"])
    return x2.reshape(bsz, seq, d)


def kernel(x, l0_w_in, l0_w_out, l0_ln1_g, l0_ln1_b, l0_router_w, l0_router_b, l0_w_gu, l0_b_gu, l0_w_dn, l0_b_dn, l0_ln2_g, l0_ln2_b, l1_w_in, l1_idx_norm_g, l1_idx_norm_b, l1_w_out, l1_ln1_g, l1_ln1_b, l1_router_w, l1_router_b, l1_w_gu, l1_b_gu, l1_w_dn, l1_b_dn, l1_ln2_g, l1_ln2_b, l2_w_in, l2_b_forget, l2_w_out, l2_ln1_g, l2_ln1_b, l2_router_w, l2_router_b, l2_w_gu, l2_b_gu, l2_w_dn, l2_b_dn, l2_ln2_g, l2_ln2_b, l3_w_in, l3_w_out, l3_ln1_g, l3_ln1_b, l3_router_w, l3_router_b, l3_w_gu, l3_b_gu, l3_w_dn, l3_b_dn, l3_ln2_g, l3_ln2_b):
    names = ("w_out", "ln1_g", "ln1_b", "router_w", "router_b", "w_gu", "b_gu", "w_dn", "b_dn", "ln2_g", "ln2_b")
    l0 = dict(zip(("w_in",) + names, (l0_w_in, l0_w_out, l0_ln1_g, l0_ln1_b, l0_router_w, l0_router_b,
                                      l0_w_gu, l0_b_gu, l0_w_dn, l0_b_dn, l0_ln2_g, l0_ln2_b)))
    l1 = dict(zip(("w_in", "idx_norm_g", "idx_norm_b") + names,
                  (l1_w_in, l1_idx_norm_g, l1_idx_norm_b, l1_w_out, l1_ln1_g, l1_ln1_b, l1_router_w, l1_router_b,
                   l1_w_gu, l1_b_gu, l1_w_dn, l1_b_dn, l1_ln2_g, l1_ln2_b)))
    l2 = dict(zip(("w_in", "b_forget") + names,
                  (l2_w_in, l2_b_forget, l2_w_out, l2_ln1_g, l2_ln1_b, l2_router_w, l2_router_b,
                   l2_w_gu, l2_b_gu, l2_w_dn, l2_b_dn, l2_ln2_g, l2_ln2_b)))
    l3 = dict(zip(("w_in",) + names, (l3_w_in, l3_w_out, l3_ln1_g, l3_ln1_b, l3_router_w, l3_router_b,
                                      l3_w_gu, l3_b_gu, l3_w_dn, l3_b_dn, l3_ln2_g, l3_ln2_b)))
    return _trunk(x, (l0, l1, l2, l3))
```

```python
import functools

import jax
import jax.numpy as jnp
import numpy as np
from jax import lax
from jax.experimental import pallas as pl
from jax.experimental.pallas import tpu as pltpu

N_HEADS = 16
HEAD_DIM = 128
D_MODEL = 2048
D_ATTN = N_HEADS * HEAD_DIM
ROPE_THETA = 10000.0
MASK_VALUE = -1e30
LN_EPS = 1e-5
DILATED_GROUPS = ((128, 1), (512, 4), (2048, 16))
DIL_MAX = 16
B_KV_HEADS = 4
GQA_GROUP = N_HEADS // B_KV_HEADS
KV_WIDTH = B_KV_HEADS * HEAD_DIM
IDX_HEADS = 16
IDX_DIM = 64
IDX_TOPK_MAX = 256
MOBA_BLOCK = 256
MOBA_TOPK = 3
N_EXPERTS = 32
TOP_K = 4
D_EXPERT = 1024
SWIGLU_ALPHA = 1.702
SWIGLU_LIMIT = 7.0
DEPTH = 4
DEEPNORM_ALPHA = (2 * DEPTH) ** 0.25

LANES = 128
VMEM_LIMIT = 56 * 1024 * 1024
MOE_TM = 256
INT_MIN = -(2 ** 31)
NEG_INF_KEY = -2139095041
LOG2E = 1.4426950408889634
QSCALE = HEAD_DIM ** -0.5 * LOG2E

BF16 = jnp.bfloat16
F32 = jnp.float32
_DN_T = (((1,), (1,)), ((), ()))


def _pack_bf16_pairs(v):
    n = v.shape[1] // 2
    lo = pltpu.bitcast(v[:, :n].astype(BF16).astype(F32), jnp.int32)
    hi = pltpu.bitcast(v[:, n:].astype(BF16).astype(F32), jnp.int32)
    return lax.shift_right_logical(lo, jnp.int32(16)) | (hi & jnp.int32(-65536))


def _unpack_bf16_pairs(pk):
    lo = pltpu.bitcast(lax.shift_left(pk, jnp.int32(16)), F32)
    hi = pltpu.bitcast(pk & jnp.int32(-65536), F32)
    return jnp.concatenate([lo, hi], axis=1)


def _cparams(sem):
    return pltpu.CompilerParams(dimension_semantics=sem, vmem_limit_bytes=VMEM_LIMIT)


def _rope_tables(pos, half):
    inv_freq = ROPE_THETA ** (-jnp.arange(half, dtype=F32) / half)
    ang = pos.astype(F32)[:, None] * inv_freq[None, :]
    cos = jnp.tile(jnp.cos(ang), (1, LANES // half))
    sin = jnp.tile(jnp.concatenate([-jnp.sin(ang), jnp.sin(ang)], -1), (1, LANES // (2 * half)))
    return cos, sin


def _rotate_half(t, half):
    if 2 * half == LANES:
        return pltpu.roll(t, half, 1)
    lane = lax.broadcasted_iota(jnp.int32, t.shape, 1)
    first = (lane % (2 * half)) < half
    return jnp.where(first, pltpu.roll(t, LANES - half, 1), pltpu.roll(t, half, 1))


def _proj_body(*refs, half, q_tiles):
    if half:
        x_ref, w_ref, cos_ref, sin_ref, o_ref = refs
    else:
        x_ref, w_ref, o_ref = refs
    acc = jnp.dot(x_ref[...].astype(BF16), w_ref[...], preferred_element_type=F32)
    if q_tiles:
        acc = acc * jnp.where(pl.program_id(1) < q_tiles, QSCALE, 1.0)
    if not half:
        o_ref[...] = acc.astype(o_ref.dtype)
        return
    cos = cos_ref[...]
    sin = sin_ref[...]
    for c in range(acc.shape[1] // LANES):
        t = acc[:, c * LANES:(c + 1) * LANES]
        o_ref[:, c * LANES:(c + 1) * LANES] = (t * cos + _rotate_half(t, half) * sin).astype(o_ref.dtype)


def _proj(x2, w, pos, half=0, out_dtype=BF16, tm=512, qcols=0):
    m, k = x2.shape
    n = w.shape[1]
    seq = pos.shape[0]
    tm = min(tm, seq)
    tn = 512 if n % 512 == 0 else (256 if n % 256 == 0 else LANES)
    assert qcols % tn == 0 and seq % tm == 0
    in_specs = [pl.BlockSpec((tm, k), lambda i, j: (i, 0)),
                pl.BlockSpec((k, tn), lambda i, j: (0, j))]
    args = [x2, w]
    if half:
        cos, sin = _rope_tables(pos, half)
        nsb = seq // tm
        in_specs += [pl.BlockSpec((tm, LANES), lambda i, j: (i % nsb, 0))] * 2
        args += [cos, sin]
    return pl.pallas_call(
        functools.partial(_proj_body, half=half, q_tiles=qcols // tn),
        grid=(m // tm, n // tn),
        in_specs=in_specs,
        out_specs=pl.BlockSpec((tm, tn), lambda i, j: (i, j)),
        out_shape=jax.ShapeDtypeStruct((m, n), out_dtype),
        compiler_params=_cparams(("parallel", "arbitrary")),
        name="proj_rope%d" % half,
    )(*args)


def _layer_norm_rows(z, g, b):
    mu = jnp.mean(z, axis=-1, keepdims=True)
    zc = z - mu
    var = jnp.mean(zc * zc, axis=-1, keepdims=True)
    return zc * lax.rsqrt(var + LN_EPS) * g + b


def _outln_body(a_ref, w_ref, x_ref, g_ref, b_ref, rw_ref, rb_ref, xo_ref, xp_ref, gt_ref, te_ref):
    h = jnp.dot(a_ref[...], w_ref[...], preferred_element_type=F32)
    y = _layer_norm_rows(DEEPNORM_ALPHA * x_ref[...] + h, g_ref[...], b_ref[...])
    xo_ref[...] = y
    xp_ref[...] = _pack_bf16_pairs(y)
    work = jnp.dot(y.astype(BF16), rw_ref[...], preferred_element_type=F32) + rb_ref[...]
    lane = lax.broadcasted_iota(jnp.int32, work.shape, 1).astype(F32)
    vals, idxs = [], []
    for _ in range(TOP_K):
        mx = jnp.max(work, axis=-1, keepdims=True)
        first = jnp.min(jnp.where(work == mx, lane, float(LANES)), axis=-1, keepdims=True)
        vals.append(mx)
        idxs.append(first)
        work = jnp.where(lane == first, -jnp.inf, work)
    es = [jnp.exp(v - vals[0]) for v in vals]
    denom = functools.reduce(jnp.add, es)
    gates = jnp.zeros(work.shape, F32)
    experts = jnp.zeros(work.shape, F32)
    for k in range(TOP_K):
        gates = jnp.where(lane == k, es[k] / denom, gates)
        experts = jnp.where(lane == k, idxs[k], experts)
    gt_ref[...] = gates
    te_ref[...] = experts.astype(jnp.int32)


def _outln(a, w_out, x2, g, b, rw, rb, tm=256):
    m, d = x2.shape
    row = lambda i: (i, 0)
    const = lambda i: (0, 0)
    rw_pad = _pad_cols(rw, LANES)
    rb_pad = jnp.pad(rb.reshape(1, N_EXPERTS), ((0, 0), (0, LANES - N_EXPERTS)), constant_values=-jnp.inf)
    return pl.pallas_call(
        _outln_body,
        grid=(m // tm,),
        in_specs=[pl.BlockSpec((tm, d), row), pl.BlockSpec((d, d), const), pl.BlockSpec((tm, d), row),
                  pl.BlockSpec((1, d), const), pl.BlockSpec((1, d), const),
                  pl.BlockSpec((d, LANES), const), pl.BlockSpec((1, LANES), const)],
        out_specs=[pl.BlockSpec((tm, d), row), pl.BlockSpec((tm, d // 2), row),
                   pl.BlockSpec((tm, LANES), row), pl.BlockSpec((tm, LANES), row)],
        out_shape=[jax.ShapeDtypeStruct((m, d), F32), jax.ShapeDtypeStruct((m, d // 2), jnp.int32),
                   jax.ShapeDtypeStruct((m, LANES), F32), jax.ShapeDtypeStruct((m, LANES), jnp.int32)],
        compiler_params=_cparams(("parallel",)),
        name="outproj_ln_router",
    )(a, w_out, x2, g.reshape(1, d), b.reshape(1, d), rw_pad, rb_pad)


def _moe_body(be_ref, nb_ref, nx_ref, x_ref, wgu_hbm, bgu_ref, wdn_hbm, bdn_ref, o_ref,
              wgu_f, wdn_f, wgu_b, wdn_b, sem):
    i = pl.program_id(0)
    used = i < nb_ref[0]
    expert = be_ref[i]
    new_expert = (i == 0) | (expert != be_ref[jnp.maximum(i - 1, 0)])

    def weight_copies(e):
        return (pltpu.make_async_copy(wgu_hbm.at[e], wgu_f, sem.at[0]),
                pltpu.make_async_copy(wdn_hbm.at[e], wdn_f, sem.at[1]))

    @pl.when(used & (i == 0))
    def _():
        for c in weight_copies(expert):
            c.start()

    @pl.when(used & new_expert)
    def _():
        for c in weight_copies(expert):
            c.wait()
        rows = 64

        def cast(src, dst):
            def body(r, _):
                sl = pl.ds(pl.multiple_of(r * rows, rows), rows)
                dst[sl, :] = src[sl, :].astype(BF16)
                return 0
            lax.fori_loop(0, src.shape[0] // rows, body, 0)

        cast(wgu_f, wgu_b)
        cast(wdn_f, wdn_b)

        @pl.when(nx_ref[i] >= 0)
        def _():
            for c in weight_copies(nx_ref[i]):
                c.start()

    @pl.when(used)
    def _():
        h = jnp.dot(_unpack_bf16_pairs(x_ref[...]).astype(BF16), wgu_b[...], preferred_element_type=F32) + bgu_ref[0]
        glu = jnp.minimum(h[:, :D_EXPERT], SWIGLU_LIMIT)
        lin = jnp.clip(h[:, D_EXPERT:], -SWIGLU_LIMIT, SWIGLU_LIMIT)
        act = glu * jax.nn.sigmoid(SWIGLU_ALPHA * glu) * (lin + 1.0)
        y = jnp.dot(act.astype(BF16), wdn_b[...], preferred_element_type=F32) + bdn_ref[0]
        o_ref[...] = _pack_bf16_pairs(y)

    @pl.when(jnp.logical_not(used))
    def _():
        o_ref[...] = jnp.zeros_like(o_ref)


def _moe_experts(x_rows, block_exp, n_used, next_exp, w_gu, b_gu, w_dn, b_dn):
    n_rows = x_rows.shape[0]
    d = 2 * x_rows.shape[1]
    n_blocks = n_rows // MOE_TM
    grid_spec = pltpu.PrefetchScalarGridSpec(
        num_scalar_prefetch=3,
        grid=(n_blocks,),
        in_specs=[pl.BlockSpec((MOE_TM, d // 2), lambda i, be, nb, nx: (i, 0)),
                  pl.BlockSpec(memory_space=pl.ANY),
                  pl.BlockSpec((1, 1, 2 * D_EXPERT), lambda i, be, nb, nx: (be[i], 0, 0)),
                  pl.BlockSpec(memory_space=pl.ANY),
                  pl.BlockSpec((1, 1, d), lambda i, be, nb, nx: (be[i], 0, 0))],
        out_specs=pl.BlockSpec((MOE_TM, d // 2), lambda i, be, nb, nx: (i, 0)),
        scratch_shapes=[pltpu.VMEM((d, 2 * D_EXPERT), F32), pltpu.VMEM((D_EXPERT, d), F32),
                        pltpu.VMEM((d, 2 * D_EXPERT), BF16), pltpu.VMEM((D_EXPERT, d), BF16),
                        pltpu.SemaphoreType.DMA((2,))],
    )
    return pl.pallas_call(
        _moe_body,
        grid_spec=grid_spec,
        out_shape=jax.ShapeDtypeStruct((n_rows, d // 2), jnp.int32),
        compiler_params=_cparams(("arbitrary",)),
        name="moe_experts",
    )(block_exp, n_used, next_exp, x_rows, w_gu, b_gu.reshape(N_EXPERTS, 1, -1), w_dn,
      b_dn.reshape(N_EXPERTS, 1, -1))


def _combine_body(*refs):
    y_refs = refs[:TOP_K]
    gt_ref, x_ref, g_ref, b_ref, xo_ref, xb_ref = refs[TOP_K:]
    gt = gt_ref[...]
    y = gt[:, 0:1] * _unpack_bf16_pairs(y_refs[0][...])
    for k in range(1, TOP_K):
        y = y + gt[:, k:k + 1] * _unpack_bf16_pairs(y_refs[k][...])
    out = _layer_norm_rows(DEEPNORM_ALPHA * x_ref[...] + y, g_ref[...], b_ref[...])
    xo_ref[...] = out
    xb_ref[...] = out.astype(BF16)


def _combine_ln(y4, gates, x2, g, b, tm=256):
    m, d = x2.shape
    nb = m // tm
    row = lambda i: (i, 0)
    const = lambda i: (0, 0)
    y_specs = [pl.BlockSpec((tm, d // 2), functools.partial(lambda i, k: (k * nb + i, 0), k=k)) for k in range(TOP_K)]
    return pl.pallas_call(
        _combine_body,
        grid=(nb,),
        in_specs=y_specs + [pl.BlockSpec((tm, TOP_K), row),
                            pl.BlockSpec((tm, d), row), pl.BlockSpec((1, d), const), pl.BlockSpec((1, d), const)],
        out_specs=[pl.BlockSpec((tm, d), row), pl.BlockSpec((tm, d), row)],
        out_shape=[jax.ShapeDtypeStruct((m, d), F32), jax.ShapeDtypeStruct((m, d), BF16)],
        compiler_params=_cparams(("parallel",)),
        name="moe_combine_ln",
    )(*([y4] * TOP_K), gates, x2, g.reshape(1, d), b.reshape(1, d))


def _moe_layer(x1, x1p, gates, top_exp, w_gu, b_gu, w_dn, b_dn, g, b):
    n_tok, d = x1.shape
    n_assign = n_tok * TOP_K
    flat_exp = top_exp.reshape(-1)
    experts = jnp.arange(N_EXPERTS, dtype=jnp.int32)
    counts = jnp.sum((flat_exp[:, None] == experts[None, :]).astype(jnp.int32), axis=0)
    padded = (counts + MOE_TM - 1) // MOE_TM * MOE_TM
    pad_end = jnp.cumsum(padded)
    n_blocks = n_assign // MOE_TM + N_EXPERTS
    n_rows = n_blocks * MOE_TM
    block_start = jnp.arange(n_blocks, dtype=jnp.int32) * MOE_TM
    block_exp = jnp.minimum(jnp.sum((pad_end[None, :] <= block_start[:, None]).astype(jnp.int32), axis=1),
                            N_EXPERTS - 1).astype(jnp.int32)
    n_used = (pad_end[-1] // MOE_TM).astype(jnp.int32).reshape(1)
    later = (experts[None, :] > experts[:, None]) & (counts[None, :] > 0)
    nxt = jnp.min(jnp.where(later, experts[None, :], N_EXPERTS), axis=1)
    nxt = jnp.where(nxt == N_EXPERTS, -1, nxt)
    next_exp = jnp.sum(jnp.where(block_exp[:, None] == experts[None, :], nxt[None, :], 0), axis=1).astype(jnp.int32)
    idx_bits = (n_assign + N_EXPERTS * MOE_TM - 1).bit_length()
    a_idx = jnp.arange(n_assign, dtype=jnp.int32)
    f_idx = n_assign + jnp.arange(N_EXPERTS * MOE_TM, dtype=jnp.int32)
    f_num = jnp.arange(MOE_TM, dtype=jnp.int32)[None, :]
    f_major = jnp.where(f_num < (padded - counts)[:, None], 2 * experts[:, None] + 1, 2 * N_EXPERTS).reshape(-1)
    keys = jnp.concatenate([(2 * flat_exp << idx_bits) | a_idx, (f_major << idx_bits) | f_idx])
    src = jnp.sort(keys) & ((1 << idx_bits) - 1)
    row_tok = jnp.where(src < n_assign, src // TOP_K, 0)[:n_rows]
    _, inv = lax.sort_key_val(src, jnp.arange(src.shape[0], dtype=jnp.int32))
    pos = inv[:n_assign]
    x_rows = x1p[row_tok]
    y_rows = _moe_experts(x_rows, block_exp, n_used, next_exp, w_gu, b_gu, w_dn, b_dn)
    y4 = y_rows[pos.reshape(n_tok, TOP_K).T.reshape(-1)]
    return _combine_ln(y4, gates, x1, g, b)


def _attn_tile(q, k, v, m, l, acc, col_fn):
    s = lax.dot_general(q, k, _DN_T, preferred_element_type=F32)
    cols = [col_fn(c, s[:, c * LANES:(c + 1) * LANES]) for c in range(s.shape[1] // LANES)]
    m_new = jnp.maximum(m, jnp.max(functools.reduce(jnp.maximum, cols), axis=-1, keepdims=True))
    alpha = jnp.exp2(m - m_new)
    ps = [jnp.exp2(c - m_new) for c in cols]
    l = alpha * l + functools.reduce(jnp.add, ps)
    p = jnp.concatenate([x.astype(BF16) for x in ps], axis=1)
    acc = alpha * acc + jnp.dot(p, v, preferred_element_type=F32)
    return m_new, l, acc


def _attn_init(rows):
    return (jnp.full((rows, LANES), -jnp.inf, F32), jnp.zeros((rows, LANES), F32),
            jnp.zeros((rows, HEAD_DIM), F32))


def _attn_finish(l, acc):
    return acc / jnp.sum(l, axis=-1, keepdims=True)


def _band_body(*refs, has_prev, is_last, planes, rows):
    if has_prev:
        q_ref, kc_ref, kp_ref, vc_ref, vp_ref, oi_ref, li_ref = refs[:7]
        outs = refs[7:]
    else:
        q_ref, kc_ref, kp_ref, vc_ref, vp_ref = refs[:5]
        outs = refs[5:]
    o_ref = outs[0]
    j = pl.program_id(2)
    w = planes * rows

    def local(t):
        return (t % rows) * planes + t // rows

    tq = lax.broadcasted_iota(jnp.int32, (w, 2 * w), 0)
    tk = lax.broadcasted_iota(jnp.int32, (w, 2 * w), 1)
    cur = tk >= w
    dist = (w + local(tq)) - (local(tk % w) + jnp.where(cur, w, 0))
    mask = (dist >= 0) & (dist <= LANES) & (cur | (j > 0))

    def blk(ref, hs):
        return ref[0, :, 0, :, hs].reshape(w, HEAD_DIM)

    group = N_HEADS if w <= LANES else N_HEADS // 2
    for h0 in range(0, N_HEADS, group):
        hss = [slice(h * HEAD_DIM, (h + 1) * HEAD_DIM) for h in range(h0, h0 + group)]
        vcats = [jnp.concatenate([blk(vp_ref, hs), blk(vc_ref, hs)], axis=0) for hs in hss]
        s = jnp.stack([lax.dot_general(blk(q_ref, hs),
                                       jnp.concatenate([blk(kp_ref, hs), blk(kc_ref, hs)], axis=0),
                                       _DN_T, preferred_element_type=F32) for hs in hss])
        s = jnp.where(mask[None], s, MASK_VALUE)
        m = jnp.max(s, axis=-1, keepdims=True)
        p = jnp.exp2(s - m)
        l = jnp.sum(p, axis=-1, keepdims=True)
        pn = (p / l).astype(BF16)
        o = jnp.stack([jnp.dot(pn[g], vcats[g], preferred_element_type=F32) for g in range(group)])
        lse = jnp.broadcast_to(m + jnp.log2(l), o.shape)
        if has_prev:
            lse_prev = jnp.stack([blk(li_ref, hs) for hs in hss])
            o_prev = jnp.stack([blk(oi_ref, hs) for hs in hss])
            mx = jnp.maximum(lse, lse_prev)
            e_new = jnp.exp2(lse - mx)
            e_old = jnp.exp2(lse_prev - mx)
            tot = e_new + e_old
            o = (e_new * o + e_old * o_prev) / tot
            lse = mx + jnp.log2(tot)
        for g, hs in enumerate(hss):
            o_ref[0, :, 0, :, hs] = o[g].astype(o_ref.dtype).reshape(planes, rows, HEAD_DIM)
            if not is_last:
                outs[1][0, :, 0, :, hs] = lse[g].reshape(planes, rows, HEAD_DIM)


def _band_stage(qk, v, prev, dil, is_last):
    bsz, _, sub, _ = v.shape
    planes = DIL_MAX // dil
    groups = DIL_MAX // planes
    rows = (LANES if planes < DIL_MAX else 2 * LANES) // planes
    view = lambda a: a.reshape(bsz, planes, groups, sub, a.shape[-1])
    nb = sub // rows
    blk = lambda col, prevblk: pl.BlockSpec(
        (1, planes, 1, rows, D_ATTN),
        (lambda b, g, j: (b, 0, g, jnp.maximum(j - 1, 0), col)) if prevblk else (lambda b, g, j: (b, 0, g, j, col)))
    in_specs = [blk(0, False), blk(1, False), blk(1, True), blk(0, False), blk(0, True)]
    args = [view(qk)] * 3 + [view(v)] * 2
    o_spec = blk(0, False)
    if prev is not None:
        in_specs += [o_spec, o_spec]
        args += [view(prev[0]), view(prev[1])]
    out_specs = [o_spec] if is_last else [o_spec, o_spec]
    out_shape = [jax.ShapeDtypeStruct((bsz, planes, groups, sub, D_ATTN), BF16 if is_last else F32)]
    if not is_last:
        out_shape.append(jax.ShapeDtypeStruct((bsz, planes, groups, sub, D_ATTN), F32))
    outs = pl.pallas_call(
        functools.partial(_band_body, has_prev=prev is not None, is_last=is_last, planes=planes, rows=rows),
        grid=(bsz, groups, nb),
        in_specs=in_specs, out_specs=out_specs, out_shape=out_shape,
        compiler_params=_cparams(("parallel", "parallel", "arbitrary")),
        name="dilated_band_d%d" % dil,
    )(*args)
    unview = lambda a: a.reshape(bsz, DIL_MAX, sub, a.shape[-1])
    if is_last:
        return unview(outs[0])
    return unview(outs[0]), unview(outs[1])


def _mixer_dilated(xb2, bsz, seq, w_in):
    d = xb2.shape[1]
    sub = seq // DIL_MAX
    assert seq % (DIL_MAX * 2 * LANES) == 0
    xp = xb2.reshape(bsz, sub, DIL_MAX, d).transpose(0, 2, 1, 3).reshape(bsz * seq, d)
    pos = (jnp.arange(sub)[None, :] * DIL_MAX + jnp.arange(DIL_MAX)[:, None]).reshape(-1)
    qk = _proj(xp, w_in[:, :2 * D_ATTN], pos, half=64, qcols=D_ATTN).reshape(bsz, DIL_MAX, sub, 2 * D_ATTN)
    v = _proj(xp, w_in[:, 2 * D_ATTN:], pos).reshape(bsz, DIL_MAX, sub, D_ATTN)
    prev = None
    for g, (window, dil) in enumerate(DILATED_GROUPS):
        assert window // dil == LANES and DIL_MAX % dil == 0
        prev = _band_stage(qk, v, prev, dil, g == len(DILATED_GROUPS) - 1)
    return prev.transpose(0, 2, 1, 3).reshape(bsz * seq, D_ATTN)


def _flash_body(*refs, kind, hp, tq):
    if kind == "fox":
        q_ref, k_ref, v_ref, cq_ref, ck_ref, o_ref = refs
    else:
        q_ref, k_ref, v_ref, o_ref, kmean_ref = refs
    hg = pl.program_id(1)
    i = pl.program_id(2)
    seq = k_ref.shape[1]
    nsl = tq // LANES
    rel = lax.broadcasted_iota(jnp.int32, (tq, LANES), 0) - lax.broadcasted_iota(jnp.int32, (tq, LANES), 1)
    causal = [rel >= c * LANES for c in range(nsl)]
    heads = []

    if kind == "moba":
        nblk = seq // MOBA_BLOCK
        spb = MOBA_BLOCK // LANES
        bpt = tq // MOBA_BLOCK

        @pl.when(i == 0)
        def _():
            kmean_ref[...] = jnp.zeros_like(kmean_ref)
            for hh in range(hp):
                hs = slice(hh * HEAD_DIM, (hh + 1) * HEAD_DIM)
                for n in range(nblk):
                    kb = k_ref[0, n * MOBA_BLOCK:(n + 1) * MOBA_BLOCK, hs].astype(F32)
                    kmean_ref[hh, n:n + 1, :] = jnp.sum(kb, axis=0, keepdims=True) / MOBA_BLOCK

        row_blk = lax.broadcasted_iota(jnp.int32, (tq, 1), 0) // MOBA_BLOCK
        own = (i * bpt + row_blk).astype(F32)
        blk_id = lax.broadcasted_iota(jnp.int32, (tq, LANES), 1).astype(F32)

    for hh in range(hp):
        hs = slice(hh * HEAD_DIM, (hh + 1) * HEAD_DIM)
        q = q_ref[0, :, hs]
        if kind == "fox":
            cq = jnp.broadcast_to(cq_ref[0, 0, :, hh:hh + 1], (tq, LANES))
            heads.append((hs, q, cq, hg * hp + hh))
        else:
            gate = lax.dot_general(q, kmean_ref[hh].astype(BF16), _DN_T, preferred_element_type=F32)
            gate = jnp.where(blk_id < own, gate, -jnp.inf)
            sel = jnp.zeros(gate.shape, F32)
            for _ in range(min(MOBA_TOPK, nblk)):
                mx = jnp.max(gate, axis=-1, keepdims=True)
                first = jnp.min(jnp.where(gate == mx, blk_id, float(LANES)), axis=-1, keepdims=True)
                pick = blk_id == first
                sel = jnp.where(pick & (first < own), 1.0, sel)
                gate = jnp.where(pick, -jnp.inf, gate)
            heads.append((hs, q, sel, None))

    def step(j, carry, diag):
        off = pl.multiple_of(j * tq, tq)
        out = []
        for (hs, q, aux, head), (m, l, acc) in zip(heads, carry):
            k = k_ref[0, pl.ds(off, tq), hs]
            v = v_ref[0, pl.ds(off, tq), hs]
            if kind == "fox":
                ck = ck_ref[0, pl.ds(head, 1), pl.ds(off, tq)]

                def col_fn(c, sc):
                    sc = sc + (aux - ck[:, c * LANES:(c + 1) * LANES])
                    return jnp.where(causal[c], sc, MASK_VALUE) if diag else sc
            else:
                rowsel = [jnp.max(jnp.where(blk_id == (j * bpt + n).astype(F32), aux, 0.0),
                                  axis=-1, keepdims=True) > 0.0 for n in range(bpt)]

                def col_fn(c, sc):
                    keep = rowsel[c // spb]
                    if diag:
                        keep = keep | ((row_blk == c // spb) & causal[c])
                    return jnp.where(keep, sc, MASK_VALUE)
            out.append(_attn_tile(q, k, v, m, l, acc, col_fn))
        return tuple(out)

    carry = tuple(_attn_init(tq) for _ in range(hp))
    carry = lax.fori_loop(0, i, functools.partial(step, diag=False), carry)
    carry = step(i, carry, True)
    for (hs, _, _, _), (m, l, acc) in zip(heads, carry):
        o_ref[0, :, hs] = _attn_finish(l, acc).astype(o_ref.dtype)


def _flash(kind, q_arr, q_off, k_arr, k_off, v_arr, v_off, extra, hp, tq):
    bsz, seq, _ = q_arr.shape
    wid = hp * HEAD_DIM
    hgs = N_HEADS // hp
    qo, ko, vo = q_off // hp, k_off // hp, v_off // hp
    in_specs = [pl.BlockSpec((1, tq, wid), lambda b, h, i: (b, i, qo + h)),
                pl.BlockSpec((1, seq, wid), lambda b, h, i: (b, 0, ko + h)),
                pl.BlockSpec((1, seq, wid), lambda b, h, i: (b, 0, vo + h))]
    args = [q_arr, k_arr, v_arr]
    scratch = []
    if kind == "fox":
        cum_col, cum_row = extra
        in_specs += [pl.BlockSpec((1, 1, tq, hp), lambda b, h, i: (b, h, i, 0)),
                     pl.BlockSpec((1, N_HEADS, seq), lambda b, h, i: (b, 0, 0))]
        args += [cum_col, cum_row]
    else:
        assert seq // MOBA_BLOCK <= LANES and tq % MOBA_BLOCK == 0
        scratch = [pltpu.VMEM((hp, LANES, HEAD_DIM), F32)]
    return pl.pallas_call(
        functools.partial(_flash_body, kind=kind, hp=hp, tq=tq),
        grid=(bsz, hgs, seq // tq),
        in_specs=in_specs,
        out_specs=pl.BlockSpec((1, tq, wid), lambda b, h, i: (b, i, h)),
        out_shape=jax.ShapeDtypeStruct((bsz, seq, D_ATTN), BF16),
        scratch_shapes=scratch,
        compiler_params=_cparams(("parallel", "parallel", "arbitrary")),
        name="flash_" + kind,
    )(*args)


def _cumsum_body(f_ref, b_ref, o_ref, carry_ref):
    j = pl.program_id(1)

    @pl.when(j == 0)
    def _():
        carry_ref[...] = jnp.zeros_like(carry_ref)

    ts = f_ref.shape[1]
    logf = jax.nn.log_sigmoid(f_ref[0] + b_ref[...])
    tri = (lax.broadcasted_iota(jnp.int32, (ts, ts), 0) >= lax.broadcasted_iota(jnp.int32, (ts, ts), 1)).astype(F32)
    cum = jnp.dot(tri, logf, precision=lax.Precision.HIGHEST, preferred_element_type=F32) + carry_ref[...]
    o_ref[0] = cum * LOG2E
    carry_ref[...] = cum[ts - 1:ts, :]


def _forget_cumsum(f_raw, b_pad, ts=256):
    bsz, seq, _ = f_raw.shape
    return pl.pallas_call(
        _cumsum_body,
        grid=(bsz, seq // ts),
        in_specs=[pl.BlockSpec((1, ts, LANES), lambda b, j: (b, j, 0)),
                  pl.BlockSpec((1, LANES), lambda b, j: (0, 0))],
        out_specs=pl.BlockSpec((1, ts, LANES), lambda b, j: (b, j, 0)),
        out_shape=jax.ShapeDtypeStruct((bsz, seq, LANES), F32),
        scratch_shapes=[pltpu.VMEM((1, LANES), F32)],
        compiler_params=_cparams(("parallel", "arbitrary")),
        name="forget_cumsum",
    )(f_raw, b_pad)


def _pad_cols(w, n):
    return jnp.pad(w, ((0, 0), (0, n - w.shape[1])))


def _mixer_fox(xb2, bsz, seq, w_in, b_forget, hp=2, tq=512):
    tq = min(tq, seq)
    pos = jnp.arange(seq)
    qkv = _proj(xb2, w_in[:, :3 * D_ATTN], pos, qcols=D_ATTN).reshape(bsz, seq, 3 * D_ATTN)
    f_raw = _proj(xb2, _pad_cols(w_in[:, 3 * D_ATTN:], LANES), pos, out_dtype=F32).reshape(bsz, seq, LANES)
    cum = _forget_cumsum(f_raw, _pad_cols(b_forget.reshape(1, N_HEADS), LANES))[..., :N_HEADS]
    cum_row = cum.transpose(0, 2, 1)
    cum_col = cum.reshape(bsz, seq, N_HEADS // hp, hp).transpose(0, 2, 1, 3)
    o = _flash("fox", qkv, 0, qkv, N_HEADS, qkv, 2 * N_HEADS, (cum_col, cum_row), hp, tq)
    return o.reshape(bsz * seq, D_ATTN)


def _mixer_moba(xb2, bsz, seq, w_in, hp=2, tq=512):
    tq = min(tq, seq)
    pos = jnp.arange(seq)
    qk = _proj(xb2, w_in[:, :2 * D_ATTN], pos, half=64, qcols=D_ATTN).reshape(bsz, seq, 2 * D_ATTN)
    v = _proj(xb2, w_in[:, 2 * D_ATTN:], pos).reshape(bsz, seq, D_ATTN)
    assert seq % MOBA_BLOCK == 0
    o = _flash("moba", qk, 0, qk, N_HEADS, v, 0, None, hp, tq)
    return o.reshape(bsz * seq, D_ATTN)


def _kiprep_body(r_ref, g_ref, b_ref, cos_ref, sin_ref, a_ref, b2_ref):
    x = r_ref[...]
    lane = lax.broadcasted_iota(jnp.int32, x.shape, 1)
    inside = lane < IDX_DIM
    mu = jnp.sum(jnp.where(inside, x, 0.0), axis=-1, keepdims=True) / IDX_DIM
    xc = jnp.where(inside, x - mu, 0.0)
    var = jnp.sum(xc * xc, axis=-1, keepdims=True) / IDX_DIM
    y = xc * lax.rsqrt(var + LN_EPS) * g_ref[...] + b_ref[...]
    y = y * cos_ref[...] + _rotate_half(y, IDX_DIM // 2) * sin_ref[...]
    y = jnp.where(inside, y, 0.0)
    a_ref[...] = y.astype(BF16)
    b2_ref[...] = pltpu.roll(y, IDX_DIM, 1).astype(BF16)


def _ki_prep(raw2, g, b, seq, tm=512):
    m = raw2.shape[0]
    tm = min(tm, seq)
    cos, sin = _rope_tables(jnp.arange(seq), IDX_DIM // 2)
    nsb = seq // tm
    row = lambda i: (i, 0)
    const = lambda i: (0, 0)
    return pl.pallas_call(
        _kiprep_body,
        grid=(m // tm,),
        in_specs=[pl.BlockSpec((tm, LANES), row), pl.BlockSpec((1, LANES), const), pl.BlockSpec((1, LANES), const),
                  pl.BlockSpec((tm, LANES), lambda i: (i % nsb, 0)), pl.BlockSpec((tm, LANES), lambda i: (i % nsb, 0))],
        out_specs=[pl.BlockSpec((tm, LANES), row)] * 2,
        out_shape=[jax.ShapeDtypeStruct((m, LANES), BF16)] * 2,
        compiler_params=_cparams(("parallel",)),
        name="dsa_ki_prep",
    )(raw2, _pad_cols(g.reshape(1, IDX_DIM), LANES), _pad_cols(b.reshape(1, IDX_DIM), LANES), cos, sin)


def _dsa_body(qi_ref, wr_ref, kia_ref, kib_ref, q_ref, k_ref, v_ref, o_ref, key_ref, bias_ref,
              *, tq, ch, topk, wscale):
    i = pl.program_id(1)
    seq = k_ref.shape[1]
    nch = (i * tq + tq + ch - 1) // ch
    nsl = ch // LANES
    wi = wr_ref[0][:, IDX_DIM:IDX_DIM + IDX_HEADS] * wscale
    rowpos = i * tq + lax.broadcasted_iota(jnp.int32, (tq, 1), 0)
    lane_ch = lax.broadcasted_iota(jnp.int32, (tq, ch), 1)
    lane_1 = lax.broadcasted_iota(jnp.int32, (tq, LANES), 1)

    def score_chunk(c, _):
        off = pl.multiple_of(c * ch, ch)
        ka = kia_ref[0, pl.ds(off, ch), :]
        kb = kib_ref[0, pl.ds(off, ch), :]
        sc = jnp.zeros((tq, ch), F32)
        for hpair in range(IDX_HEADS // 2):
            qp = qi_ref[0, :, hpair * LANES:(hpair + 1) * LANES]
            for t, kk in enumerate((ka, kb)):
                h = 2 * hpair + t
                lg = lax.dot_general(qp, kk, _DN_T, preferred_element_type=F32)
                sc = sc + wi[:, h:h + 1] * jnp.maximum(lg, 0.0)
        bits = pltpu.bitcast(sc, jnp.int32)
        bits = jnp.where(bits == INT_MIN, 0, bits)
        key = bits ^ ((bits >> 31) & 0x7FFFFFFF)
        key = jnp.where(off + lane_ch <= rowpos, key, NEG_INF_KEY)
        key_ref[:, pl.ds(off, ch)] = key
        return 0

    lax.fori_loop(0, nch, score_chunk, 0)

    def count(pred):
        def cb(c, acc):
            off = pl.multiple_of(c * ch, ch)
            blk = key_ref[:, pl.ds(off, ch)]
            for s_ in range(nsl):
                acc = acc + jnp.where(pred(blk[:, s_ * LANES:(s_ + 1) * LANES], off + s_ * LANES + lane_1), 1, 0)
            return acc
        acc = lax.fori_loop(0, nch, cb, jnp.zeros((tq, LANES), jnp.int32))
        return jnp.sum(acc.astype(F32), axis=-1, keepdims=True).astype(jnp.int32)

    def bisect_val(t, lo):
        cand = lo + (jnp.int32(1) << (31 - t))
        return jnp.where(count(lambda kv, idx: kv >= cand) >= topk, cand, lo)

    thr = lax.fori_loop(0, 32, bisect_val, jnp.full((tq, LANES), INT_MIN, jnp.int32))
    need = topk - count(lambda kv, idx: kv > thr)
    n_eq = count(lambda kv, idx: kv == thr)
    excess = (n_eq > need) & (thr[:, 0:1] > NEG_INF_KEY)
    any_excess = jnp.max(jnp.where(excess, 1.0, 0.0)) > 0.0

    def tie_path():
        def bisect_idx(t, lo):
            cand = lo + (jnp.int32(1) << (12 - t))
            return jnp.where(count(lambda kv, idx: (kv == thr) & (idx < cand)) < need, cand, lo)
        return lax.fori_loop(0, 13, bisect_idx, jnp.zeros((tq, LANES), jnp.int32))

    jmax = lax.cond(any_excess, tie_path, lambda: jnp.full((tq, LANES), seq, jnp.int32))

    def bias_chunk(c, _):
        off = pl.multiple_of(c * ch, ch)
        blk = key_ref[:, pl.ds(off, ch)]
        for s_ in range(nsl):
            kv = blk[:, s_ * LANES:(s_ + 1) * LANES]
            idx = off + s_ * LANES + lane_1
            keep = ((kv > thr) | ((kv == thr) & (idx <= jmax))) & (idx <= rowpos)
            bias_ref[:, pl.ds(pl.multiple_of(off + s_ * LANES, LANES), LANES)] = jnp.where(keep, 0.0, MASK_VALUE)
        return 0

    lax.fori_loop(0, nch, bias_chunk, 0)

    rows = GQA_GROUP * tq
    for g in range(B_KV_HEADS):
        gs = slice(g * HEAD_DIM, (g + 1) * HEAD_DIM)
        qg = jnp.concatenate(
            [q_ref[0, :, (g * GQA_GROUP + r) * HEAD_DIM:(g * GQA_GROUP + r + 1) * HEAD_DIM]
             for r in range(GQA_GROUP)], axis=0)

        def att(c, carry):
            off = pl.multiple_of(c * ch, ch)
            bias = bias_ref[:, pl.ds(off, ch)]

            def col_fn(s_, sc):
                return sc + jnp.concatenate([bias[:, s_ * LANES:(s_ + 1) * LANES]] * GQA_GROUP, axis=0)

            return _attn_tile(qg, k_ref[0, pl.ds(off, ch), gs], v_ref[0, pl.ds(off, ch), gs], *carry, col_fn)

        m, l, acc = lax.fori_loop(0, nch, att, _attn_init(rows))
        o = _attn_finish(l, acc)
        for r in range(GQA_GROUP):
            h = g * GQA_GROUP + r
            o_ref[0, :, h * HEAD_DIM:(h + 1) * HEAD_DIM] = o[r * tq:(r + 1) * tq].astype(o_ref.dtype)


def _mixer_dsa(xb2, bsz, seq, w_in, idx_g, idx_b, tq=256, ch=512):
    ch = min(ch, seq)
    pos = jnp.arange(seq)
    nqk = D_ATTN + KV_WIDTH
    qk = _proj(xb2, w_in[:, :nqk], pos, half=64, qcols=D_ATTN).reshape(bsz, seq, nqk)
    v = _proj(xb2, w_in[:, nqk:nqk + KV_WIDTH], pos).reshape(bsz, seq, KV_WIDTH)
    o_qi = nqk + KV_WIDTH
    n_qi = IDX_HEADS * IDX_DIM
    qi = _proj(xb2, w_in[:, o_qi:o_qi + n_qi], pos, half=IDX_DIM // 2).reshape(bsz, seq, n_qi)
    raw2 = _proj(xb2, _pad_cols(w_in[:, o_qi + n_qi:], LANES), pos, out_dtype=F32)
    kia, kib = _ki_prep(raw2, idx_g, idx_b, seq)
    topk = min(IDX_TOPK_MAX, seq // 4)
    body = functools.partial(_dsa_body, tq=tq, ch=ch, topk=topk, wscale=IDX_HEADS ** -0.5 * IDX_DIM ** -0.5)
    res = lambda b, i: (b, 0, 0)
    o = pl.pallas_call(
        body,
        grid=(bsz, seq // tq),
        in_specs=[pl.BlockSpec((1, tq, n_qi), lambda b, i: (b, i, 0)),
                  pl.BlockSpec((1, tq, LANES), lambda b, i: (b, i, 0)),
                  pl.BlockSpec((1, seq, LANES), res), pl.BlockSpec((1, seq, LANES), res),
                  pl.BlockSpec((1, tq, D_ATTN), lambda b, i: (b, i, 0)),
                  pl.BlockSpec((1, seq, KV_WIDTH), lambda b, i: (b, 0, D_ATTN // KV_WIDTH)),
                  pl.BlockSpec((1, seq, KV_WIDTH), res)],
        out_specs=pl.BlockSpec((1, tq, D_ATTN), lambda b, i: (b, i, 0)),
        out_shape=jax.ShapeDtypeStruct((bsz, seq, D_ATTN), BF16),
        scratch_shapes=[pltpu.VMEM((tq, seq), jnp.int32), pltpu.VMEM((tq, seq), F32)],
        compiler_params=_cparams(("parallel", "arbitrary")),
        name="dsa_select_attend",
    )(qi, raw2.reshape(bsz, seq, LANES), kia.reshape(bsz, seq, LANES), kib.reshape(bsz, seq, LANES), qk, qk, v)
    return o.reshape(bsz * seq, D_ATTN)


def _trunk(x, layers):
    bsz, seq, d = x.shape
    x2 = x.reshape(bsz * seq, d)
    xb2 = x2
    for kind, p in enumerate(layers):
        w_in = p["w_in"].astype(BF16)
        if kind == 0:
            a = _mixer_dilated(xb2, bsz, seq, w_in)
        elif kind == 1:
            a = _mixer_dsa(xb2, bsz, seq, w_in, p["idx_norm_g"], p["idx_norm_b"])
        elif kind == 2:
            a = _mixer_fox(xb2, bsz, seq, w_in, p["b_forget"])
        else:
            a = _mixer_moba(xb2, bsz, seq, w_in)
        x1, x1p, gates, top_exp = _outln(a, p["w_out"].astype(BF16), x2, p["ln1_g"], p["ln1_b"],
                                         p["router_w"].astype(BF16), p["router_b"])
        x2, xb2 = _moe_layer(x1, x1p, gates[:, :TOP_K], top_exp[:, :TOP_K], p["w_gu"], p["b_gu"], p["w_dn"], p["b_dn"],
                             p["ln2_g"], p["ln2_b"])
    return x2.reshape(bsz, seq, d)


def kernel(x, l0_w_in, l0_w_out, l0_ln1_g, l0_ln1_b, l0_router_w, l0_router_b, l0_w_gu, l0_b_gu, l0_w_dn, l0_b_dn, l0_ln2_g, l0_ln2_b, l1_w_in, l1_idx_norm_g, l1_idx_norm_b, l1_w_out, l1_ln1_g, l1_ln1_b, l1_router_w, l1_router_b, l1_w_gu, l1_b_gu, l1_w_dn, l1_b_dn, l1_ln2_g, l1_ln2_b, l2_w_in, l2_b_forget, l2_w_out, l2_ln1_g, l2_ln1_b, l2_router_w, l2_router_b, l2_w_gu, l2_b_gu, l2_w_dn, l2_b_dn, l2_ln2_g, l2_ln2_b, l3_w_in, l3_w_out, l3_ln1_g, l3_ln1_b, l3_router_w, l3_router_b, l3_w_gu, l3_b_gu, l3_w_dn, l3_b_dn, l3_ln2_g, l3_ln2_b):
    names = ("w_out", "ln1_g", "ln1_b", "router_w", "router_b", "w_gu", "b_gu", "w_dn", "b_dn", "ln2_g", "ln2_b")
    l0 = dict(zip(("w_in",) + names, (l0_w_in, l0_w_out, l0_ln1_g, l0_ln1_b, l0_router_w, l0_router_b,
                                      l0_w_gu, l0_b_gu, l0_w_dn, l0_b_dn, l0_ln2_g, l0_ln2_b)))
    l1 = dict(zip(("w_in", "idx_norm_g", "idx_norm_b") + names,
                  (l1_w_in, l1_idx_norm_g, l1_idx_norm_b, l1_w_out, l1_ln1_g, l1_ln1_b, l1_router_w, l1_router_b,
                   l1_w_gu, l1_b_gu, l1_w_dn, l1_b_dn, l1_ln2_g, l1_ln2_b)))
    l2 = dict(zip(("w_in", "b_forget") + names,
                  (l2_w_in, l2_b_forget, l2_w_out, l2_ln1_g, l2_ln1_b, l2_router_w, l2_router_b,
                   l2_w_gu, l2_b_gu, l2_w_dn, l2_b_dn, l2_ln2_g, l2_ln2_b)))
    l3 = dict(zip(("w_in",) + names, (l3_w_in, l3_w_out, l3_ln1_g, l3_ln1_b, l3_router_w, l3_router_b,
                                      l3_w_gu, l3_b_gu, l3_w_dn, l3_b_dn, l3_ln2_g, l3_ln2_b)))
    return _trunk(x, (l0, l1, l2, l3))
```

```python
import functools

import jax
import jax.numpy as jnp
import numpy as np
from jax import lax
from jax.experimental import pallas as pl
from jax.experimental.pallas import tpu as pltpu

N_HEADS = 16
HEAD_DIM = 128
D_MODEL = 2048
D_ATTN = N_HEADS * HEAD_DIM
ROPE_THETA = 10000.0
MASK_VALUE = -1e30
LN_EPS = 1e-5
DILATED_GROUPS = ((128, 1), (512, 4), (2048, 16))
DIL_MAX = 16
B_KV_HEADS = 4
GQA_GROUP = N_HEADS // B_KV_HEADS
KV_WIDTH = B_KV_HEADS * HEAD_DIM
IDX_HEADS = 16
IDX_DIM = 64
IDX_TOPK_MAX = 256
MOBA_BLOCK = 256
MOBA_TOPK = 3
N_EXPERTS = 32
TOP_K = 4
D_EXPERT = 1024
SWIGLU_ALPHA = 1.702
SWIGLU_LIMIT = 7.0
DEPTH = 4
DEEPNORM_ALPHA = (2 * DEPTH) ** 0.25

LANES = 128
VMEM_LIMIT = 56 * 1024 * 1024
MOE_TM = 256
INT_MIN = -(2 ** 31)
NEG_INF_KEY = -2139095041
LOG2E = 1.4426950408889634
QSCALE = HEAD_DIM ** -0.5 * LOG2E

BF16 = jnp.bfloat16
F32 = jnp.float32
_DN_T = (((1,), (1,)), ((), ()))


def _pack_bf16_pairs(v):
    n = v.shape[1] // 2
    lo = pltpu.bitcast(v[:, :n].astype(BF16).astype(F32), jnp.int32)
    hi = pltpu.bitcast(v[:, n:].astype(BF16).astype(F32), jnp.int32)
    return lax.shift_right_logical(lo, jnp.int32(16)) | (hi & jnp.int32(-65536))


def _unpack_bf16_pairs(pk):
    lo = pltpu.bitcast(lax.shift_left(pk, jnp.int32(16)), F32)
    hi = pltpu.bitcast(pk & jnp.int32(-65536), F32)
    return jnp.concatenate([lo, hi], axis=1)


def _cparams(sem):
    return pltpu.CompilerParams(dimension_semantics=sem, vmem_limit_bytes=VMEM_LIMIT)


def _rope_tables(pos, half):
    inv_freq = ROPE_THETA ** (-jnp.arange(half, dtype=F32) / half)
    ang = pos.astype(F32)[:, None] * inv_freq[None, :]
    cos = jnp.tile(jnp.cos(ang), (1, LANES // half))
    sin = jnp.tile(jnp.concatenate([-jnp.sin(ang), jnp.sin(ang)], -1), (1, LANES // (2 * half)))
    return cos, sin


def _rotate_half(t, half):
    if 2 * half == LANES:
        return pltpu.roll(t, half, 1)
    lane = lax.broadcasted_iota(jnp.int32, t.shape, 1)
    first = (lane % (2 * half)) < half
    return jnp.where(first, pltpu.roll(t, LANES - half, 1), pltpu.roll(t, half, 1))


def _proj_body(*refs, half, q_tiles):
    if half:
        x_ref, w_ref, cos_ref, sin_ref, o_ref = refs
    else:
        x_ref, w_ref, o_ref = refs
    acc = jnp.dot(x_ref[...].astype(BF16), w_ref[...], preferred_element_type=F32)
    if q_tiles:
        acc = acc * jnp.where(pl.program_id(1) < q_tiles, QSCALE, 1.0)
    if not half:
        o_ref[...] = acc.astype(o_ref.dtype)
        return
    cos = cos_ref[...]
    sin = sin_ref[...]
    for c in range(acc.shape[1] // LANES):
        t = acc[:, c * LANES:(c + 1) * LANES]
        o_ref[:, c * LANES:(c + 1) * LANES] = (t * cos + _rotate_half(t, half) * sin).astype(o_ref.dtype)


def _proj(x2, w, pos, half=0, out_dtype=BF16, tm=1024, qcols=0):
    m, k = x2.shape
    n = w.shape[1]
    seq = pos.shape[0]
    tm = min(tm, seq)
    tn = 512 if n % 512 == 0 else (256 if n % 256 == 0 else LANES)
    assert qcols % tn == 0 and seq % tm == 0
    in_specs = [pl.BlockSpec((tm, k), lambda i, j: (i, 0)),
                pl.BlockSpec((k, tn), lambda i, j: (0, j))]
    args = [x2, w]
    if half:
        cos, sin = _rope_tables(pos, half)
        nsb = seq // tm
        in_specs += [pl.BlockSpec((tm, LANES), lambda i, j: (i % nsb, 0))] * 2
        args += [cos, sin]
    return pl.pallas_call(
        functools.partial(_proj_body, half=half, q_tiles=qcols // tn),
        grid=(m // tm, n // tn),
        in_specs=in_specs,
        out_specs=pl.BlockSpec((tm, tn), lambda i, j: (i, j)),
        out_shape=jax.ShapeDtypeStruct((m, n), out_dtype),
        compiler_params=_cparams(("parallel", "arbitrary")),
        name="proj_rope%d" % half,
    )(*args)


def _layer_norm_rows(z, g, b):
    mu = jnp.mean(z, axis=-1, keepdims=True)
    zc = z - mu
    var = jnp.mean(zc * zc, axis=-1, keepdims=True)
    return zc * lax.rsqrt(var + LN_EPS) * g + b


def _outln_body(a_ref, w_ref, x_ref, g_ref, b_ref, rw_ref, rb_ref, xo_ref, xp_ref, gt_ref, te_ref):
    h = jnp.dot(a_ref[...], w_ref[...], preferred_element_type=F32)
    y = _layer_norm_rows(DEEPNORM_ALPHA * x_ref[...] + h, g_ref[...], b_ref[...])
    xo_ref[...] = y
    xp_ref[...] = _pack_bf16_pairs(y)
    work = jnp.dot(y.astype(BF16), rw_ref[...], preferred_element_type=F32) + rb_ref[...]
    lane = lax.broadcasted_iota(jnp.int32, work.shape, 1).astype(F32)
    vals, idxs = [], []
    for _ in range(TOP_K):
        mx = jnp.max(work, axis=-1, keepdims=True)
        first = jnp.min(jnp.where(work == mx, lane, float(LANES)), axis=-1, keepdims=True)
        vals.append(mx)
        idxs.append(first)
        work = jnp.where(lane == first, -jnp.inf, work)
    es = [jnp.exp(v - vals[0]) for v in vals]
    denom = functools.reduce(jnp.add, es)
    gates = jnp.zeros(work.shape, F32)
    experts = jnp.zeros(work.shape, F32)
    for k in range(TOP_K):
        gates = jnp.where(lane == k, es[k] / denom, gates)
        experts = jnp.where(lane == k, idxs[k], experts)
    gt_ref[...] = gates
    te_ref[...] = experts.astype(jnp.int32)


def _outln(a, w_out, x2, g, b, rw, rb, tm=256):
    m, d = x2.shape
    row = lambda i: (i, 0)
    const = lambda i: (0, 0)
    rw_pad = _pad_cols(rw, LANES)
    rb_pad = jnp.pad(rb.reshape(1, N_EXPERTS), ((0, 0), (0, LANES - N_EXPERTS)), constant_values=-jnp.inf)
    return pl.pallas_call(
        _outln_body,
        grid=(m // tm,),
        in_specs=[pl.BlockSpec((tm, d), row), pl.BlockSpec((d, d), const), pl.BlockSpec((tm, d), row),
                  pl.BlockSpec((1, d), const), pl.BlockSpec((1, d), const),
                  pl.BlockSpec((d, LANES), const), pl.BlockSpec((1, LANES), const)],
        out_specs=[pl.BlockSpec((tm, d), row), pl.BlockSpec((tm, d // 2), row),
                   pl.BlockSpec((tm, LANES), row), pl.BlockSpec((tm, LANES), row)],
        out_shape=[jax.ShapeDtypeStruct((m, d), F32), jax.ShapeDtypeStruct((m, d // 2), jnp.int32),
                   jax.ShapeDtypeStruct((m, LANES), F32), jax.ShapeDtypeStruct((m, LANES), jnp.int32)],
        compiler_params=_cparams(("parallel",)),
        name="outproj_ln_router",
    )(a, w_out, x2, g.reshape(1, d), b.reshape(1, d), rw_pad, rb_pad)


def _moe_body(be_ref, nb_ref, nx_ref, x_ref, wgu_hbm, bgu_ref, wdn_hbm, bdn_ref, o_ref,
              wgu_f, wdn_f, wgu_b, wdn_b, sem):
    i = pl.program_id(0)
    used = i < nb_ref[0]
    expert = be_ref[i]
    new_expert = (i == 0) | (expert != be_ref[jnp.maximum(i - 1, 0)])

    def weight_copies(e):
        return (pltpu.make_async_copy(wgu_hbm.at[e], wgu_f, sem.at[0]),
                pltpu.make_async_copy(wdn_hbm.at[e], wdn_f, sem.at[1]))

    @pl.when(used & (i == 0))
    def _():
        for c in weight_copies(expert):
            c.start()

    @pl.when(used & new_expert)
    def _():
        for c in weight_copies(expert):
            c.wait()
        rows = 64

        def cast(src, dst):
            def body(r, _):
                sl = pl.ds(pl.multiple_of(r * rows, rows), rows)
                dst[sl, :] = src[sl, :].astype(BF16)
                return 0
            lax.fori_loop(0, src.shape[0] // rows, body, 0)

        cast(wgu_f, wgu_b)
        cast(wdn_f, wdn_b)

        @pl.when(nx_ref[i] >= 0)
        def _():
            for c in weight_copies(nx_ref[i]):
                c.start()

    @pl.when(used)
    def _():
        h = jnp.dot(_unpack_bf16_pairs(x_ref[...]).astype(BF16), wgu_b[...], preferred_element_type=F32) + bgu_ref[0]
        glu = jnp.minimum(h[:, :D_EXPERT], SWIGLU_LIMIT)
        lin = jnp.clip(h[:, D_EXPERT:], -SWIGLU_LIMIT, SWIGLU_LIMIT)
        act = glu * jax.nn.sigmoid(SWIGLU_ALPHA * glu) * (lin + 1.0)
        y = jnp.dot(act.astype(BF16), wdn_b[...], preferred_element_type=F32) + bdn_ref[0]
        o_ref[...] = _pack_bf16_pairs(y)

    @pl.when(jnp.logical_not(used))
    def _():
        o_ref[...] = jnp.zeros_like(o_ref)


def _moe_experts(x_rows, block_exp, n_used, next_exp, w_gu, b_gu, w_dn, b_dn):
    n_rows = x_rows.shape[0]
    d = 2 * x_rows.shape[1]
    n_blocks = n_rows // MOE_TM
    grid_spec = pltpu.PrefetchScalarGridSpec(
        num_scalar_prefetch=3,
        grid=(n_blocks,),
        in_specs=[pl.BlockSpec((MOE_TM, d // 2), lambda i, be, nb, nx: (i, 0)),
                  pl.BlockSpec(memory_space=pl.ANY),
                  pl.BlockSpec((1, 1, 2 * D_EXPERT), lambda i, be, nb, nx: (be[i], 0, 0)),
                  pl.BlockSpec(memory_space=pl.ANY),
                  pl.BlockSpec((1, 1, d), lambda i, be, nb, nx: (be[i], 0, 0))],
        out_specs=pl.BlockSpec((MOE_TM, d // 2), lambda i, be, nb, nx: (i, 0)),
        scratch_shapes=[pltpu.VMEM((d, 2 * D_EXPERT), F32), pltpu.VMEM((D_EXPERT, d), F32),
                        pltpu.VMEM((d, 2 * D_EXPERT), BF16), pltpu.VMEM((D_EXPERT, d), BF16),
                        pltpu.SemaphoreType.DMA((2,))],
    )
    return pl.pallas_call(
        _moe_body,
        grid_spec=grid_spec,
        out_shape=jax.ShapeDtypeStruct((n_rows, d // 2), jnp.int32),
        compiler_params=_cparams(("arbitrary",)),
        name="moe_experts",
    )(block_exp, n_used, next_exp, x_rows, w_gu, b_gu.reshape(N_EXPERTS, 1, -1), w_dn,
      b_dn.reshape(N_EXPERTS, 1, -1))


def _combine_body(*refs):
    y_refs = refs[:TOP_K]
    gt_ref, x_ref, g_ref, b_ref, xo_ref, xb_ref = refs[TOP_K:]
    gt = gt_ref[...]
    y = gt[:, 0:1] * _unpack_bf16_pairs(y_refs[0][...])
    for k in range(1, TOP_K):
        y = y + gt[:, k:k + 1] * _unpack_bf16_pairs(y_refs[k][...])
    out = _layer_norm_rows(DEEPNORM_ALPHA * x_ref[...] + y, g_ref[...], b_ref[...])
    xo_ref[...] = out
    xb_ref[...] = out.astype(BF16)


def _combine_ln(y4, gates, x2, g, b, tm=256):
    m, d = x2.shape
    nb = m // tm
    row = lambda i: (i, 0)
    const = lambda i: (0, 0)
    y_specs = [pl.BlockSpec((tm, d // 2), functools.partial(lambda i, k: (k * nb + i, 0), k=k)) for k in range(TOP_K)]
    return pl.pallas_call(
        _combine_body,
        grid=(nb,),
        in_specs=y_specs + [pl.BlockSpec((tm, TOP_K), row),
                            pl.BlockSpec((tm, d), row), pl.BlockSpec((1, d), const), pl.BlockSpec((1, d), const)],
        out_specs=[pl.BlockSpec((tm, d), row), pl.BlockSpec((tm, d), row)],
        out_shape=[jax.ShapeDtypeStruct((m, d), F32), jax.ShapeDtypeStruct((m, d), BF16)],
        compiler_params=_cparams(("parallel",)),
        name="moe_combine_ln",
    )(*([y4] * TOP_K), gates, x2, g.reshape(1, d), b.reshape(1, d))


def _moe_layer(x1, x1p, gates, top_exp, w_gu, b_gu, w_dn, b_dn, g, b):
    n_tok, d = x1.shape
    n_assign = n_tok * TOP_K
    flat_exp = top_exp.reshape(-1)
    experts = jnp.arange(N_EXPERTS, dtype=jnp.int32)
    counts = jnp.sum((flat_exp[:, None] == experts[None, :]).astype(jnp.int32), axis=0)
    padded = (counts + MOE_TM - 1) // MOE_TM * MOE_TM
    pad_end = jnp.cumsum(padded)
    n_blocks = n_assign // MOE_TM + N_EXPERTS
    n_rows = n_blocks * MOE_TM
    block_start = jnp.arange(n_blocks, dtype=jnp.int32) * MOE_TM
    block_exp = jnp.minimum(jnp.sum((pad_end[None, :] <= block_start[:, None]).astype(jnp.int32), axis=1),
                            N_EXPERTS - 1).astype(jnp.int32)
    n_used = (pad_end[-1] // MOE_TM).astype(jnp.int32).reshape(1)
    later = (experts[None, :] > experts[:, None]) & (counts[None, :] > 0)
    nxt = jnp.min(jnp.where(later, experts[None, :], N_EXPERTS), axis=1)
    nxt = jnp.where(nxt == N_EXPERTS, -1, nxt)
    next_exp = jnp.sum(jnp.where(block_exp[:, None] == experts[None, :], nxt[None, :], 0), axis=1).astype(jnp.int32)
    idx_bits = (n_assign + N_EXPERTS * MOE_TM - 1).bit_length()
    a_idx = jnp.arange(n_assign, dtype=jnp.int32)
    f_idx = n_assign + jnp.arange(N_EXPERTS * MOE_TM, dtype=jnp.int32)
    f_num = jnp.arange(MOE_TM, dtype=jnp.int32)[None, :]
    f_major = jnp.where(f_num < (padded - counts)[:, None], 2 * experts[:, None] + 1, 2 * N_EXPERTS).reshape(-1)
    keys = jnp.concatenate([(2 * flat_exp << idx_bits) | a_idx, (f_major << idx_bits) | f_idx])
    src = jnp.sort(keys) & ((1 << idx_bits) - 1)
    row_tok = jnp.where(src < n_assign, src // TOP_K, 0)[:n_rows]
    _, inv = lax.sort_key_val(src, jnp.arange(src.shape[0], dtype=jnp.int32))
    pos = inv[:n_assign]
    x_rows = x1p[row_tok]
    y_rows = _moe_experts(x_rows, block_exp, n_used, next_exp, w_gu, b_gu, w_dn, b_dn)
    y4 = y_rows[pos.reshape(n_tok, TOP_K).T.reshape(-1)]
    return _combine_ln(y4, gates, x1, g, b)


def _attn_tile(q, k, v, m, l, acc, col_fn):
    s = lax.dot_general(q, k, _DN_T, preferred_element_type=F32)
    cols = [col_fn(c, s[:, c * LANES:(c + 1) * LANES]) for c in range(s.shape[1] // LANES)]
    m_new = jnp.maximum(m, jnp.max(functools.reduce(jnp.maximum, cols), axis=-1, keepdims=True))
    alpha = jnp.exp2(m - m_new)
    ps = [jnp.exp2(c - m_new) for c in cols]
    l = alpha * l + functools.reduce(jnp.add, ps)
    p = jnp.concatenate([x.astype(BF16) for x in ps], axis=1)
    acc = alpha * acc + jnp.dot(p, v, preferred_element_type=F32)
    return m_new, l, acc


def _attn_init(rows):
    return (jnp.full((rows, LANES), -jnp.inf, F32), jnp.zeros((rows, LANES), F32),
            jnp.zeros((rows, HEAD_DIM), F32))


def _attn_finish(l, acc):
    return acc / jnp.sum(l, axis=-1, keepdims=True)


def _band_body(*refs, has_prev, is_last, planes, rows):
    if has_prev:
        q_ref, kc_ref, kp_ref, vc_ref, vp_ref, oi_ref, li_ref = refs[:7]
        outs = refs[7:]
    else:
        q_ref, kc_ref, kp_ref, vc_ref, vp_ref = refs[:5]
        outs = refs[5:]
    o_ref = outs[0]
    j = pl.program_id(2)
    w = planes * rows

    def local(t):
        return (t % rows) * planes + t // rows

    tq = lax.broadcasted_iota(jnp.int32, (w, 2 * w), 0)
    tk = lax.broadcasted_iota(jnp.int32, (w, 2 * w), 1)
    cur = tk >= w
    dist = (w + local(tq)) - (local(tk % w) + jnp.where(cur, w, 0))
    mask = (dist >= 0) & (dist <= LANES) & (cur | (j > 0))

    def blk(ref, hs):
        return ref[0, :, 0, :, hs].reshape(w, HEAD_DIM)

    group = N_HEADS if w <= LANES else N_HEADS // 2
    for h0 in range(0, N_HEADS, group):
        hss = [slice(h * HEAD_DIM, (h + 1) * HEAD_DIM) for h in range(h0, h0 + group)]
        vcats = [jnp.concatenate([blk(vp_ref, hs), blk(vc_ref, hs)], axis=0) for hs in hss]
        s = jnp.stack([lax.dot_general(blk(q_ref, hs),
                                       jnp.concatenate([blk(kp_ref, hs), blk(kc_ref, hs)], axis=0),
                                       _DN_T, preferred_element_type=F32) for hs in hss])
        s = jnp.where(mask[None], s, MASK_VALUE)
        m = jnp.max(s, axis=-1, keepdims=True)
        p = jnp.exp2(s - m)
        l = jnp.sum(p, axis=-1, keepdims=True)
        pn = (p / l).astype(BF16)
        o = jnp.stack([jnp.dot(pn[g], vcats[g], preferred_element_type=F32) for g in range(group)])
        lse = jnp.broadcast_to(m + jnp.log2(l), o.shape)
        if has_prev:
            lse_prev = jnp.stack([blk(li_ref, hs) for hs in hss])
            o_prev = jnp.stack([blk(oi_ref, hs) for hs in hss])
            mx = jnp.maximum(lse, lse_prev)
            e_new = jnp.exp2(lse - mx)
            e_old = jnp.exp2(lse_prev - mx)
            tot = e_new + e_old
            o = (e_new * o + e_old * o_prev) / tot
            lse = mx + jnp.log2(tot)
        for g, hs in enumerate(hss):
            o_ref[0, :, 0, :, hs] = o[g].astype(o_ref.dtype).reshape(planes, rows, HEAD_DIM)
            if not is_last:
                outs[1][0, :, 0, :, hs] = lse[g].reshape(planes, rows, HEAD_DIM)


def _band_stage(qk, v, prev, dil, is_last):
    bsz, _, sub, _ = v.shape
    planes = DIL_MAX // dil
    groups = DIL_MAX // planes
    rows = (LANES if planes < DIL_MAX else 2 * LANES) // planes
    view = lambda a: a.reshape(bsz, planes, groups, sub, a.shape[-1])
    nb = sub // rows
    blk = lambda col, prevblk: pl.BlockSpec(
        (1, planes, 1, rows, D_ATTN),
        (lambda b, g, j: (b, 0, g, jnp.maximum(j - 1, 0), col)) if prevblk else (lambda b, g, j: (b, 0, g, j, col)))
    in_specs = [blk(0, False), blk(1, False), blk(1, True), blk(0, False), blk(0, True)]
    args = [view(qk)] * 3 + [view(v)] * 2
    o_spec = blk(0, False)
    if prev is not None:
        in_specs += [o_spec, o_spec]
        args += [view(prev[0]), view(prev[1])]
    out_specs = [o_spec] if is_last else [o_spec, o_spec]
    out_shape = [jax.ShapeDtypeStruct((bsz, planes, groups, sub, D_ATTN), BF16 if is_last else F32)]
    if not is_last:
        out_shape.append(jax.ShapeDtypeStruct((bsz, planes, groups, sub, D_ATTN), F32))
    outs = pl.pallas_call(
        functools.partial(_band_body, has_prev=prev is not None, is_last=is_last, planes=planes, rows=rows),
        grid=(bsz, groups, nb),
        in_specs=in_specs, out_specs=out_specs, out_shape=out_shape,
        compiler_params=_cparams(("parallel", "parallel", "arbitrary")),
        name="dilated_band_d%d" % dil,
    )(*args)
    unview = lambda a: a.reshape(bsz, DIL_MAX, sub, a.shape[-1])
    if is_last:
        return unview(outs[0])
    return unview(outs[0]), unview(outs[1])


def _mixer_dilated(xb2, bsz, seq, w_in):
    d = xb2.shape[1]
    sub = seq // DIL_MAX
    assert seq % (DIL_MAX * 2 * LANES) == 0
    xp = xb2.reshape(bsz, sub, DIL_MAX, d).transpose(0, 2, 1, 3).reshape(bsz * seq, d)
    pos = (jnp.arange(sub)[None, :] * DIL_MAX + jnp.arange(DIL_MAX)[:, None]).reshape(-1)
    qk = _proj(xp, w_in[:, :2 * D_ATTN], pos, half=64, qcols=D_ATTN).reshape(bsz, DIL_MAX, sub, 2 * D_ATTN)
    v = _proj(xp, w_in[:, 2 * D_ATTN:], pos).reshape(bsz, DIL_MAX, sub, D_ATTN)
    prev = None
    for g, (window, dil) in enumerate(DILATED_GROUPS):
        assert window // dil == LANES and DIL_MAX % dil == 0
        prev = _band_stage(qk, v, prev, dil, g == len(DILATED_GROUPS) - 1)
    return prev.transpose(0, 2, 1, 3).reshape(bsz * seq, D_ATTN)


def _flash_body(*refs, kind, hp, tq):
    if kind == "fox":
        q_ref, k_ref, v_ref, cq_ref, ck_ref, o_ref = refs
    else:
        q_ref, k_ref, v_ref, o_ref, kmean_ref = refs
    hg = pl.program_id(1)
    i = pl.program_id(2)
    seq = k_ref.shape[1]
    nsl = tq // LANES
    rel = lax.broadcasted_iota(jnp.int32, (tq, LANES), 0) - lax.broadcasted_iota(jnp.int32, (tq, LANES), 1)
    causal = [rel >= c * LANES for c in range(nsl)]
    heads = []

    if kind == "moba":
        nblk = seq // MOBA_BLOCK
        spb = MOBA_BLOCK // LANES
        bpt = tq // MOBA_BLOCK

        @pl.when(i == 0)
        def _():
            kmean_ref[...] = jnp.zeros_like(kmean_ref)
            for hh in range(hp):
                hs = slice(hh * HEAD_DIM, (hh + 1) * HEAD_DIM)
                for n in range(nblk):
                    kb = k_ref[0, n * MOBA_BLOCK:(n + 1) * MOBA_BLOCK, hs].astype(F32)
                    kmean_ref[hh, n:n + 1, :] = jnp.sum(kb, axis=0, keepdims=True) / MOBA_BLOCK

        row_blk = lax.broadcasted_iota(jnp.int32, (tq, 1), 0) // MOBA_BLOCK
        own = (i * bpt + row_blk).astype(F32)
        blk_id = lax.broadcasted_iota(jnp.int32, (tq, LANES), 1).astype(F32)

    for hh in range(hp):
        hs = slice(hh * HEAD_DIM, (hh + 1) * HEAD_DIM)
        q = q_ref[0, :, hs]
        if kind == "fox":
            cq = jnp.broadcast_to(cq_ref[0, 0, :, hh:hh + 1], (tq, LANES))
            heads.append((hs, q, cq, hg * hp + hh))
        else:
            gate = lax.dot_general(q, kmean_ref[hh].astype(BF16), _DN_T, preferred_element_type=F32)
            gate = jnp.where(blk_id < own, gate, -jnp.inf)
            sel = jnp.zeros(gate.shape, F32)
            for _ in range(min(MOBA_TOPK, nblk)):
                mx = jnp.max(gate, axis=-1, keepdims=True)
                first = jnp.min(jnp.where(gate == mx, blk_id, float(LANES)), axis=-1, keepdims=True)
                pick = blk_id == first
                sel = jnp.where(pick & (first < own), 1.0, sel)
                gate = jnp.where(pick, -jnp.inf, gate)
            heads.append((hs, q, sel, None))

    def step(j, carry, diag):
        off = pl.multiple_of(j * tq, tq)
        out = []
        for (hs, q, aux, head), (m, l, acc) in zip(heads, carry):
            k = k_ref[0, pl.ds(off, tq), hs]
            v = v_ref[0, pl.ds(off, tq), hs]
            if kind == "fox":
                ck = ck_ref[0, pl.ds(head, 1), pl.ds(off, tq)]

                def col_fn(c, sc):
                    sc = sc + (aux - ck[:, c * LANES:(c + 1) * LANES])
                    return jnp.where(causal[c], sc, MASK_VALUE) if diag else sc
            else:
                rowsel = [jnp.max(jnp.where(blk_id == (j * bpt + n).astype(F32), aux, 0.0),
                                  axis=-1, keepdims=True) > 0.0 for n in range(bpt)]

                def col_fn(c, sc):
                    keep = rowsel[c // spb]
                    if diag:
                        keep = keep | ((row_blk == c // spb) & causal[c])
                    return jnp.where(keep, sc, MASK_VALUE)
            out.append(_attn_tile(q, k, v, m, l, acc, col_fn))
        return tuple(out)

    carry = tuple(_attn_init(tq) for _ in range(hp))
    carry = lax.fori_loop(0, i, functools.partial(step, diag=False), carry)
    carry = step(i, carry, True)
    for (hs, _, _, _), (m, l, acc) in zip(heads, carry):
        o_ref[0, :, hs] = _attn_finish(l, acc).astype(o_ref.dtype)


def _flash(kind, q_arr, q_off, k_arr, k_off, v_arr, v_off, extra, hp, tq):
    bsz, seq, _ = q_arr.shape
    wid = hp * HEAD_DIM
    hgs = N_HEADS // hp
    qo, ko, vo = q_off // hp, k_off // hp, v_off // hp
    in_specs = [pl.BlockSpec((1, tq, wid), lambda b, h, i: (b, i, qo + h)),
                pl.BlockSpec((1, seq, wid), lambda b, h, i: (b, 0, ko + h)),
                pl.BlockSpec((1, seq, wid), lambda b, h, i: (b, 0, vo + h))]
    args = [q_arr, k_arr, v_arr]
    scratch = []
    if kind == "fox":
        cum_col, cum_row = extra
        in_specs += [pl.BlockSpec((1, 1, tq, hp), lambda b, h, i: (b, h, i, 0)),
                     pl.BlockSpec((1, N_HEADS, seq), lambda b, h, i: (b, 0, 0))]
        args += [cum_col, cum_row]
    else:
        assert seq // MOBA_BLOCK <= LANES and tq % MOBA_BLOCK == 0
        scratch = [pltpu.VMEM((hp, LANES, HEAD_DIM), F32)]
    return pl.pallas_call(
        functools.partial(_flash_body, kind=kind, hp=hp, tq=tq),
        grid=(bsz, hgs, seq // tq),
        in_specs=in_specs,
        out_specs=pl.BlockSpec((1, tq, wid), lambda b, h, i: (b, i, h)),
        out_shape=jax.ShapeDtypeStruct((bsz, seq, D_ATTN), BF16),
        scratch_shapes=scratch,
        compiler_params=_cparams(("parallel", "parallel", "arbitrary")),
        name="flash_" + kind,
    )(*args)


def _cumsum_body(f_ref, b_ref, o_ref, carry_ref):
    j = pl.program_id(1)

    @pl.when(j == 0)
    def _():
        carry_ref[...] = jnp.zeros_like(carry_ref)

    ts = f_ref.shape[1]
    logf = jax.nn.log_sigmoid(f_ref[0] + b_ref[...])
    tri = (lax.broadcasted_iota(jnp.int32, (ts, ts), 0) >= lax.broadcasted_iota(jnp.int32, (ts, ts), 1)).astype(F32)
    cum = jnp.dot(tri, logf, precision=lax.Precision.HIGHEST, preferred_element_type=F32) + carry_ref[...]
    o_ref[0] = cum * LOG2E
    carry_ref[...] = cum[ts - 1:ts, :]


def _forget_cumsum(f_raw, b_pad, ts=256):
    bsz, seq, _ = f_raw.shape
    return pl.pallas_call(
        _cumsum_body,
        grid=(bsz, seq // ts),
        in_specs=[pl.BlockSpec((1, ts, LANES), lambda b, j: (b, j, 0)),
                  pl.BlockSpec((1, LANES), lambda b, j: (0, 0))],
        out_specs=pl.BlockSpec((1, ts, LANES), lambda b, j: (b, j, 0)),
        out_shape=jax.ShapeDtypeStruct((bsz, seq, LANES), F32),
        scratch_shapes=[pltpu.VMEM((1, LANES), F32)],
        compiler_params=_cparams(("parallel", "arbitrary")),
        name="forget_cumsum",
    )(f_raw, b_pad)


def _pad_cols(w, n):
    return jnp.pad(w, ((0, 0), (0, n - w.shape[1])))


def _mixer_fox(xb2, bsz, seq, w_in, b_forget, hp=2, tq=512):
    tq = min(tq, seq)
    pos = jnp.arange(seq)
    qkv = _proj(xb2, w_in[:, :3 * D_ATTN], pos, qcols=D_ATTN).reshape(bsz, seq, 3 * D_ATTN)
    f_raw = _proj(xb2, _pad_cols(w_in[:, 3 * D_ATTN:], LANES), pos, out_dtype=F32).reshape(bsz, seq, LANES)
    cum = _forget_cumsum(f_raw, _pad_cols(b_forget.reshape(1, N_HEADS), LANES))[..., :N_HEADS]
    cum_row = cum.transpose(0, 2, 1)
    cum_col = cum.reshape(bsz, seq, N_HEADS // hp, hp).transpose(0, 2, 1, 3)
    o = _flash("fox", qkv, 0, qkv, N_HEADS, qkv, 2 * N_HEADS, (cum_col, cum_row), hp, tq)
    return o.reshape(bsz * seq, D_ATTN)


def _mixer_moba(xb2, bsz, seq, w_in, hp=2, tq=512):
    tq = min(tq, seq)
    pos = jnp.arange(seq)
    qk = _proj(xb2, w_in[:, :2 * D_ATTN], pos, half=64, qcols=D_ATTN).reshape(bsz, seq, 2 * D_ATTN)
    v = _proj(xb2, w_in[:, 2 * D_ATTN:], pos).reshape(bsz, seq, D_ATTN)
    assert seq % MOBA_BLOCK == 0
    o = _flash("moba", qk, 0, qk, N_HEADS, v, 0, None, hp, tq)
    return o.reshape(bsz * seq, D_ATTN)


def _kiprep_body(r_ref, g_ref, b_ref, cos_ref, sin_ref, a_ref, b2_ref):
    x = r_ref[...]
    lane = lax.broadcasted_iota(jnp.int32, x.shape, 1)
    inside = lane < IDX_DIM
    mu = jnp.sum(jnp.where(inside, x, 0.0), axis=-1, keepdims=True) / IDX_DIM
    xc = jnp.where(inside, x - mu, 0.0)
    var = jnp.sum(xc * xc, axis=-1, keepdims=True) / IDX_DIM
    y = xc * lax.rsqrt(var + LN_EPS) * g_ref[...] + b_ref[...]
    y = y * cos_ref[...] + _rotate_half(y, IDX_DIM // 2) * sin_ref[...]
    y = jnp.where(inside, y, 0.0)
    a_ref[...] = y.astype(BF16)
    b2_ref[...] = pltpu.roll(y, IDX_DIM, 1).astype(BF16)


def _ki_prep(raw2, g, b, seq, tm=512):
    m = raw2.shape[0]
    tm = min(tm, seq)
    cos, sin = _rope_tables(jnp.arange(seq), IDX_DIM // 2)
    nsb = seq // tm
    row = lambda i: (i, 0)
    const = lambda i: (0, 0)
    return pl.pallas_call(
        _kiprep_body,
        grid=(m // tm,),
        in_specs=[pl.BlockSpec((tm, LANES), row), pl.BlockSpec((1, LANES), const), pl.BlockSpec((1, LANES), const),
                  pl.BlockSpec((tm, LANES), lambda i: (i % nsb, 0)), pl.BlockSpec((tm, LANES), lambda i: (i % nsb, 0))],
        out_specs=[pl.BlockSpec((tm, LANES), row)] * 2,
        out_shape=[jax.ShapeDtypeStruct((m, LANES), BF16)] * 2,
        compiler_params=_cparams(("parallel",)),
        name="dsa_ki_prep",
    )(raw2, _pad_cols(g.reshape(1, IDX_DIM), LANES), _pad_cols(b.reshape(1, IDX_DIM), LANES), cos, sin)


def _dsa_body(qi_ref, wr_ref, kia_ref, kib_ref, q_ref, k_ref, v_ref, o_ref, key_ref, bias_ref,
              *, tq, ch, topk, wscale):
    i = pl.program_id(1)
    seq = k_ref.shape[1]
    nch = (i * tq + tq + ch - 1) // ch
    nsl = ch // LANES
    wi = wr_ref[0][:, IDX_DIM:IDX_DIM + IDX_HEADS] * wscale
    rowpos = i * tq + lax.broadcasted_iota(jnp.int32, (tq, 1), 0)
    lane_ch = lax.broadcasted_iota(jnp.int32, (tq, ch), 1)
    lane_1 = lax.broadcasted_iota(jnp.int32, (tq, LANES), 1)

    def score_chunk(c, _):
        off = pl.multiple_of(c * ch, ch)
        ka = kia_ref[0, pl.ds(off, ch), :]
        kb = kib_ref[0, pl.ds(off, ch), :]
        sc = jnp.zeros((tq, ch), F32)
        for hpair in range(IDX_HEADS // 2):
            qp = qi_ref[0, :, hpair * LANES:(hpair + 1) * LANES]
            for t, kk in enumerate((ka, kb)):
                h = 2 * hpair + t
                lg = lax.dot_general(qp, kk, _DN_T, preferred_element_type=F32)
                sc = sc + wi[:, h:h + 1] * jnp.maximum(lg, 0.0)
        bits = pltpu.bitcast(sc, jnp.int32)
        bits = jnp.where(bits == INT_MIN, 0, bits)
        key = bits ^ ((bits >> 31) & 0x7FFFFFFF)
        key = jnp.where(off + lane_ch <= rowpos, key, NEG_INF_KEY)
        key_ref[:, pl.ds(off, ch)] = key
        return 0

    lax.fori_loop(0, nch, score_chunk, 0)

    def count(pred):
        def cb(c, acc):
            off = pl.multiple_of(c * ch, ch)
            blk = key_ref[:, pl.ds(off, ch)]
            for s_ in range(nsl):
                acc = acc + jnp.where(pred(blk[:, s_ * LANES:(s_ + 1) * LANES], off + s_ * LANES + lane_1), 1, 0)
            return acc
        acc = lax.fori_loop(0, nch, cb, jnp.zeros((tq, LANES), jnp.int32))
        return jnp.sum(acc.astype(F32), axis=-1, keepdims=True).astype(jnp.int32)

    def bisect_val(t, lo):
        cand = lo + (jnp.int32(1) << (31 - t))
        return jnp.where(count(lambda kv, idx: kv >= cand) >= topk, cand, lo)

    thr = lax.fori_loop(0, 32, bisect_val, jnp.full((tq, LANES), INT_MIN, jnp.int32))
    need = topk - count(lambda kv, idx: kv > thr)
    n_eq = count(lambda kv, idx: kv == thr)
    excess = (n_eq > need) & (thr[:, 0:1] > NEG_INF_KEY)
    any_excess = jnp.max(jnp.where(excess, 1.0, 0.0)) > 0.0

    def tie_path():
        def bisect_idx(t, lo):
            cand = lo + (jnp.int32(1) << (12 - t))
            return jnp.where(count(lambda kv, idx: (kv == thr) & (idx < cand)) < need, cand, lo)
        return lax.fori_loop(0, 13, bisect_idx, jnp.zeros((tq, LANES), jnp.int32))

    jmax = lax.cond(any_excess, tie_path, lambda: jnp.full((tq, LANES), seq, jnp.int32))

    def bias_chunk(c, _):
        off = pl.multiple_of(c * ch, ch)
        blk = key_ref[:, pl.ds(off, ch)]
        for s_ in range(nsl):
            kv = blk[:, s_ * LANES:(s_ + 1) * LANES]
            idx = off + s_ * LANES + lane_1
            keep = ((kv > thr) | ((kv == thr) & (idx <= jmax))) & (idx <= rowpos)
            bias_ref[:, pl.ds(pl.multiple_of(off + s_ * LANES, LANES), LANES)] = jnp.where(keep, 0.0, MASK_VALUE)
        return 0

    lax.fori_loop(0, nch, bias_chunk, 0)

    rows = GQA_GROUP * tq
    for g in range(B_KV_HEADS):
        gs = slice(g * HEAD_DIM, (g + 1) * HEAD_DIM)
        qg = jnp.concatenate(
            [q_ref[0, :, (g * GQA_GROUP + r) * HEAD_DIM:(g * GQA_GROUP + r + 1) * HEAD_DIM]
             for r in range(GQA_GROUP)], axis=0)

        def att(c, carry):
            off = pl.multiple_of(c * ch, ch)
            bias = bias_ref[:, pl.ds(off, ch)]

            def col_fn(s_, sc):
                return sc + jnp.concatenate([bias[:, s_ * LANES:(s_ + 1) * LANES]] * GQA_GROUP, axis=0)

            return _attn_tile(qg, k_ref[0, pl.ds(off, ch), gs], v_ref[0, pl.ds(off, ch), gs], *carry, col_fn)

        m, l, acc = lax.fori_loop(0, nch, att, _attn_init(rows))
        o = _attn_finish(l, acc)
        for r in range(GQA_GROUP):
            h = g * GQA_GROUP + r
            o_ref[0, :, h * HEAD_DIM:(h + 1) * HEAD_DIM] = o[r * tq:(r + 1) * tq].astype(o_ref.dtype)


def _mixer_dsa(xb2, bsz, seq, w_in, idx_g, idx_b, tq=256, ch=512):
    ch = min(ch, seq)
    pos = jnp.arange(seq)
    nqk = D_ATTN + KV_WIDTH
    qk = _proj(xb2, w_in[:, :nqk], pos, half=64, qcols=D_ATTN).reshape(bsz, seq, nqk)
    v = _proj(xb2, w_in[:, nqk:nqk + KV_WIDTH], pos).reshape(bsz, seq, KV_WIDTH)
    o_qi = nqk + KV_WIDTH
    n_qi = IDX_HEADS * IDX_DIM
    qi = _proj(xb2, w_in[:, o_qi:o_qi + n_qi], pos, half=IDX_DIM // 2).reshape(bsz, seq, n_qi)
    raw2 = _proj(xb2, _pad_cols(w_in[:, o_qi + n_qi:], LANES), pos, out_dtype=F32)
    kia, kib = _ki_prep(raw2, idx_g, idx_b, seq)
    topk = min(IDX_TOPK_MAX, seq // 4)
    body = functools.partial(_dsa_body, tq=tq, ch=ch, topk=topk, wscale=IDX_HEADS ** -0.5 * IDX_DIM ** -0.5)
    res = lambda b, i: (b, 0, 0)
    o = pl.pallas_call(
        body,
        grid=(bsz, seq // tq),
        in_specs=[pl.BlockSpec((1, tq, n_qi), lambda b, i: (b, i, 0)),
                  pl.BlockSpec((1, tq, LANES), lambda b, i: (b, i, 0)),
                  pl.BlockSpec((1, seq, LANES), res), pl.BlockSpec((1, seq, LANES), res),
                  pl.BlockSpec((1, tq, D_ATTN), lambda b, i: (b, i, 0)),
                  pl.BlockSpec((1, seq, KV_WIDTH), lambda b, i: (b, 0, D_ATTN // KV_WIDTH)),
                  pl.BlockSpec((1, seq, KV_WIDTH), res)],
        out_specs=pl.BlockSpec((1, tq, D_ATTN), lambda b, i: (b, i, 0)),
        out_shape=jax.ShapeDtypeStruct((bsz, seq, D_ATTN), BF16),
        scratch_shapes=[pltpu.VMEM((tq, seq), jnp.int32), pltpu.VMEM((tq, seq), F32)],
        compiler_params=_cparams(("parallel", "arbitrary")),
        name="dsa_select_attend",
    )(qi, raw2.reshape(bsz, seq, LANES), kia.reshape(bsz, seq, LANES), kib.reshape(bsz, seq, LANES), qk, qk, v)
    return o.reshape(bsz * seq, D_ATTN)


def _trunk(x, layers):
    bsz, seq, d = x.shape
    x2 = x.reshape(bsz * seq, d)
    xb2 = x2
    for kind, p in enumerate(layers):
        w_in = p["w_in"].astype(BF16)
        if kind == 0:
            a = _mixer_dilated(xb2, bsz, seq, w_in)
        elif kind == 1:
            a = _mixer_dsa(xb2, bsz, seq, w_in, p["idx_norm_g"], p["idx_norm_b"])
        elif kind == 2:
            a = _mixer_fox(xb2, bsz, seq, w_in, p["b_forget"])
        else:
            a = _mixer_moba(xb2, bsz, seq, w_in)
        x1, x1p, gates, top_exp = _outln(a, p["w_out"].astype(BF16), x2, p["ln1_g"], p["ln1_b"],
                                         p["router_w"].astype(BF16), p["router_b"])
        x2, xb2 = _moe_layer(x1, x1p, gates[:, :TOP_K], top_exp[:, :TOP_K], p["w_gu"], p["b_gu"], p["w_dn"], p["b_dn"],
                             p["ln2_g"], p["ln2_b"])
    return x2.reshape(bsz, seq, d)


def kernel(x, l0_w_in, l0_w_out, l0_ln1_g, l0_ln1_b, l0_router_w, l0_router_b, l0_w_gu, l0_b_gu, l0_w_dn, l0_b_dn, l0_ln2_g, l0_ln2_b, l1_w_in, l1_idx_norm_g, l1_idx_norm_b, l1_w_out, l1_ln1_g, l1_ln1_b, l1_router_w, l1_router_b, l1_w_gu, l1_b_gu, l1_w_dn, l1_b_dn, l1_ln2_g, l1_ln2_b, l2_w_in, l2_b_forget, l2_w_out, l2_ln1_g, l2_ln1_b, l2_router_w, l2_router_b, l2_w_gu, l2_b_gu, l2_w_dn, l2_b_dn, l2_ln2_g, l2_ln2_b, l3_w_in, l3_w_out, l3_ln1_g, l3_ln1_b, l3_router_w, l3_router_b, l3_w_gu, l3_b_gu, l3_w_dn, l3_b_dn, l3_ln2_g, l3_ln2_b):
    names = ("w_out", "ln1_g", "ln1_b", "router_w", "router_b", "w_gu", "b_gu", "w_dn", "b_dn", "ln2_g", "ln2_b")
    l0 = dict(zip(("w_in",) + names, (l0_w_in, l0_w_out, l0_ln1_g, l0_ln1_b, l0_router_w, l0_router_b,
                                      l0_w_gu, l0_b_gu, l0_w_dn, l0_b_dn, l0_ln2_g, l0_ln2_b)))
    l1 = dict(zip(("w_in", "idx_norm_g", "idx_norm_b") + names,
                  (l1_w_in, l1_idx_norm_g, l1_idx_norm_b, l1_w_out, l1_ln1_g, l1_ln1_b, l1_router_w, l1_router_b,
                   l1_w_gu, l1_b_gu, l1_w_dn, l1_b_dn, l1_ln2_g, l1_ln2_b)))
    l2 = dict(zip(("w_in", "b_forget") + names,
                  (l2_w_in, l2_b_forget, l2_w_out, l2_ln1_g, l2_ln1_b, l2_router_w, l2_router_b,
                   l2_w_gu, l2_b_gu, l2_w_dn, l2_b_dn, l2_ln2_g, l2_ln2_b)))
    l3 = dict(zip(("w_in",) + names, (l3_w_in, l3_w_out, l3_ln1_g, l3_ln1_b, l3_router_w, l3_router_b,
                                      l3_w_gu, l3_b_gu, l3_w_dn, l3_b_dn, l3_ln2_g, l3_ln2_b)))
    return _trunk(x, (l0, l1, l2, l3))
```

```python
import functools

import jax
import jax.numpy as jnp
import numpy as np
from jax import lax
from jax.experimental import pallas as pl
from jax.experimental.pallas import tpu as pltpu

N_HEADS = 16
HEAD_DIM = 128
D_MODEL = 2048
D_ATTN = N_HEADS * HEAD_DIM
ROPE_THETA = 10000.0
MASK_VALUE = -1e30
LN_EPS = 1e-5
DILATED_GROUPS = ((128, 1), (512, 4), (2048, 16))
DIL_MAX = 16
B_KV_HEADS = 4
GQA_GROUP = N_HEADS // B_KV_HEADS
KV_WIDTH = B_KV_HEADS * HEAD_DIM
IDX_HEADS = 16
IDX_DIM = 64
IDX_TOPK_MAX = 256
MOBA_BLOCK = 256
MOBA_TOPK = 3
N_EXPERTS = 32
TOP_K = 4
D_EXPERT = 1024
SWIGLU_ALPHA = 1.702
SWIGLU_LIMIT = 7.0
DEPTH = 4
DEEPNORM_ALPHA = (2 * DEPTH) ** 0.25

LANES = 128
VMEM_LIMIT = 56 * 1024 * 1024
MOE_TM = 256
INT_MIN = -(2 ** 31)
NEG_INF_KEY = -2139095041
LOG2E = 1.4426950408889634
QSCALE = HEAD_DIM ** -0.5 * LOG2E

BF16 = jnp.bfloat16
F32 = jnp.float32
_DN_T = (((1,), (1,)), ((), ()))


def _pack_bf16_pairs(v):
    n = v.shape[1] // 2
    lo = pltpu.bitcast(v[:, :n].astype(BF16).astype(F32), jnp.int32)
    hi = pltpu.bitcast(v[:, n:].astype(BF16).astype(F32), jnp.int32)
    return lax.shift_right_logical(lo, jnp.int32(16)) | (hi & jnp.int32(-65536))


def _unpack_bf16_pairs(pk):
    lo = pltpu.bitcast(lax.shift_left(pk, jnp.int32(16)), F32)
    hi = pltpu.bitcast(pk & jnp.int32(-65536), F32)
    return jnp.concatenate([lo, hi], axis=1)


def _cparams(sem):
    return pltpu.CompilerParams(dimension_semantics=sem, vmem_limit_bytes=VMEM_LIMIT)


def _rope_tables(pos, half):
    inv_freq = ROPE_THETA ** (-jnp.arange(half, dtype=F32) / half)
    ang = pos.astype(F32)[:, None] * inv_freq[None, :]
    cos = jnp.tile(jnp.cos(ang), (1, LANES // half))
    sin = jnp.tile(jnp.concatenate([-jnp.sin(ang), jnp.sin(ang)], -1), (1, LANES // (2 * half)))
    return cos, sin


def _rotate_half(t, half):
    if 2 * half == LANES:
        return pltpu.roll(t, half, 1)
    lane = lax.broadcasted_iota(jnp.int32, t.shape, 1)
    first = (lane % (2 * half)) < half
    return jnp.where(first, pltpu.roll(t, LANES - half, 1), pltpu.roll(t, half, 1))


def _proj_body(*refs, half, q_tiles):
    if half:
        x_ref, w_ref, cos_ref, sin_ref, o_ref = refs
    else:
        x_ref, w_ref, o_ref = refs
    acc = jnp.dot(x_ref[...].astype(BF16), w_ref[...], preferred_element_type=F32)
    if q_tiles:
        acc = acc * jnp.where(pl.program_id(1) < q_tiles, QSCALE, 1.0)
    if not half:
        o_ref[...] = acc.astype(o_ref.dtype)
        return
    cos = cos_ref[...]
    sin = sin_ref[...]
    for c in range(acc.shape[1] // LANES):
        t = acc[:, c * LANES:(c + 1) * LANES]
        o_ref[:, c * LANES:(c + 1) * LANES] = (t * cos + _rotate_half(t, half) * sin).astype(o_ref.dtype)


def _proj(x2, w, pos, half=0, out_dtype=BF16, tm=1024, qcols=0):
    m, k = x2.shape
    n = w.shape[1]
    seq = pos.shape[0]
    tm = min(tm, seq)
    tn = 1024 if n % 1024 == 0 else (512 if n % 512 == 0 else (256 if n % 256 == 0 else LANES))
    assert qcols % tn == 0 and seq % tm == 0
    in_specs = [pl.BlockSpec((tm, k), lambda i, j: (i, 0)),
                pl.BlockSpec((k, tn), lambda i, j: (0, j))]
    args = [x2, w]
    if half:
        cos, sin = _rope_tables(pos, half)
        nsb = seq // tm
        in_specs += [pl.BlockSpec((tm, LANES), lambda i, j: (i % nsb, 0))] * 2
        args += [cos, sin]
    return pl.pallas_call(
        functools.partial(_proj_body, half=half, q_tiles=qcols // tn),
        grid=(m // tm, n // tn),
        in_specs=in_specs,
        out_specs=pl.BlockSpec((tm, tn), lambda i, j: (i, j)),
        out_shape=jax.ShapeDtypeStruct((m, n), out_dtype),
        compiler_params=_cparams(("parallel", "arbitrary")),
        name="proj_rope%d" % half,
    )(*args)


def _layer_norm_rows(z, g, b):
    mu = jnp.mean(z, axis=-1, keepdims=True)
    zc = z - mu
    var = jnp.mean(zc * zc, axis=-1, keepdims=True)
    return zc * lax.rsqrt(var + LN_EPS) * g + b


def _outln_body(a_ref, w_ref, x_ref, g_ref, b_ref, rw_ref, rb_ref, xo_ref, xp_ref, gt_ref, te_ref):
    h = jnp.dot(a_ref[...], w_ref[...], preferred_element_type=F32)
    y = _layer_norm_rows(DEEPNORM_ALPHA * x_ref[...] + h, g_ref[...], b_ref[...])
    xo_ref[...] = y
    xp_ref[...] = _pack_bf16_pairs(y)
    work = jnp.dot(y.astype(BF16), rw_ref[...], preferred_element_type=F32) + rb_ref[...]
    lane = lax.broadcasted_iota(jnp.int32, work.shape, 1).astype(F32)
    vals, idxs = [], []
    for _ in range(TOP_K):
        mx = jnp.max(work, axis=-1, keepdims=True)
        first = jnp.min(jnp.where(work == mx, lane, float(LANES)), axis=-1, keepdims=True)
        vals.append(mx)
        idxs.append(first)
        work = jnp.where(lane == first, -jnp.inf, work)
    es = [jnp.exp(v - vals[0]) for v in vals]
    denom = functools.reduce(jnp.add, es)
    gates = jnp.zeros(work.shape, F32)
    experts = jnp.zeros(work.shape, F32)
    for k in range(TOP_K):
        gates = jnp.where(lane == k, es[k] / denom, gates)
        experts = jnp.where(lane == k, idxs[k], experts)
    gt_ref[...] = gates
    te_ref[...] = experts.astype(jnp.int32)


def _outln(a, w_out, x2, g, b, rw, rb, tm=256):
    m, d = x2.shape
    row = lambda i: (i, 0)
    const = lambda i: (0, 0)
    rw_pad = _pad_cols(rw, LANES)
    rb_pad = jnp.pad(rb.reshape(1, N_EXPERTS), ((0, 0), (0, LANES - N_EXPERTS)), constant_values=-jnp.inf)
    return pl.pallas_call(
        _outln_body,
        grid=(m // tm,),
        in_specs=[pl.BlockSpec((tm, d), row), pl.BlockSpec((d, d), const), pl.BlockSpec((tm, d), row),
                  pl.BlockSpec((1, d), const), pl.BlockSpec((1, d), const),
                  pl.BlockSpec((d, LANES), const), pl.BlockSpec((1, LANES), const)],
        out_specs=[pl.BlockSpec((tm, d), row), pl.BlockSpec((tm, d // 2), row),
                   pl.BlockSpec((tm, LANES), row), pl.BlockSpec((tm, LANES), row)],
        out_shape=[jax.ShapeDtypeStruct((m, d), F32), jax.ShapeDtypeStruct((m, d // 2), jnp.int32),
                   jax.ShapeDtypeStruct((m, LANES), F32), jax.ShapeDtypeStruct((m, LANES), jnp.int32)],
        compiler_params=_cparams(("parallel",)),
        name="outproj_ln_router",
    )(a, w_out, x2, g.reshape(1, d), b.reshape(1, d), rw_pad, rb_pad)


def _moe_body(be_ref, nb_ref, nx_ref, x_ref, wgu_hbm, bgu_ref, wdn_hbm, bdn_ref, o_ref,
              wgu_f, wdn_f, wgu_b, wdn_b, sem):
    i = pl.program_id(0)
    used = i < nb_ref[0]
    expert = be_ref[i]
    new_expert = (i == 0) | (expert != be_ref[jnp.maximum(i - 1, 0)])

    def weight_copies(e):
        return (pltpu.make_async_copy(wgu_hbm.at[e], wgu_f, sem.at[0]),
                pltpu.make_async_copy(wdn_hbm.at[e], wdn_f, sem.at[1]))

    @pl.when(used & (i == 0))
    def _():
        for c in weight_copies(expert):
            c.start()

    @pl.when(used & new_expert)
    def _():
        for c in weight_copies(expert):
            c.wait()
        rows = 64

        def cast(src, dst):
            def body(r, _):
                sl = pl.ds(pl.multiple_of(r * rows, rows), rows)
                dst[sl, :] = src[sl, :].astype(BF16)
                return 0
            lax.fori_loop(0, src.shape[0] // rows, body, 0)

        cast(wgu_f, wgu_b)
        cast(wdn_f, wdn_b)

        @pl.when(nx_ref[i] >= 0)
        def _():
            for c in weight_copies(nx_ref[i]):
                c.start()

    @pl.when(used)
    def _():
        h = jnp.dot(_unpack_bf16_pairs(x_ref[...]).astype(BF16), wgu_b[...], preferred_element_type=F32) + bgu_ref[0]
        glu = jnp.minimum(h[:, :D_EXPERT], SWIGLU_LIMIT)
        lin = jnp.clip(h[:, D_EXPERT:], -SWIGLU_LIMIT, SWIGLU_LIMIT)
        act = glu * jax.nn.sigmoid(SWIGLU_ALPHA * glu) * (lin + 1.0)
        y = jnp.dot(act.astype(BF16), wdn_b[...], preferred_element_type=F32) + bdn_ref[0]
        o_ref[...] = _pack_bf16_pairs(y)

    @pl.when(jnp.logical_not(used))
    def _():
        o_ref[...] = jnp.zeros_like(o_ref)


def _moe_experts(x_rows, block_exp, n_used, next_exp, w_gu, b_gu, w_dn, b_dn):
    n_rows = x_rows.shape[0]
    d = 2 * x_rows.shape[1]
    n_blocks = n_rows // MOE_TM
    grid_spec = pltpu.PrefetchScalarGridSpec(
        num_scalar_prefetch=3,
        grid=(n_blocks,),
        in_specs=[pl.BlockSpec((MOE_TM, d // 2), lambda i, be, nb, nx: (i, 0)),
                  pl.BlockSpec(memory_space=pl.ANY),
                  pl.BlockSpec((1, 1, 2 * D_EXPERT), lambda i, be, nb, nx: (be[i], 0, 0)),
                  pl.BlockSpec(memory_space=pl.ANY),
                  pl.BlockSpec((1, 1, d), lambda i, be, nb, nx: (be[i], 0, 0))],
        out_specs=pl.BlockSpec((MOE_TM, d // 2), lambda i, be, nb, nx: (i, 0)),
        scratch_shapes=[pltpu.VMEM((d, 2 * D_EXPERT), F32), pltpu.VMEM((D_EXPERT, d), F32),
                        pltpu.VMEM((d, 2 * D_EXPERT), BF16), pltpu.VMEM((D_EXPERT, d), BF16),
                        pltpu.SemaphoreType.DMA((2,))],
    )
    return pl.pallas_call(
        _moe_body,
        grid_spec=grid_spec,
        out_shape=jax.ShapeDtypeStruct((n_rows, d // 2), jnp.int32),
        compiler_params=_cparams(("arbitrary",)),
        name="moe_experts",
    )(block_exp, n_used, next_exp, x_rows, w_gu, b_gu.reshape(N_EXPERTS, 1, -1), w_dn,
      b_dn.reshape(N_EXPERTS, 1, -1))


def _combine_body(*refs):
    y_refs = refs[:TOP_K]
    gt_ref, x_ref, g_ref, b_ref, xo_ref, xb_ref = refs[TOP_K:]
    gt = gt_ref[...]
    y = gt[:, 0:1] * _unpack_bf16_pairs(y_refs[0][...])
    for k in range(1, TOP_K):
        y = y + gt[:, k:k + 1] * _unpack_bf16_pairs(y_refs[k][...])
    out = _layer_norm_rows(DEEPNORM_ALPHA * x_ref[...] + y, g_ref[...], b_ref[...])
    xo_ref[...] = out
    xb_ref[...] = out.astype(BF16)


def _combine_ln(y4, gates, x2, g, b, tm=256):
    m, d = x2.shape
    nb = m // tm
    row = lambda i: (i, 0)
    const = lambda i: (0, 0)
    y_specs = [pl.BlockSpec((tm, d // 2), functools.partial(lambda i, k: (k * nb + i, 0), k=k)) for k in range(TOP_K)]
    return pl.pallas_call(
        _combine_body,
        grid=(nb,),
        in_specs=y_specs + [pl.BlockSpec((tm, TOP_K), row),
                            pl.BlockSpec((tm, d), row), pl.BlockSpec((1, d), const), pl.BlockSpec((1, d), const)],
        out_specs=[pl.BlockSpec((tm, d), row), pl.BlockSpec((tm, d), row)],
        out_shape=[jax.ShapeDtypeStruct((m, d), F32), jax.ShapeDtypeStruct((m, d), BF16)],
        compiler_params=_cparams(("parallel",)),
        name="moe_combine_ln",
    )(*([y4] * TOP_K), gates, x2, g.reshape(1, d), b.reshape(1, d))


def _moe_layer(x1, x1p, gates, top_exp, w_gu, b_gu, w_dn, b_dn, g, b):
    n_tok, d = x1.shape
    n_assign = n_tok * TOP_K
    flat_exp = top_exp.reshape(-1)
    experts = jnp.arange(N_EXPERTS, dtype=jnp.int32)
    counts = jnp.sum((flat_exp[:, None] == experts[None, :]).astype(jnp.int32), axis=0)
    padded = (counts + MOE_TM - 1) // MOE_TM * MOE_TM
    pad_end = jnp.cumsum(padded)
    n_blocks = n_assign // MOE_TM + N_EXPERTS
    n_rows = n_blocks * MOE_TM
    block_start = jnp.arange(n_blocks, dtype=jnp.int32) * MOE_TM
    block_exp = jnp.minimum(jnp.sum((pad_end[None, :] <= block_start[:, None]).astype(jnp.int32), axis=1),
                            N_EXPERTS - 1).astype(jnp.int32)
    n_used = (pad_end[-1] // MOE_TM).astype(jnp.int32).reshape(1)
    later = (experts[None, :] > experts[:, None]) & (counts[None, :] > 0)
    nxt = jnp.min(jnp.where(later, experts[None, :], N_EXPERTS), axis=1)
    nxt = jnp.where(nxt == N_EXPERTS, -1, nxt)
    next_exp = jnp.sum(jnp.where(block_exp[:, None] == experts[None, :], nxt[None, :], 0), axis=1).astype(jnp.int32)
    idx_bits = (n_assign + N_EXPERTS * MOE_TM - 1).bit_length()
    a_idx = jnp.arange(n_assign, dtype=jnp.int32)
    f_idx = n_assign + jnp.arange(N_EXPERTS * MOE_TM, dtype=jnp.int32)
    f_num = jnp.arange(MOE_TM, dtype=jnp.int32)[None, :]
    f_major = jnp.where(f_num < (padded - counts)[:, None], 2 * experts[:, None] + 1, 2 * N_EXPERTS).reshape(-1)
    keys = jnp.concatenate([(2 * flat_exp << idx_bits) | a_idx, (f_major << idx_bits) | f_idx])
    src = jnp.sort(keys) & ((1 << idx_bits) - 1)
    row_tok = jnp.where(src < n_assign, src // TOP_K, 0)[:n_rows]
    _, inv = lax.sort_key_val(src, jnp.arange(src.shape[0], dtype=jnp.int32))
    pos = inv[:n_assign]
    x_rows = x1p[row_tok]
    y_rows = _moe_experts(x_rows, block_exp, n_used, next_exp, w_gu, b_gu, w_dn, b_dn)
    y4 = y_rows[pos.reshape(n_tok, TOP_K).T.reshape(-1)]
    return _combine_ln(y4, gates, x1, g, b)


def _attn_tile(q, k, v, m, l, acc, col_fn):
    s = lax.dot_general(q, k, _DN_T, preferred_element_type=F32)
    cols = [col_fn(c, s[:, c * LANES:(c + 1) * LANES]) for c in range(s.shape[1] // LANES)]
    m_new = jnp.maximum(m, jnp.max(functools.reduce(jnp.maximum, cols), axis=-1, keepdims=True))
    alpha = jnp.exp2(m - m_new)
    ps = [jnp.exp2(c - m_new) for c in cols]
    l = alpha * l + functools.reduce(jnp.add, ps)
    p = jnp.concatenate([x.astype(BF16) for x in ps], axis=1)
    acc = alpha * acc + jnp.dot(p, v, preferred_element_type=F32)
    return m_new, l, acc


def _attn_init(rows):
    return (jnp.full((rows, LANES), -jnp.inf, F32), jnp.zeros((rows, LANES), F32),
            jnp.zeros((rows, HEAD_DIM), F32))


def _attn_finish(l, acc):
    return acc / jnp.sum(l, axis=-1, keepdims=True)


def _band_body(*refs, has_prev, is_last, planes, rows):
    if has_prev:
        q_ref, kc_ref, kp_ref, vc_ref, vp_ref, oi_ref, li_ref = refs[:7]
        outs = refs[7:]
    else:
        q_ref, kc_ref, kp_ref, vc_ref, vp_ref = refs[:5]
        outs = refs[5:]
    o_ref = outs[0]
    j = pl.program_id(2)
    w = planes * rows

    def local(t):
        return (t % rows) * planes + t // rows

    tq = lax.broadcasted_iota(jnp.int32, (w, 2 * w), 0)
    tk = lax.broadcasted_iota(jnp.int32, (w, 2 * w), 1)
    cur = tk >= w
    dist = (w + local(tq)) - (local(tk % w) + jnp.where(cur, w, 0))
    mask = (dist >= 0) & (dist <= LANES) & (cur | (j > 0))

    def blk(ref, hs):
        return ref[0, :, 0, :, hs].reshape(w, HEAD_DIM)

    group = N_HEADS if w <= LANES else N_HEADS // 2
    for h0 in range(0, N_HEADS, group):
        hss = [slice(h * HEAD_DIM, (h + 1) * HEAD_DIM) for h in range(h0, h0 + group)]
        vcats = [jnp.concatenate([blk(vp_ref, hs), blk(vc_ref, hs)], axis=0) for hs in hss]
        s = jnp.stack([lax.dot_general(blk(q_ref, hs),
                                       jnp.concatenate([blk(kp_ref, hs), blk(kc_ref, hs)], axis=0),
                                       _DN_T, preferred_element_type=F32) for hs in hss])
        s = jnp.where(mask[None], s, MASK_VALUE)
        m = jnp.max(s, axis=-1, keepdims=True)
        p = jnp.exp2(s - m)
        l = jnp.sum(p, axis=-1, keepdims=True)
        pn = (p / l).astype(BF16)
        o = jnp.stack([jnp.dot(pn[g], vcats[g], preferred_element_type=F32) for g in range(group)])
        lse = jnp.broadcast_to(m + jnp.log2(l), o.shape)
        if has_prev:
            lse_prev = jnp.stack([blk(li_ref, hs) for hs in hss])
            o_prev = jnp.stack([blk(oi_ref, hs) for hs in hss])
            mx = jnp.maximum(lse, lse_prev)
            e_new = jnp.exp2(lse - mx)
            e_old = jnp.exp2(lse_prev - mx)
            tot = e_new + e_old
            o = (e_new * o + e_old * o_prev) / tot
            lse = mx + jnp.log2(tot)
        for g, hs in enumerate(hss):
            o_ref[0, :, 0, :, hs] = o[g].astype(o_ref.dtype).reshape(planes, rows, HEAD_DIM)
            if not is_last:
                outs[1][0, :, 0, :, hs] = lse[g].reshape(planes, rows, HEAD_DIM)


def _band_stage(qk, v, prev, dil, is_last):
    bsz, _, sub, _ = v.shape
    planes = DIL_MAX // dil
    groups = DIL_MAX // planes
    rows = (LANES if planes < DIL_MAX else 2 * LANES) // planes
    view = lambda a: a.reshape(bsz, planes, groups, sub, a.shape[-1])
    nb = sub // rows
    blk = lambda col, prevblk: pl.BlockSpec(
        (1, planes, 1, rows, D_ATTN),
        (lambda b, g, j: (b, 0, g, jnp.maximum(j - 1, 0), col)) if prevblk else (lambda b, g, j: (b, 0, g, j, col)))
    in_specs = [blk(0, False), blk(1, False), blk(1, True), blk(0, False), blk(0, True)]
    args = [view(qk)] * 3 + [view(v)] * 2
    o_spec = blk(0, False)
    if prev is not None:
        in_specs += [o_spec, o_spec]
        args += [view(prev[0]), view(prev[1])]
    out_specs = [o_spec] if is_last else [o_spec, o_spec]
    out_shape = [jax.ShapeDtypeStruct((bsz, planes, groups, sub, D_ATTN), BF16 if is_last else F32)]
    if not is_last:
        out_shape.append(jax.ShapeDtypeStruct((bsz, planes, groups, sub, D_ATTN), F32))
    outs = pl.pallas_call(
        functools.partial(_band_body, has_prev=prev is not None, is_last=is_last, planes=planes, rows=rows),
        grid=(bsz, groups, nb),
        in_specs=in_specs, out_specs=out_specs, out_shape=out_shape,
        compiler_params=_cparams(("parallel", "parallel", "arbitrary")),
        name="dilated_band_d%d" % dil,
    )(*args)
    unview = lambda a: a.reshape(bsz, DIL_MAX, sub, a.shape[-1])
    if is_last:
        return unview(outs[0])
    return unview(outs[0]), unview(outs[1])


def _mixer_dilated(xb2, bsz, seq, w_in):
    d = xb2.shape[1]
    sub = seq // DIL_MAX
    assert seq % (DIL_MAX * 2 * LANES) == 0
    xp = xb2.reshape(bsz, sub, DIL_MAX, d).transpose(0, 2, 1, 3).reshape(bsz * seq, d)
    pos = (jnp.arange(sub)[None, :] * DIL_MAX + jnp.arange(DIL_MAX)[:, None]).reshape(-1)
    qk = _proj(xp, w_in[:, :2 * D_ATTN], pos, half=64, qcols=D_ATTN).reshape(bsz, DIL_MAX, sub, 2 * D_ATTN)
    v = _proj(xp, w_in[:, 2 * D_ATTN:], pos).reshape(bsz, DIL_MAX, sub, D_ATTN)
    prev = None
    for g, (window, dil) in enumerate(DILATED_GROUPS):
        assert window // dil == LANES and DIL_MAX % dil == 0
        prev = _band_stage(qk, v, prev, dil, g == len(DILATED_GROUPS) - 1)
    return prev.transpose(0, 2, 1, 3).reshape(bsz * seq, D_ATTN)


def _flash_body(*refs, kind, hp, tq):
    if kind == "fox":
        q_ref, k_ref, v_ref, cq_ref, ck_ref, o_ref = refs
    else:
        q_ref, k_ref, v_ref, o_ref, kmean_ref = refs
    hg = pl.program_id(1)
    i = pl.program_id(2)
    seq = k_ref.shape[1]
    nsl = tq // LANES
    rel = lax.broadcasted_iota(jnp.int32, (tq, LANES), 0) - lax.broadcasted_iota(jnp.int32, (tq, LANES), 1)
    causal = [rel >= c * LANES for c in range(nsl)]
    heads = []

    if kind == "moba":
        nblk = seq // MOBA_BLOCK
        spb = MOBA_BLOCK // LANES
        bpt = tq // MOBA_BLOCK

        @pl.when(i == 0)
        def _():
            kmean_ref[...] = jnp.zeros_like(kmean_ref)
            for hh in range(hp):
                hs = slice(hh * HEAD_DIM, (hh + 1) * HEAD_DIM)
                for n in range(nblk):
                    kb = k_ref[0, n * MOBA_BLOCK:(n + 1) * MOBA_BLOCK, hs].astype(F32)
                    kmean_ref[hh, n:n + 1, :] = jnp.sum(kb, axis=0, keepdims=True) / MOBA_BLOCK

        row_blk = lax.broadcasted_iota(jnp.int32, (tq, 1), 0) // MOBA_BLOCK
        own = (i * bpt + row_blk).astype(F32)
        blk_id = lax.broadcasted_iota(jnp.int32, (tq, LANES), 1).astype(F32)

    for hh in range(hp):
        hs = slice(hh * HEAD_DIM, (hh + 1) * HEAD_DIM)
        q = q_ref[0, :, hs]
        if kind == "fox":
            cq = jnp.broadcast_to(cq_ref[0, 0, :, hh:hh + 1], (tq, LANES))
            heads.append((hs, q, cq, hg * hp + hh))
        else:
            gate = lax.dot_general(q, kmean_ref[hh].astype(BF16), _DN_T, preferred_element_type=F32)
            gate = jnp.where(blk_id < own, gate, -jnp.inf)
            sel = jnp.zeros(gate.shape, F32)
            for _ in range(min(MOBA_TOPK, nblk)):
                mx = jnp.max(gate, axis=-1, keepdims=True)
                first = jnp.min(jnp.where(gate == mx, blk_id, float(LANES)), axis=-1, keepdims=True)
                pick = blk_id == first
                sel = jnp.where(pick & (first < own), 1.0, sel)
                gate = jnp.where(pick, -jnp.inf, gate)
            heads.append((hs, q, sel, None))

    def step(j, carry, diag):
        off = pl.multiple_of(j * tq, tq)
        out = []
        for (hs, q, aux, head), (m, l, acc) in zip(heads, carry):
            k = k_ref[0, pl.ds(off, tq), hs]
            v = v_ref[0, pl.ds(off, tq), hs]
            if kind == "fox":
                ck = ck_ref[0, pl.ds(head, 1), pl.ds(off, tq)]

                def col_fn(c, sc):
                    sc = sc + (aux - ck[:, c * LANES:(c + 1) * LANES])
                    return jnp.where(causal[c], sc, MASK_VALUE) if diag else sc
            else:
                rowsel = [jnp.max(jnp.where(blk_id == (j * bpt + n).astype(F32), aux, 0.0),
                                  axis=-1, keepdims=True) > 0.0 for n in range(bpt)]

                def col_fn(c, sc):
                    keep = rowsel[c // spb]
                    if diag:
                        keep = keep | ((row_blk == c // spb) & causal[c])
                    return jnp.where(keep, sc, MASK_VALUE)
            out.append(_attn_tile(q, k, v, m, l, acc, col_fn))
        return tuple(out)

    carry = tuple(_attn_init(tq) for _ in range(hp))
    carry = lax.fori_loop(0, i, functools.partial(step, diag=False), carry)
    carry = step(i, carry, True)
    for (hs, _, _, _), (m, l, acc) in zip(heads, carry):
        o_ref[0, :, hs] = _attn_finish(l, acc).astype(o_ref.dtype)


def _flash(kind, q_arr, q_off, k_arr, k_off, v_arr, v_off, extra, hp, tq):
    bsz, seq, _ = q_arr.shape
    wid = hp * HEAD_DIM
    hgs = N_HEADS // hp
    qo, ko, vo = q_off // hp, k_off // hp, v_off // hp
    in_specs = [pl.BlockSpec((1, tq, wid), lambda b, h, i: (b, i, qo + h)),
                pl.BlockSpec((1, seq, wid), lambda b, h, i: (b, 0, ko + h)),
                pl.BlockSpec((1, seq, wid), lambda b, h, i: (b, 0, vo + h))]
    args = [q_arr, k_arr, v_arr]
    scratch = []
    if kind == "fox":
        cum_col, cum_row = extra
        in_specs += [pl.BlockSpec((1, 1, tq, hp), lambda b, h, i: (b, h, i, 0)),
                     pl.BlockSpec((1, N_HEADS, seq), lambda b, h, i: (b, 0, 0))]
        args += [cum_col, cum_row]
    else:
        assert seq // MOBA_BLOCK <= LANES and tq % MOBA_BLOCK == 0
        scratch = [pltpu.VMEM((hp, LANES, HEAD_DIM), F32)]
    return pl.pallas_call(
        functools.partial(_flash_body, kind=kind, hp=hp, tq=tq),
        grid=(bsz, hgs, seq // tq),
        in_specs=in_specs,
        out_specs=pl.BlockSpec((1, tq, wid), lambda b, h, i: (b, i, h)),
        out_shape=jax.ShapeDtypeStruct((bsz, seq, D_ATTN), BF16),
        scratch_shapes=scratch,
        compiler_params=_cparams(("parallel", "parallel", "arbitrary")),
        name="flash_" + kind,
    )(*args)


def _cumsum_body(f_ref, b_ref, o_ref, carry_ref):
    j = pl.program_id(1)

    @pl.when(j == 0)
    def _():
        carry_ref[...] = jnp.zeros_like(carry_ref)

    ts = f_ref.shape[1]
    logf = jax.nn.log_sigmoid(f_ref[0] + b_ref[...])
    tri = (lax.broadcasted_iota(jnp.int32, (ts, ts), 0) >= lax.broadcasted_iota(jnp.int32, (ts, ts), 1)).astype(F32)
    cum = jnp.dot(tri, logf, precision=lax.Precision.HIGHEST, preferred_element_type=F32) + carry_ref[...]
    o_ref[0] = cum * LOG2E
    carry_ref[...] = cum[ts - 1:ts, :]


def _forget_cumsum(f_raw, b_pad, ts=256):
    bsz, seq, _ = f_raw.shape
    return pl.pallas_call(
        _cumsum_body,
        grid=(bsz, seq // ts),
        in_specs=[pl.BlockSpec((1, ts, LANES), lambda b, j: (b, j, 0)),
                  pl.BlockSpec((1, LANES), lambda b, j: (0, 0))],
        out_specs=pl.BlockSpec((1, ts, LANES), lambda b, j: (b, j, 0)),
        out_shape=jax.ShapeDtypeStruct((bsz, seq, LANES), F32),
        scratch_shapes=[pltpu.VMEM((1, LANES), F32)],
        compiler_params=_cparams(("parallel", "arbitrary")),
        name="forget_cumsum",
    )(f_raw, b_pad)


def _pad_cols(w, n):
    return jnp.pad(w, ((0, 0), (0, n - w.shape[1])))


def _mixer_fox(xb2, bsz, seq, w_in, b_forget, hp=4, tq=512):
    tq = min(tq, seq)
    pos = jnp.arange(seq)
    qkv = _proj(xb2, w_in[:, :3 * D_ATTN], pos, qcols=D_ATTN).reshape(bsz, seq, 3 * D_ATTN)
    f_raw = _proj(xb2, _pad_cols(w_in[:, 3 * D_ATTN:], LANES), pos, out_dtype=F32).reshape(bsz, seq, LANES)
    cum = _forget_cumsum(f_raw, _pad_cols(b_forget.reshape(1, N_HEADS), LANES))[..., :N_HEADS]
    cum_row = cum.transpose(0, 2, 1)
    cum_col = cum.reshape(bsz, seq, N_HEADS // hp, hp).transpose(0, 2, 1, 3)
    o = _flash("fox", qkv, 0, qkv, N_HEADS, qkv, 2 * N_HEADS, (cum_col, cum_row), hp, tq)
    return o.reshape(bsz * seq, D_ATTN)


def _mixer_moba(xb2, bsz, seq, w_in, hp=4, tq=512):
    tq = min(tq, seq)
    pos = jnp.arange(seq)
    qk = _proj(xb2, w_in[:, :2 * D_ATTN], pos, half=64, qcols=D_ATTN).reshape(bsz, seq, 2 * D_ATTN)
    v = _proj(xb2, w_in[:, 2 * D_ATTN:], pos).reshape(bsz, seq, D_ATTN)
    assert seq % MOBA_BLOCK == 0
    o = _flash("moba", qk, 0, qk, N_HEADS, v, 0, None, hp, tq)
    return o.reshape(bsz * seq, D_ATTN)


def _kiprep_body(r_ref, g_ref, b_ref, cos_ref, sin_ref, a_ref, b2_ref):
    x = r_ref[...]
    lane = lax.broadcasted_iota(jnp.int32, x.shape, 1)
    inside = lane < IDX_DIM
    mu = jnp.sum(jnp.where(inside, x, 0.0), axis=-1, keepdims=True) / IDX_DIM
    xc = jnp.where(inside, x - mu, 0.0)
    var = jnp.sum(xc * xc, axis=-1, keepdims=True) / IDX_DIM
    y = xc * lax.rsqrt(var + LN_EPS) * g_ref[...] + b_ref[...]
    y = y * cos_ref[...] + _rotate_half(y, IDX_DIM // 2) * sin_ref[...]
    y = jnp.where(inside, y, 0.0)
    a_ref[...] = y.astype(BF16)
    b2_ref[...] = pltpu.roll(y, IDX_DIM, 1).astype(BF16)


def _ki_prep(raw2, g, b, seq, tm=512):
    m = raw2.shape[0]
    tm = min(tm, seq)
    cos, sin = _rope_tables(jnp.arange(seq), IDX_DIM // 2)
    nsb = seq // tm
    row = lambda i: (i, 0)
    const = lambda i: (0, 0)
    return pl.pallas_call(
        _kiprep_body,
        grid=(m // tm,),
        in_specs=[pl.BlockSpec((tm, LANES), row), pl.BlockSpec((1, LANES), const), pl.BlockSpec((1, LANES), const),
                  pl.BlockSpec((tm, LANES), lambda i: (i % nsb, 0)), pl.BlockSpec((tm, LANES), lambda i: (i % nsb, 0))],
        out_specs=[pl.BlockSpec((tm, LANES), row)] * 2,
        out_shape=[jax.ShapeDtypeStruct((m, LANES), BF16)] * 2,
        compiler_params=_cparams(("parallel",)),
        name="dsa_ki_prep",
    )(raw2, _pad_cols(g.reshape(1, IDX_DIM), LANES), _pad_cols(b.reshape(1, IDX_DIM), LANES), cos, sin)


def _dsa_body(qi_ref, wr_ref, kia_ref, kib_ref, q_ref, k_ref, v_ref, o_ref, key_ref, bias_ref,
              *, tq, ch, topk, wscale):
    i = pl.program_id(1)
    seq = k_ref.shape[1]
    nch = (i * tq + tq + ch - 1) // ch
    nsl = ch // LANES
    wi = wr_ref[0][:, IDX_DIM:IDX_DIM + IDX_HEADS] * wscale
    rowpos = i * tq + lax.broadcasted_iota(jnp.int32, (tq, 1), 0)
    lane_ch = lax.broadcasted_iota(jnp.int32, (tq, ch), 1)
    lane_1 = lax.broadcasted_iota(jnp.int32, (tq, LANES), 1)

    def score_chunk(c, _):
        off = pl.multiple_of(c * ch, ch)
        ka = kia_ref[0, pl.ds(off, ch), :]
        kb = kib_ref[0, pl.ds(off, ch), :]
        sc = jnp.zeros((tq, ch), F32)
        for hpair in range(IDX_HEADS // 2):
            qp = qi_ref[0, :, hpair * LANES:(hpair + 1) * LANES]
            for t, kk in enumerate((ka, kb)):
                h = 2 * hpair + t
                lg = lax.dot_general(qp, kk, _DN_T, preferred_element_type=F32)
                sc = sc + wi[:, h:h + 1] * jnp.maximum(lg, 0.0)
        bits = pltpu.bitcast(sc, jnp.int32)
        bits = jnp.where(bits == INT_MIN, 0, bits)
        key = bits ^ ((bits >> 31) & 0x7FFFFFFF)
        key = jnp.where(off + lane_ch <= rowpos, key, NEG_INF_KEY)
        key_ref[:, pl.ds(off, ch)] = key
        return 0

    lax.fori_loop(0, nch, score_chunk, 0)

    def count(pred):
        def cb(c, acc):
            off = pl.multiple_of(c * ch, ch)
            blk = key_ref[:, pl.ds(off, ch)]
            for s_ in range(nsl):
                acc = acc + jnp.where(pred(blk[:, s_ * LANES:(s_ + 1) * LANES], off + s_ * LANES + lane_1), 1, 0)
            return acc
        acc = lax.fori_loop(0, nch, cb, jnp.zeros((tq, LANES), jnp.int32))
        return jnp.sum(acc.astype(F32), axis=-1, keepdims=True).astype(jnp.int32)

    def bisect_val(t, lo):
        cand = lo + (jnp.int32(1) << (31 - t))
        return jnp.where(count(lambda kv, idx: kv >= cand) >= topk, cand, lo)

    thr = lax.fori_loop(0, 32, bisect_val, jnp.full((tq, LANES), INT_MIN, jnp.int32))
    need = topk - count(lambda kv, idx: kv > thr)
    n_eq = count(lambda kv, idx: kv == thr)
    excess = (n_eq > need) & (thr[:, 0:1] > NEG_INF_KEY)
    any_excess = jnp.max(jnp.where(excess, 1.0, 0.0)) > 0.0

    def tie_path():
        def bisect_idx(t, lo):
            cand = lo + (jnp.int32(1) << (12 - t))
            return jnp.where(count(lambda kv, idx: (kv == thr) & (idx < cand)) < need, cand, lo)
        return lax.fori_loop(0, 13, bisect_idx, jnp.zeros((tq, LANES), jnp.int32))

    jmax = lax.cond(any_excess, tie_path, lambda: jnp.full((tq, LANES), seq, jnp.int32))

    def bias_chunk(c, _):
        off = pl.multiple_of(c * ch, ch)
        blk = key_ref[:, pl.ds(off, ch)]
        for s_ in range(nsl):
            kv = blk[:, s_ * LANES:(s_ + 1) * LANES]
            idx = off + s_ * LANES + lane_1
            keep = ((kv > thr) | ((kv == thr) & (idx <= jmax))) & (idx <= rowpos)
            bias_ref[:, pl.ds(pl.multiple_of(off + s_ * LANES, LANES), LANES)] = jnp.where(keep, 0.0, MASK_VALUE)
        return 0

    lax.fori_loop(0, nch, bias_chunk, 0)

    rows = GQA_GROUP * tq
    for g in range(B_KV_HEADS):
        gs = slice(g * HEAD_DIM, (g + 1) * HEAD_DIM)
        qg = jnp.concatenate(
            [q_ref[0, :, (g * GQA_GROUP + r) * HEAD_DIM:(g * GQA_GROUP + r + 1) * HEAD_DIM]
             for r in range(GQA_GROUP)], axis=0)

        def att(c, carry):
            off = pl.multiple_of(c * ch, ch)
            bias = bias_ref[:, pl.ds(off, ch)]

            def col_fn(s_, sc):
                return sc + jnp.concatenate([bias[:, s_ * LANES:(s_ + 1) * LANES]] * GQA_GROUP, axis=0)

            return _attn_tile(qg, k_ref[0, pl.ds(off, ch), gs], v_ref[0, pl.ds(off, ch), gs], *carry, col_fn)

        m, l, acc = lax.fori_loop(0, nch, att, _attn_init(rows))
        o = _attn_finish(l, acc)
        for r in range(GQA_GROUP):
            h = g * GQA_GROUP + r
            o_ref[0, :, h * HEAD_DIM:(h + 1) * HEAD_DIM] = o[r * tq:(r + 1) * tq].astype(o_ref.dtype)


def _mixer_dsa(xb2, bsz, seq, w_in, idx_g, idx_b, tq=256, ch=512):
    ch = min(ch, seq)
    pos = jnp.arange(seq)
    nqk = D_ATTN + KV_WIDTH
    qk = _proj(xb2, w_in[:, :nqk], pos, half=64, qcols=D_ATTN).reshape(bsz, seq, nqk)
    v = _proj(xb2, w_in[:, nqk:nqk + KV_WIDTH], pos).reshape(bsz, seq, KV_WIDTH)
    o_qi = nqk + KV_WIDTH
    n_qi = IDX_HEADS * IDX_DIM
    qi = _proj(xb2, w_in[:, o_qi:o_qi + n_qi], pos, half=IDX_DIM // 2).reshape(bsz, seq, n_qi)
    raw2 = _proj(xb2, _pad_cols(w_in[:, o_qi + n_qi:], LANES), pos, out_dtype=F32)
    kia, kib = _ki_prep(raw2, idx_g, idx_b, seq)
    topk = min(IDX_TOPK_MAX, seq // 4)
    body = functools.partial(_dsa_body, tq=tq, ch=ch, topk=topk, wscale=IDX_HEADS ** -0.5 * IDX_DIM ** -0.5)
    res = lambda b, i: (b, 0, 0)
    o = pl.pallas_call(
        body,
        grid=(bsz, seq // tq),
        in_specs=[pl.BlockSpec((1, tq, n_qi), lambda b, i: (b, i, 0)),
                  pl.BlockSpec((1, tq, LANES), lambda b, i: (b, i, 0)),
                  pl.BlockSpec((1, seq, LANES), res), pl.BlockSpec((1, seq, LANES), res),
                  pl.BlockSpec((1, tq, D_ATTN), lambda b, i: (b, i, 0)),
                  pl.BlockSpec((1, seq, KV_WIDTH), lambda b, i: (b, 0, D_ATTN // KV_WIDTH)),
                  pl.BlockSpec((1, seq, KV_WIDTH), res)],
        out_specs=pl.BlockSpec((1, tq, D_ATTN), lambda b, i: (b, i, 0)),
        out_shape=jax.ShapeDtypeStruct((bsz, seq, D_ATTN), BF16),
        scratch_shapes=[pltpu.VMEM((tq, seq), jnp.int32), pltpu.VMEM((tq, seq), F32)],
        compiler_params=_cparams(("parallel", "arbitrary")),
        name="dsa_select_attend",
    )(qi, raw2.reshape(bsz, seq, LANES), kia.reshape(bsz, seq, LANES), kib.reshape(bsz, seq, LANES), qk, qk, v)
    return o.reshape(bsz * seq, D_ATTN)


def _trunk(x, layers):
    bsz, seq, d = x.shape
    x2 = x.reshape(bsz * seq, d)
    xb2 = x2
    for kind, p in enumerate(layers):
        w_in = p["w_in"].astype(BF16)
        if kind == 0:
            a = _mixer_dilated(xb2, bsz, seq, w_in)
        elif kind == 1:
            a = _mixer_dsa(xb2, bsz, seq, w_in, p["idx_norm_g"], p["idx_norm_b"])
        elif kind == 2:
            a = _mixer_fox(xb2, bsz, seq, w_in, p["b_forget"])
        else:
            a = _mixer_moba(xb2, bsz, seq, w_in)
        x1, x1p, gates, top_exp = _outln(a, p["w_out"].astype(BF16), x2, p["ln1_g"], p["ln1_b"],
                                         p["router_w"].astype(BF16), p["router_b"])
        x2, xb2 = _moe_layer(x1, x1p, gates[:, :TOP_K], top_exp[:, :TOP_K], p["w_gu"], p["b_gu"], p["w_dn"], p["b_dn"],
                             p["ln2_g"], p["ln2_b"])
    return x2.reshape(bsz, seq, d)


def kernel(x, l0_w_in, l0_w_out, l0_ln1_g, l0_ln1_b, l0_router_w, l0_router_b, l0_w_gu, l0_b_gu, l0_w_dn, l0_b_dn, l0_ln2_g, l0_ln2_b, l1_w_in, l1_idx_norm_g, l1_idx_norm_b, l1_w_out, l1_ln1_g, l1_ln1_b, l1_router_w, l1_router_b, l1_w_gu, l1_b_gu, l1_w_dn, l1_b_dn, l1_ln2_g, l1_ln2_b, l2_w_in, l2_b_forget, l2_w_out, l2_ln1_g, l2_ln1_b, l2_router_w, l2_router_b, l2_w_gu, l2_b_gu, l2_w_dn, l2_b_dn, l2_ln2_g, l2_ln2_b, l3_w_in, l3_w_out, l3_ln1_g, l3_ln1_b, l3_router_w, l3_router_b, l3_w_gu, l3_b_gu, l3_w_dn, l3_b_dn, l3_ln2_g, l3_ln2_b):
    names = ("w_out", "ln1_g", "ln1_b", "router_w", "router_b", "w_gu", "b_gu", "w_dn", "b_dn", "ln2_g", "ln2_b")
    l0 = dict(zip(("w_in",) + names, (l0_w_in, l0_w_out, l0_ln1_g, l0_ln1_b, l0_router_w, l0_router_b,
                                      l0_w_gu, l0_b_gu, l0_w_dn, l0_b_dn, l0_ln2_g, l0_ln2_b)))
    l1 = dict(zip(("w_in", "idx_norm_g", "idx_norm_b") + names,
                  (l1_w_in, l1_idx_norm_g, l1_idx_norm_b, l1_w_out, l1_ln1_g, l1_ln1_b, l1_router_w, l1_router_b,
                   l1_w_gu, l1_b_gu, l1_w_dn, l1_b_dn, l1_ln2_g, l1_ln2_b)))
    l2 = dict(zip(("w_in", "b_forget") + names,
                  (l2_w_in, l2_b_forget, l2_w_out, l2_ln1_g, l2_ln1_b, l2_router_w, l2_router_b,
                   l2_w_gu, l2_b_gu, l2_w_dn, l2_b_dn, l2_ln2_g, l2_ln2_b)))
    l3 = dict(zip(("w_in",) + names, (l3_w_in, l3_w_out, l3_ln1_g, l3_ln1_b, l3_router_w, l3_router_b,
                                      l3_w_gu, l3_b_gu, l3_w_dn, l3_b_dn, l3_ln2_g, l3_ln2_b)))
    return _trunk(x, (l0, l1, l2, l3))
```

```python
import functools

import jax
import jax.numpy as jnp
import numpy as np
from jax import lax
from jax.experimental import pallas as pl
from jax.experimental.pallas import tpu as pltpu

N_HEADS = 16
HEAD_DIM = 128
D_MODEL = 2048
D_ATTN = N_HEADS * HEAD_DIM
ROPE_THETA = 10000.0
MASK_VALUE = -1e30
LN_EPS = 1e-5
DILATED_GROUPS = ((128, 1), (512, 4), (2048, 16))
DIL_MAX = 16
B_KV_HEADS = 4
GQA_GROUP = N_HEADS // B_KV_HEADS
KV_WIDTH = B_KV_HEADS * HEAD_DIM
IDX_HEADS = 16
IDX_DIM = 64
IDX_TOPK_MAX = 256
MOBA_BLOCK = 256
MOBA_TOPK = 3
N_EXPERTS = 32
TOP_K = 4
D_EXPERT = 1024
SWIGLU_ALPHA = 1.702
SWIGLU_LIMIT = 7.0
DEPTH = 4
DEEPNORM_ALPHA = (2 * DEPTH) ** 0.25

LANES = 128
VMEM_LIMIT = 56 * 1024 * 1024
MOE_TM = 256
INT_MIN = -(2 ** 31)
NEG_INF_KEY = -2139095041
LOG2E = 1.4426950408889634
QSCALE = HEAD_DIM ** -0.5 * LOG2E

BF16 = jnp.bfloat16
F32 = jnp.float32
_DN_T = (((1,), (1,)), ((), ()))


def _pack_bf16_pairs(v):
    n = v.shape[1] // 2
    lo = pltpu.bitcast(v[:, :n].astype(BF16).astype(F32), jnp.int32)
    hi = pltpu.bitcast(v[:, n:].astype(BF16).astype(F32), jnp.int32)
    return lax.shift_right_logical(lo, jnp.int32(16)) | (hi & jnp.int32(-65536))


def _unpack_bf16_pairs(pk):
    lo = pltpu.bitcast(lax.shift_left(pk, jnp.int32(16)), F32)
    hi = pltpu.bitcast(pk & jnp.int32(-65536), F32)
    return jnp.concatenate([lo, hi], axis=1)


def _cparams(sem):
    return pltpu.CompilerParams(dimension_semantics=sem, vmem_limit_bytes=VMEM_LIMIT)


def _rope_tables(pos, half):
    inv_freq = ROPE_THETA ** (-jnp.arange(half, dtype=F32) / half)
    ang = pos.astype(F32)[:, None] * inv_freq[None, :]
    cos = jnp.tile(jnp.cos(ang), (1, LANES // half))
    sin = jnp.tile(jnp.concatenate([-jnp.sin(ang), jnp.sin(ang)], -1), (1, LANES // (2 * half)))
    return cos, sin


def _rotate_half(t, half):
    if 2 * half == LANES:
        return pltpu.roll(t, half, 1)
    lane = lax.broadcasted_iota(jnp.int32, t.shape, 1)
    first = (lane % (2 * half)) < half
    return jnp.where(first, pltpu.roll(t, LANES - half, 1), pltpu.roll(t, half, 1))


def _proj_body(*refs, half, q_tiles):
    if half:
        x_ref, w_ref, cos_ref, sin_ref, o_ref = refs
    else:
        x_ref, w_ref, o_ref = refs
    acc = jnp.dot(x_ref[...].astype(BF16), w_ref[...], preferred_element_type=F32)
    if q_tiles:
        acc = acc * jnp.where(pl.program_id(1) < q_tiles, QSCALE, 1.0)
    if not half:
        o_ref[...] = acc.astype(o_ref.dtype)
        return
    cos = cos_ref[...]
    sin = sin_ref[...]
    for c in range(acc.shape[1] // LANES):
        t = acc[:, c * LANES:(c + 1) * LANES]
        o_ref[:, c * LANES:(c + 1) * LANES] = (t * cos + _rotate_half(t, half) * sin).astype(o_ref.dtype)


def _proj(x2, w, pos, half=0, out_dtype=BF16, tm=1024, qcols=0):
    m, k = x2.shape
    n = w.shape[1]
    seq = pos.shape[0]
    tm = min(tm, seq)
    tn = 1024 if n % 1024 == 0 else (512 if n % 512 == 0 else (256 if n % 256 == 0 else LANES))
    assert qcols % tn == 0 and seq % tm == 0
    in_specs = [pl.BlockSpec((tm, k), lambda i, j: (i, 0)),
                pl.BlockSpec((k, tn), lambda i, j: (0, j))]
    args = [x2, w]
    if half:
        cos, sin = _rope_tables(pos, half)
        nsb = seq // tm
        in_specs += [pl.BlockSpec((tm, LANES), lambda i, j: (i % nsb, 0))] * 2
        args += [cos, sin]
    return pl.pallas_call(
        functools.partial(_proj_body, half=half, q_tiles=qcols // tn),
        grid=(m // tm, n // tn),
        in_specs=in_specs,
        out_specs=pl.BlockSpec((tm, tn), lambda i, j: (i, j)),
        out_shape=jax.ShapeDtypeStruct((m, n), out_dtype),
        compiler_params=_cparams(("parallel", "arbitrary")),
        name="proj_rope%d" % half,
    )(*args)


def _layer_norm_rows(z, g, b):
    mu = jnp.mean(z, axis=-1, keepdims=True)
    zc = z - mu
    var = jnp.mean(zc * zc, axis=-1, keepdims=True)
    return zc * lax.rsqrt(var + LN_EPS) * g + b


def _outln_body(a_ref, w_ref, x_ref, g_ref, b_ref, rw_ref, rb_ref, xo_ref, xp_ref, gt_ref, te_ref):
    h = jnp.dot(a_ref[...], w_ref[...], preferred_element_type=F32)
    y = _layer_norm_rows(DEEPNORM_ALPHA * x_ref[...] + h, g_ref[...], b_ref[...])
    xo_ref[...] = y
    xp_ref[...] = _pack_bf16_pairs(y)
    work = jnp.dot(y.astype(BF16), rw_ref[...], preferred_element_type=F32) + rb_ref[...]
    lane = lax.broadcasted_iota(jnp.int32, work.shape, 1).astype(F32)
    vals, idxs = [], []
    for _ in range(TOP_K):
        mx = jnp.max(work, axis=-1, keepdims=True)
        first = jnp.min(jnp.where(work == mx, lane, float(LANES)), axis=-1, keepdims=True)
        vals.append(mx)
        idxs.append(first)
        work = jnp.where(lane == first, -jnp.inf, work)
    es = [jnp.exp(v - vals[0]) for v in vals]
    denom = functools.reduce(jnp.add, es)
    gates = jnp.zeros(work.shape, F32)
    experts = jnp.zeros(work.shape, F32)
    for k in range(TOP_K):
        gates = jnp.where(lane == k, es[k] / denom, gates)
        experts = jnp.where(lane == k, idxs[k], experts)
    gt_ref[...] = gates
    te_ref[...] = experts.astype(jnp.int32)


def _outln(a, w_out, x2, g, b, rw, rb, tm=256):
    m, d = x2.shape
    row = lambda i: (i, 0)
    const = lambda i: (0, 0)
    rw_pad = _pad_cols(rw, LANES)
    rb_pad = jnp.pad(rb.reshape(1, N_EXPERTS), ((0, 0), (0, LANES - N_EXPERTS)), constant_values=-jnp.inf)
    return pl.pallas_call(
        _outln_body,
        grid=(m // tm,),
        in_specs=[pl.BlockSpec((tm, d), row), pl.BlockSpec((d, d), const), pl.BlockSpec((tm, d), row),
                  pl.BlockSpec((1, d), const), pl.BlockSpec((1, d), const),
                  pl.BlockSpec((d, LANES), const), pl.BlockSpec((1, LANES), const)],
        out_specs=[pl.BlockSpec((tm, d), row), pl.BlockSpec((tm, d // 2), row),
                   pl.BlockSpec((tm, LANES), row), pl.BlockSpec((tm, LANES), row)],
        out_shape=[jax.ShapeDtypeStruct((m, d), F32), jax.ShapeDtypeStruct((m, d // 2), jnp.int32),
                   jax.ShapeDtypeStruct((m, LANES), F32), jax.ShapeDtypeStruct((m, LANES), jnp.int32)],
        compiler_params=_cparams(("parallel",)),
        name="outproj_ln_router",
    )(a, w_out, x2, g.reshape(1, d), b.reshape(1, d), rw_pad, rb_pad)


def _moe_body(be_ref, nb_ref, nx_ref, x_ref, wgu_hbm, bgu_ref, wdn_hbm, bdn_ref, o_ref,
              wgu_f, wdn_f, wgu_b, wdn_b, sem):
    i = pl.program_id(0)
    used = i < nb_ref[0]
    expert = be_ref[i]
    new_expert = (i == 0) | (expert != be_ref[jnp.maximum(i - 1, 0)])

    def weight_copies(e):
        return (pltpu.make_async_copy(wgu_hbm.at[e], wgu_f, sem.at[0]),
                pltpu.make_async_copy(wdn_hbm.at[e], wdn_f, sem.at[1]))

    @pl.when(used & (i == 0))
    def _():
        for c in weight_copies(expert):
            c.start()

    @pl.when(used & new_expert)
    def _():
        for c in weight_copies(expert):
            c.wait()
        rows = 64

        def cast(src, dst):
            def body(r, _):
                sl = pl.ds(pl.multiple_of(r * rows, rows), rows)
                dst[sl, :] = src[sl, :].astype(BF16)
                return 0
            lax.fori_loop(0, src.shape[0] // rows, body, 0)

        cast(wgu_f, wgu_b)
        cast(wdn_f, wdn_b)

        @pl.when(nx_ref[i] >= 0)
        def _():
            for c in weight_copies(nx_ref[i]):
                c.start()

    @pl.when(used)
    def _():
        h = jnp.dot(_unpack_bf16_pairs(x_ref[...]).astype(BF16), wgu_b[...], preferred_element_type=F32) + bgu_ref[0]
        glu = jnp.minimum(h[:, :D_EXPERT], SWIGLU_LIMIT)
        lin = jnp.clip(h[:, D_EXPERT:], -SWIGLU_LIMIT, SWIGLU_LIMIT)
        act = glu * jax.nn.sigmoid(SWIGLU_ALPHA * glu) * (lin + 1.0)
        y = jnp.dot(act.astype(BF16), wdn_b[...], preferred_element_type=F32) + bdn_ref[0]
        o_ref[...] = _pack_bf16_pairs(y)

    @pl.when(jnp.logical_not(used))
    def _():
        o_ref[...] = jnp.zeros_like(o_ref)


def _moe_experts(x_rows, block_exp, n_used, next_exp, w_gu, b_gu, w_dn, b_dn):
    n_rows = x_rows.shape[0]
    d = 2 * x_rows.shape[1]
    n_blocks = n_rows // MOE_TM
    grid_spec = pltpu.PrefetchScalarGridSpec(
        num_scalar_prefetch=3,
        grid=(n_blocks,),
        in_specs=[pl.BlockSpec((MOE_TM, d // 2), lambda i, be, nb, nx: (i, 0)),
                  pl.BlockSpec(memory_space=pl.ANY),
                  pl.BlockSpec((1, 1, 2 * D_EXPERT), lambda i, be, nb, nx: (be[i], 0, 0)),
                  pl.BlockSpec(memory_space=pl.ANY),
                  pl.BlockSpec((1, 1, d), lambda i, be, nb, nx: (be[i], 0, 0))],
        out_specs=pl.BlockSpec((MOE_TM, d // 2), lambda i, be, nb, nx: (i, 0)),
        scratch_shapes=[pltpu.VMEM((d, 2 * D_EXPERT), F32), pltpu.VMEM((D_EXPERT, d), F32),
                        pltpu.VMEM((d, 2 * D_EXPERT), BF16), pltpu.VMEM((D_EXPERT, d), BF16),
                        pltpu.SemaphoreType.DMA((2,))],
    )
    return pl.pallas_call(
        _moe_body,
        grid_spec=grid_spec,
        out_shape=jax.ShapeDtypeStruct((n_rows, d // 2), jnp.int32),
        compiler_params=_cparams(("arbitrary",)),
        name="moe_experts",
    )(block_exp, n_used, next_exp, x_rows, w_gu, b_gu.reshape(N_EXPERTS, 1, -1), w_dn,
      b_dn.reshape(N_EXPERTS, 1, -1))


def _combine_body(*refs):
    y_refs = refs[:TOP_K]
    gt_ref, x_ref, g_ref, b_ref, xo_ref, xb_ref = refs[TOP_K:]
    gt = gt_ref[...]
    y = gt[:, 0:1] * _unpack_bf16_pairs(y_refs[0][...])
    for k in range(1, TOP_K):
        y = y + gt[:, k:k + 1] * _unpack_bf16_pairs(y_refs[k][...])
    out = _layer_norm_rows(DEEPNORM_ALPHA * x_ref[...] + y, g_ref[...], b_ref[...])
    xo_ref[...] = out
    xb_ref[...] = out.astype(BF16)


def _combine_ln(y4, gates, x2, g, b, tm=256):
    m, d = x2.shape
    nb = m // tm
    row = lambda i: (i, 0)
    const = lambda i: (0, 0)
    y_specs = [pl.BlockSpec((tm, d // 2), functools.partial(lambda i, k: (k * nb + i, 0), k=k)) for k in range(TOP_K)]
    return pl.pallas_call(
        _combine_body,
        grid=(nb,),
        in_specs=y_specs + [pl.BlockSpec((tm, TOP_K), row),
                            pl.BlockSpec((tm, d), row), pl.BlockSpec((1, d), const), pl.BlockSpec((1, d), const)],
        out_specs=[pl.BlockSpec((tm, d), row), pl.BlockSpec((tm, d), row)],
        out_shape=[jax.ShapeDtypeStruct((m, d), F32), jax.ShapeDtypeStruct((m, d), BF16)],
        compiler_params=_cparams(("parallel",)),
        name="moe_combine_ln",
    )(*([y4] * TOP_K), gates, x2, g.reshape(1, d), b.reshape(1, d))


def _moe_layer(x1, x1p, gates, top_exp, w_gu, b_gu, w_dn, b_dn, g, b):
    n_tok, d = x1.shape
    n_assign = n_tok * TOP_K
    flat_exp = top_exp.reshape(-1)
    experts = jnp.arange(N_EXPERTS, dtype=jnp.int32)
    counts = jnp.sum((flat_exp[:, None] == experts[None, :]).astype(jnp.int32), axis=0)
    padded = (counts + MOE_TM - 1) // MOE_TM * MOE_TM
    pad_end = jnp.cumsum(padded)
    n_blocks = n_assign // MOE_TM + N_EXPERTS
    n_rows = n_blocks * MOE_TM
    block_start = jnp.arange(n_blocks, dtype=jnp.int32) * MOE_TM
    block_exp = jnp.minimum(jnp.sum((pad_end[None, :] <= block_start[:, None]).astype(jnp.int32), axis=1),
                            N_EXPERTS - 1).astype(jnp.int32)
    n_used = (pad_end[-1] // MOE_TM).astype(jnp.int32).reshape(1)
    later = (experts[None, :] > experts[:, None]) & (counts[None, :] > 0)
    nxt = jnp.min(jnp.where(later, experts[None, :], N_EXPERTS), axis=1)
    nxt = jnp.where(nxt == N_EXPERTS, -1, nxt)
    next_exp = jnp.sum(jnp.where(block_exp[:, None] == experts[None, :], nxt[None, :], 0), axis=1).astype(jnp.int32)
    idx_bits = (n_assign + N_EXPERTS * MOE_TM - 1).bit_length()
    a_idx = jnp.arange(n_assign, dtype=jnp.int32)
    f_idx = n_assign + jnp.arange(N_EXPERTS * MOE_TM, dtype=jnp.int32)
    f_num = jnp.arange(MOE_TM, dtype=jnp.int32)[None, :]
    f_major = jnp.where(f_num < (padded - counts)[:, None], 2 * experts[:, None] + 1, 2 * N_EXPERTS).reshape(-1)
    keys = jnp.concatenate([(2 * flat_exp << idx_bits) | a_idx, (f_major << idx_bits) | f_idx])
    src = jnp.sort(keys) & ((1 << idx_bits) - 1)
    row_tok = jnp.where(src < n_assign, src // TOP_K, 0)[:n_rows]
    _, inv = lax.sort_key_val(src, jnp.arange(src.shape[0], dtype=jnp.int32))
    pos = inv[:n_assign]
    x_rows = x1p[row_tok]
    y_rows = _moe_experts(x_rows, block_exp, n_used, next_exp, w_gu, b_gu, w_dn, b_dn)
    y4 = y_rows[pos.reshape(n_tok, TOP_K).T.reshape(-1)]
    return _combine_ln(y4, gates, x1, g, b)


def _attn_tile(q, k, v, m, l, acc, col_fn):
    s = lax.dot_general(q, k, _DN_T, preferred_element_type=F32)
    cols = [col_fn(c, s[:, c * LANES:(c + 1) * LANES]) for c in range(s.shape[1] // LANES)]
    m_new = jnp.maximum(m, jnp.max(functools.reduce(jnp.maximum, cols), axis=-1, keepdims=True))
    alpha = jnp.exp2(m - m_new)
    ps = [jnp.exp2(c - m_new) for c in cols]
    l = alpha * l + functools.reduce(jnp.add, ps)
    p = jnp.concatenate([x.astype(BF16) for x in ps], axis=1)
    acc = alpha * acc + jnp.dot(p, v, preferred_element_type=F32)
    return m_new, l, acc


def _attn_init(rows):
    return (jnp.full((rows, LANES), -jnp.inf, F32), jnp.zeros((rows, LANES), F32),
            jnp.zeros((rows, HEAD_DIM), F32))


def _attn_finish(l, acc):
    return acc / jnp.sum(l, axis=-1, keepdims=True)


def _band_body(*refs, has_prev, is_last, planes, rows):
    if has_prev:
        q_ref, kc_ref, kp_ref, vc_ref, vp_ref, oi_ref, li_ref = refs[:7]
        outs = refs[7:]
    else:
        q_ref, kc_ref, kp_ref, vc_ref, vp_ref = refs[:5]
        outs = refs[5:]
    o_ref = outs[0]
    j = pl.program_id(2)
    w = planes * rows

    def local(t):
        return (t % rows) * planes + t // rows

    tq = lax.broadcasted_iota(jnp.int32, (w, 2 * w), 0)
    tk = lax.broadcasted_iota(jnp.int32, (w, 2 * w), 1)
    cur = tk >= w
    dist = (w + local(tq)) - (local(tk % w) + jnp.where(cur, w, 0))
    mask = (dist >= 0) & (dist <= LANES) & (cur | (j > 0))

    def blk(ref, hs):
        return ref[0, :, 0, :, hs].reshape(w, HEAD_DIM)

    group = N_HEADS if w <= LANES else N_HEADS // 2
    for h0 in range(0, N_HEADS, group):
        hss = [slice(h * HEAD_DIM, (h + 1) * HEAD_DIM) for h in range(h0, h0 + group)]
        vcats = [jnp.concatenate([blk(vp_ref, hs), blk(vc_ref, hs)], axis=0) for hs in hss]
        s = jnp.stack([lax.dot_general(blk(q_ref, hs),
                                       jnp.concatenate([blk(kp_ref, hs), blk(kc_ref, hs)], axis=0),
                                       _DN_T, preferred_element_type=F32) for hs in hss])
        s = jnp.where(mask[None], s, MASK_VALUE)
        m = jnp.max(s, axis=-1, keepdims=True)
        p = jnp.exp2(s - m)
        l = jnp.sum(p, axis=-1, keepdims=True)
        pn = (p / l).astype(BF16)
        o = jnp.stack([jnp.dot(pn[g], vcats[g], preferred_element_type=F32) for g in range(group)])
        lse = jnp.broadcast_to(m + jnp.log2(l), o.shape)
        if has_prev:
            lse_prev = jnp.stack([blk(li_ref, hs) for hs in hss])
            o_prev = jnp.stack([blk(oi_ref, hs) for hs in hss])
            mx = jnp.maximum(lse, lse_prev)
            e_new = jnp.exp2(lse - mx)
            e_old = jnp.exp2(lse_prev - mx)
            tot = e_new + e_old
            o = (e_new * o + e_old * o_prev) / tot
            lse = mx + jnp.log2(tot)
        for g, hs in enumerate(hss):
            o_ref[0, :, 0, :, hs] = o[g].astype(o_ref.dtype).reshape(planes, rows, HEAD_DIM)
            if not is_last:
                outs[1][0, :, 0, :, hs] = lse[g].reshape(planes, rows, HEAD_DIM)


def _band_stage(qk, v, prev, dil, is_last):
    bsz, _, sub, _ = v.shape
    planes = DIL_MAX // dil
    groups = DIL_MAX // planes
    rows = (LANES if planes < DIL_MAX else 2 * LANES) // planes
    view = lambda a: a.reshape(bsz, planes, groups, sub, a.shape[-1])
    nb = sub // rows
    blk = lambda col, prevblk: pl.BlockSpec(
        (1, planes, 1, rows, D_ATTN),
        (lambda b, g, j: (b, 0, g, jnp.maximum(j - 1, 0), col)) if prevblk else (lambda b, g, j: (b, 0, g, j, col)))
    in_specs = [blk(0, False), blk(1, False), blk(1, True), blk(0, False), blk(0, True)]
    args = [view(qk)] * 3 + [view(v)] * 2
    o_spec = blk(0, False)
    if prev is not None:
        in_specs += [o_spec, o_spec]
        args += [view(prev[0]), view(prev[1])]
    out_specs = [o_spec] if is_last else [o_spec, o_spec]
    out_shape = [jax.ShapeDtypeStruct((bsz, planes, groups, sub, D_ATTN), BF16 if is_last else F32)]
    if not is_last:
        out_shape.append(jax.ShapeDtypeStruct((bsz, planes, groups, sub, D_ATTN), F32))
    outs = pl.pallas_call(
        functools.partial(_band_body, has_prev=prev is not None, is_last=is_last, planes=planes, rows=rows),
        grid=(bsz, groups, nb),
        in_specs=in_specs, out_specs=out_specs, out_shape=out_shape,
        compiler_params=_cparams(("parallel", "parallel", "arbitrary")),
        name="dilated_band_d%d" % dil,
    )(*args)
    unview = lambda a: a.reshape(bsz, DIL_MAX, sub, a.shape[-1])
    if is_last:
        return unview(outs[0])
    return unview(outs[0]), unview(outs[1])


def _mixer_dilated(xb2, bsz, seq, w_in):
    d = xb2.shape[1]
    sub = seq // DIL_MAX
    assert seq % (DIL_MAX * 2 * LANES) == 0
    xp = xb2.reshape(bsz, sub, DIL_MAX, d).transpose(0, 2, 1, 3).reshape(bsz * seq, d)
    pos = (jnp.arange(sub)[None, :] * DIL_MAX + jnp.arange(DIL_MAX)[:, None]).reshape(-1)
    qk = _proj(xp, w_in[:, :2 * D_ATTN], pos, half=64, qcols=D_ATTN).reshape(bsz, DIL_MAX, sub, 2 * D_ATTN)
    v = _proj(xp, w_in[:, 2 * D_ATTN:], pos).reshape(bsz, DIL_MAX, sub, D_ATTN)
    prev = None
    for g, (window, dil) in enumerate(DILATED_GROUPS):
        assert window // dil == LANES and DIL_MAX % dil == 0
        prev = _band_stage(qk, v, prev, dil, g == len(DILATED_GROUPS) - 1)
    return prev.transpose(0, 2, 1, 3).reshape(bsz * seq, D_ATTN)


def _flash_body(*refs, kind, hp, tq):
    if kind == "fox":
        q_ref, k_ref, v_ref, cq_ref, ck_ref, o_ref = refs
    else:
        q_ref, k_ref, v_ref, o_ref, kmean_ref = refs
    hg = pl.program_id(1)
    i = pl.program_id(2)
    seq = k_ref.shape[1]
    nsl = tq // LANES
    rel = lax.broadcasted_iota(jnp.int32, (tq, LANES), 0) - lax.broadcasted_iota(jnp.int32, (tq, LANES), 1)
    causal = [rel >= c * LANES for c in range(nsl)]
    heads = []

    if kind == "moba":
        nblk = seq // MOBA_BLOCK
        spb = MOBA_BLOCK // LANES
        bpt = tq // MOBA_BLOCK

        @pl.when(i == 0)
        def _():
            kmean_ref[...] = jnp.zeros_like(kmean_ref)
            for hh in range(hp):
                hs = slice(hh * HEAD_DIM, (hh + 1) * HEAD_DIM)
                for n in range(nblk):
                    kb = k_ref[0, n * MOBA_BLOCK:(n + 1) * MOBA_BLOCK, hs].astype(F32)
                    kmean_ref[hh, n:n + 1, :] = jnp.sum(kb, axis=0, keepdims=True) / MOBA_BLOCK

        row_blk = lax.broadcasted_iota(jnp.int32, (tq, 1), 0) // MOBA_BLOCK
        own = (i * bpt + row_blk).astype(F32)
        blk_id = lax.broadcasted_iota(jnp.int32, (tq, LANES), 1).astype(F32)

    for hh in range(hp):
        hs = slice(hh * HEAD_DIM, (hh + 1) * HEAD_DIM)
        q = q_ref[0, :, hs]
        if kind == "fox":
            cq = jnp.broadcast_to(cq_ref[0, 0, :, hh:hh + 1], (tq, LANES))
            heads.append((hs, q, cq, hg * hp + hh))
        else:
            gate = lax.dot_general(q, kmean_ref[hh].astype(BF16), _DN_T, preferred_element_type=F32)
            gate = jnp.where(blk_id < own, gate, -jnp.inf)
            sel = jnp.zeros(gate.shape, F32)
            for _ in range(min(MOBA_TOPK, nblk)):
                mx = jnp.max(gate, axis=-1, keepdims=True)
                first = jnp.min(jnp.where(gate == mx, blk_id, float(LANES)), axis=-1, keepdims=True)
                pick = blk_id == first
                sel = jnp.where(pick & (first < own), 1.0, sel)
                gate = jnp.where(pick, -jnp.inf, gate)
            heads.append((hs, q, sel, None))

    def step(j, carry, diag):
        off = pl.multiple_of(j * tq, tq)
        out = []
        for (hs, q, aux, head), (m, l, acc) in zip(heads, carry):
            k = k_ref[0, pl.ds(off, tq), hs]
            v = v_ref[0, pl.ds(off, tq), hs]
            if kind == "fox":
                ck = ck_ref[0, pl.ds(head, 1), pl.ds(off, tq)]

                def col_fn(c, sc):
                    sc = sc + (aux - ck[:, c * LANES:(c + 1) * LANES])
                    return jnp.where(causal[c], sc, MASK_VALUE) if diag else sc
            else:
                rowsel = [jnp.max(jnp.where(blk_id == (j * bpt + n).astype(F32), aux, 0.0),
                                  axis=-1, keepdims=True) > 0.0 for n in range(bpt)]

                def col_fn(c, sc):
                    keep = rowsel[c // spb]
                    if diag:
                        keep = keep | ((row_blk == c // spb) & causal[c])
                    return jnp.where(keep, sc, MASK_VALUE)
            out.append(_attn_tile(q, k, v, m, l, acc, col_fn))
        return tuple(out)

    carry = tuple(_attn_init(tq) for _ in range(hp))
    carry = lax.fori_loop(0, i, functools.partial(step, diag=False), carry)
    carry = step(i, carry, True)
    for (hs, _, _, _), (m, l, acc) in zip(heads, carry):
        o_ref[0, :, hs] = _attn_finish(l, acc).astype(o_ref.dtype)


def _flash(kind, q_arr, q_off, k_arr, k_off, v_arr, v_off, extra, hp, tq):
    bsz, seq, _ = q_arr.shape
    wid = hp * HEAD_DIM
    hgs = N_HEADS // hp
    qo, ko, vo = q_off // hp, k_off // hp, v_off // hp
    in_specs = [pl.BlockSpec((1, tq, wid), lambda b, h, i: (b, i, qo + h)),
                pl.BlockSpec((1, seq, wid), lambda b, h, i: (b, 0, ko + h)),
                pl.BlockSpec((1, seq, wid), lambda b, h, i: (b, 0, vo + h))]
    args = [q_arr, k_arr, v_arr]
    scratch = []
    if kind == "fox":
        cum_col, cum_row = extra
        in_specs += [pl.BlockSpec((1, 1, tq, hp), lambda b, h, i: (b, h, i, 0)),
                     pl.BlockSpec((1, N_HEADS, seq), lambda b, h, i: (b, 0, 0))]
        args += [cum_col, cum_row]
    else:
        assert seq // MOBA_BLOCK <= LANES and tq % MOBA_BLOCK == 0
        scratch = [pltpu.VMEM((hp, LANES, HEAD_DIM), F32)]
    return pl.pallas_call(
        functools.partial(_flash_body, kind=kind, hp=hp, tq=tq),
        grid=(bsz, hgs, seq // tq),
        in_specs=in_specs,
        out_specs=pl.BlockSpec((1, tq, wid), lambda b, h, i: (b, i, h)),
        out_shape=jax.ShapeDtypeStruct((bsz, seq, D_ATTN), BF16),
        scratch_shapes=scratch,
        compiler_params=_cparams(("parallel", "parallel", "arbitrary")),
        name="flash_" + kind,
    )(*args)


def _cumsum_body(f_ref, b_ref, o_ref, carry_ref):
    j = pl.program_id(1)

    @pl.when(j == 0)
    def _():
        carry_ref[...] = jnp.zeros_like(carry_ref)

    ts = f_ref.shape[1]
    logf = jax.nn.log_sigmoid(f_ref[0] + b_ref[...])
    tri = (lax.broadcasted_iota(jnp.int32, (ts, ts), 0) >= lax.broadcasted_iota(jnp.int32, (ts, ts), 1)).astype(F32)
    cum = jnp.dot(tri, logf, precision=lax.Precision.HIGHEST, preferred_element_type=F32) + carry_ref[...]
    o_ref[0] = cum * LOG2E
    carry_ref[...] = cum[ts - 1:ts, :]


def _forget_cumsum(f_raw, b_pad, ts=256):
    bsz, seq, _ = f_raw.shape
    return pl.pallas_call(
        _cumsum_body,
        grid=(bsz, seq // ts),
        in_specs=[pl.BlockSpec((1, ts, LANES), lambda b, j: (b, j, 0)),
                  pl.BlockSpec((1, LANES), lambda b, j: (0, 0))],
        out_specs=pl.BlockSpec((1, ts, LANES), lambda b, j: (b, j, 0)),
        out_shape=jax.ShapeDtypeStruct((bsz, seq, LANES), F32),
        scratch_shapes=[pltpu.VMEM((1, LANES), F32)],
        compiler_params=_cparams(("parallel", "arbitrary")),
        name="forget_cumsum",
    )(f_raw, b_pad)


def _pad_cols(w, n):
    return jnp.pad(w, ((0, 0), (0, n - w.shape[1])))


def _mixer_fox(xb2, bsz, seq, w_in, b_forget, hp=4, tq=512):
    tq = min(tq, seq)
    pos = jnp.arange(seq)
    qkv = _proj(xb2, w_in[:, :3 * D_ATTN], pos, qcols=D_ATTN).reshape(bsz, seq, 3 * D_ATTN)
    f_raw = _proj(xb2, _pad_cols(w_in[:, 3 * D_ATTN:], LANES), pos, out_dtype=F32).reshape(bsz, seq, LANES)
    cum = _forget_cumsum(f_raw, _pad_cols(b_forget.reshape(1, N_HEADS), LANES))[..., :N_HEADS]
    cum_row = cum.transpose(0, 2, 1)
    cum_col = cum.reshape(bsz, seq, N_HEADS // hp, hp).transpose(0, 2, 1, 3)
    o = _flash("fox", qkv, 0, qkv, N_HEADS, qkv, 2 * N_HEADS, (cum_col, cum_row), hp, tq)
    return o.reshape(bsz * seq, D_ATTN)


def _mixer_moba(xb2, bsz, seq, w_in, hp=4, tq=512):
    tq = min(tq, seq)
    pos = jnp.arange(seq)
    qk = _proj(xb2, w_in[:, :2 * D_ATTN], pos, half=64, qcols=D_ATTN).reshape(bsz, seq, 2 * D_ATTN)
    v = _proj(xb2, w_in[:, 2 * D_ATTN:], pos).reshape(bsz, seq, D_ATTN)
    assert seq % MOBA_BLOCK == 0
    o = _flash("moba", qk, 0, qk, N_HEADS, v, 0, None, hp, tq)
    return o.reshape(bsz * seq, D_ATTN)


def _kiprep_body(r_ref, g_ref, b_ref, cos_ref, sin_ref, a_ref, b2_ref):
    x = r_ref[...]
    lane = lax.broadcasted_iota(jnp.int32, x.shape, 1)
    inside = lane < IDX_DIM
    mu = jnp.sum(jnp.where(inside, x, 0.0), axis=-1, keepdims=True) / IDX_DIM
    xc = jnp.where(inside, x - mu, 0.0)
    var = jnp.sum(xc * xc, axis=-1, keepdims=True) / IDX_DIM
    y = xc * lax.rsqrt(var + LN_EPS) * g_ref[...] + b_ref[...]
    y = y * cos_ref[...] + _rotate_half(y, IDX_DIM // 2) * sin_ref[...]
    y = jnp.where(inside, y, 0.0)
    a_ref[...] = y.astype(BF16)
    b2_ref[...] = pltpu.roll(y, IDX_DIM, 1).astype(BF16)


def _ki_prep(raw2, g, b, seq, tm=512):
    m = raw2.shape[0]
    tm = min(tm, seq)
    cos, sin = _rope_tables(jnp.arange(seq), IDX_DIM // 2)
    nsb = seq // tm
    row = lambda i: (i, 0)
    const = lambda i: (0, 0)
    return pl.pallas_call(
        _kiprep_body,
        grid=(m // tm,),
        in_specs=[pl.BlockSpec((tm, LANES), row), pl.BlockSpec((1, LANES), const), pl.BlockSpec((1, LANES), const),
                  pl.BlockSpec((tm, LANES), lambda i: (i % nsb, 0)), pl.BlockSpec((tm, LANES), lambda i: (i % nsb, 0))],
        out_specs=[pl.BlockSpec((tm, LANES), row)] * 2,
        out_shape=[jax.ShapeDtypeStruct((m, LANES), BF16)] * 2,
        compiler_params=_cparams(("parallel",)),
        name="dsa_ki_prep",
    )(raw2, _pad_cols(g.reshape(1, IDX_DIM), LANES), _pad_cols(b.reshape(1, IDX_DIM), LANES), cos, sin)


def _dsa_body(qi_ref, wr_ref, kia_ref, kib_ref, q_ref, k_ref, v_ref, o_ref, key_ref, bias_ref,
              *, tq, ch, topk, wscale):
    i = pl.program_id(1)
    seq = k_ref.shape[1]
    nch = (i * tq + tq + ch - 1) // ch
    nsl = ch // LANES
    wi = wr_ref[0][:, IDX_DIM:IDX_DIM + IDX_HEADS] * wscale
    rowpos = i * tq + lax.broadcasted_iota(jnp.int32, (tq, 1), 0)
    lane_ch = lax.broadcasted_iota(jnp.int32, (tq, ch), 1)
    lane_1 = lax.broadcasted_iota(jnp.int32, (tq, LANES), 1)

    def score_chunk(c, _):
        off = pl.multiple_of(c * ch, ch)
        ka = kia_ref[0, pl.ds(off, ch), :]
        kb = kib_ref[0, pl.ds(off, ch), :]
        sc = jnp.zeros((tq, ch), F32)
        for hpair in range(IDX_HEADS // 2):
            qp = qi_ref[0, :, hpair * LANES:(hpair + 1) * LANES]
            for t, kk in enumerate((ka, kb)):
                h = 2 * hpair + t
                lg = lax.dot_general(qp, kk, _DN_T, preferred_element_type=F32)
                sc = sc + wi[:, h:h + 1] * jnp.maximum(lg, 0.0)
        bits = pltpu.bitcast(sc, jnp.int32)
        bits = jnp.where(bits == INT_MIN, 0, bits)
        key = bits ^ ((bits >> 31) & 0x7FFFFFFF)
        key = jnp.where(off + lane_ch <= rowpos, key, NEG_INF_KEY)
        key_ref[:, pl.ds(off, ch)] = key
        return 0

    lax.fori_loop(0, nch, score_chunk, 0)

    def count(pred):
        def cb(c, acc):
            off = pl.multiple_of(c * ch, ch)
            blk = key_ref[:, pl.ds(off, ch)]
            for s_ in range(nsl):
                acc = acc + jnp.where(pred(blk[:, s_ * LANES:(s_ + 1) * LANES], off + s_ * LANES + lane_1), 1, 0)
            return acc
        acc = lax.fori_loop(0, nch, cb, jnp.zeros((tq, LANES), jnp.int32))
        return jnp.sum(acc.astype(F32), axis=-1, keepdims=True).astype(jnp.int32)

    def bisect_val(t, lo):
        cand = lo + (jnp.int32(1) << (31 - t))
        return jnp.where(count(lambda kv, idx: kv >= cand) >= topk, cand, lo)

    thr = lax.fori_loop(0, 32, bisect_val, jnp.full((tq, LANES), INT_MIN, jnp.int32))
    need = topk - count(lambda kv, idx: kv > thr)
    n_eq = count(lambda kv, idx: kv == thr)
    excess = (n_eq > need) & (thr[:, 0:1] > NEG_INF_KEY)
    any_excess = jnp.max(jnp.where(excess, 1.0, 0.0)) > 0.0

    def tie_path():
        def bisect_idx(t, lo):
            cand = lo + (jnp.int32(1) << (12 - t))
            return jnp.where(count(lambda kv, idx: (kv == thr) & (idx < cand)) < need, cand, lo)
        return lax.fori_loop(0, 13, bisect_idx, jnp.zeros((tq, LANES), jnp.int32))

    jmax = lax.cond(any_excess, tie_path, lambda: jnp.full((tq, LANES), seq, jnp.int32))

    def bias_chunk(c, _):
        off = pl.multiple_of(c * ch, ch)
        blk = key_ref[:, pl.ds(off, ch)]
        for s_ in range(nsl):
            kv = blk[:, s_ * LANES:(s_ + 1) * LANES]
            idx = off + s_ * LANES + lane_1
            keep = ((kv > thr) | ((kv == thr) & (idx <= jmax))) & (idx <= rowpos)
            bias_ref[:, pl.ds(pl.multiple_of(off + s_ * LANES, LANES), LANES)] = jnp.where(keep, 0.0, MASK_VALUE)
        return 0

    lax.fori_loop(0, nch, bias_chunk, 0)

    rows = GQA_GROUP * tq
    for g in range(B_KV_HEADS):
        gs = slice(g * HEAD_DIM, (g + 1) * HEAD_DIM)
        qg = jnp.concatenate(
            [q_ref[0, :, (g * GQA_GROUP + r) * HEAD_DIM:(g * GQA_GROUP + r + 1) * HEAD_DIM]
             for r in range(GQA_GROUP)], axis=0)

        def att(c, carry):
            off = pl.multiple_of(c * ch, ch)
            bias = bias_ref[:, pl.ds(off, ch)]

            def col_fn(s_, sc):
                return sc + jnp.concatenate([bias[:, s_ * LANES:(s_ + 1) * LANES]] * GQA_GROUP, axis=0)

            return _attn_tile(qg, k_ref[0, pl.ds(off, ch), gs], v_ref[0, pl.ds(off, ch), gs], *carry, col_fn)

        m, l, acc = lax.fori_loop(0, nch, att, _attn_init(rows))
        o = _attn_finish(l, acc)
        for r in range(GQA_GROUP):
            h = g * GQA_GROUP + r
            o_ref[0, :, h * HEAD_DIM:(h + 1) * HEAD_DIM] = o[r * tq:(r + 1) * tq].astype(o_ref.dtype)


def _mixer_dsa(xb2, bsz, seq, w_in, idx_g, idx_b, tq=256, ch=1024):
    ch = min(ch, seq)
    pos = jnp.arange(seq)
    nqk = D_ATTN + KV_WIDTH
    qk = _proj(xb2, w_in[:, :nqk], pos, half=64, qcols=D_ATTN).reshape(bsz, seq, nqk)
    v = _proj(xb2, w_in[:, nqk:nqk + KV_WIDTH], pos).reshape(bsz, seq, KV_WIDTH)
    o_qi = nqk + KV_WIDTH
    n_qi = IDX_HEADS * IDX_DIM
    qi = _proj(xb2, w_in[:, o_qi:o_qi + n_qi], pos, half=IDX_DIM // 2).reshape(bsz, seq, n_qi)
    raw2 = _proj(xb2, _pad_cols(w_in[:, o_qi + n_qi:], LANES), pos, out_dtype=F32)
    kia, kib = _ki_prep(raw2, idx_g, idx_b, seq)
    topk = min(IDX_TOPK_MAX, seq // 4)
    body = functools.partial(_dsa_body, tq=tq, ch=ch, topk=topk, wscale=IDX_HEADS ** -0.5 * IDX_DIM ** -0.5)
    res = lambda b, i: (b, 0, 0)
    o = pl.pallas_call(
        body,
        grid=(bsz, seq // tq),
        in_specs=[pl.BlockSpec((1, tq, n_qi), lambda b, i: (b, i, 0)),
                  pl.BlockSpec((1, tq, LANES), lambda b, i: (b, i, 0)),
                  pl.BlockSpec((1, seq, LANES), res), pl.BlockSpec((1, seq, LANES), res),
                  pl.BlockSpec((1, tq, D_ATTN), lambda b, i: (b, i, 0)),
                  pl.BlockSpec((1, seq, KV_WIDTH), lambda b, i: (b, 0, D_ATTN // KV_WIDTH)),
                  pl.BlockSpec((1, seq, KV_WIDTH), res)],
        out_specs=pl.BlockSpec((1, tq, D_ATTN), lambda b, i: (b, i, 0)),
        out_shape=jax.ShapeDtypeStruct((bsz, seq, D_ATTN), BF16),
        scratch_shapes=[pltpu.VMEM((tq, seq), jnp.int32), pltpu.VMEM((tq, seq), F32)],
        compiler_params=_cparams(("parallel", "arbitrary")),
        name="dsa_select_attend",
    )(qi, raw2.reshape(bsz, seq, LANES), kia.reshape(bsz, seq, LANES), kib.reshape(bsz, seq, LANES), qk, qk, v)
    return o.reshape(bsz * seq, D_ATTN)


def _trunk(x, layers):
    bsz, seq, d = x.shape
    x2 = x.reshape(bsz * seq, d)
    xb2 = x2
    for kind, p in enumerate(layers):
        w_in = p["w_in"].astype(BF16)
        if kind == 0:
            a = _mixer_dilated(xb2, bsz, seq, w_in)
        elif kind == 1:
            a = _mixer_dsa(xb2, bsz, seq, w_in, p["idx_norm_g"], p["idx_norm_b"])
        elif kind == 2:
            a = _mixer_fox(xb2, bsz, seq, w_in, p["b_forget"])
        else:
            a = _mixer_moba(xb2, bsz, seq, w_in)
        x1, x1p, gates, top_exp = _outln(a, p["w_out"].astype(BF16), x2, p["ln1_g"], p["ln1_b"],
                                         p["router_w"].astype(BF16), p["router_b"])
        x2, xb2 = _moe_layer(x1, x1p, gates[:, :TOP_K], top_exp[:, :TOP_K], p["w_gu"], p["b_gu"], p["w_dn"], p["b_dn"],
                             p["ln2_g"], p["ln2_b"])
    return x2.reshape(bsz, seq, d)


def kernel(x, l0_w_in, l0_w_out, l0_ln1_g, l0_ln1_b, l0_router_w, l0_router_b, l0_w_gu, l0_b_gu, l0_w_dn, l0_b_dn, l0_ln2_g, l0_ln2_b, l1_w_in, l1_idx_norm_g, l1_idx_norm_b, l1_w_out, l1_ln1_g, l1_ln1_b, l1_router_w, l1_router_b, l1_w_gu, l1_b_gu, l1_w_dn, l1_b_dn, l1_ln2_g, l1_ln2_b, l2_w_in, l2_b_forget, l2_w_out, l2_ln1_g, l2_ln1_b, l2_router_w, l2_router_b, l2_w_gu, l2_b_gu, l2_w_dn, l2_b_dn, l2_ln2_g, l2_ln2_b, l3_w_in, l3_w_out, l3_ln1_g, l3_ln1_b, l3_router_w, l3_router_b, l3_w_gu, l3_b_gu, l3_w_dn, l3_b_dn, l3_ln2_g, l3_ln2_b):
    names = ("w_out", "ln1_g", "ln1_b", "router_w", "router_b", "w_gu", "b_gu", "w_dn", "b_dn", "ln2_g", "ln2_b")
    l0 = dict(zip(("w_in",) + names, (l0_w_in, l0_w_out, l0_ln1_g, l0_ln1_b, l0_router_w, l0_router_b,
                                      l0_w_gu, l0_b_gu, l0_w_dn, l0_b_dn, l0_ln2_g, l0_ln2_b)))
    l1 = dict(zip(("w_in", "idx_norm_g", "idx_norm_b") + names,
                  (l1_w_in, l1_idx_norm_g, l1_idx_norm_b, l1_w_out, l1_ln1_g, l1_ln1_b, l1_router_w, l1_router_b,
                   l1_w_gu, l1_b_gu, l1_w_dn, l1_b_dn, l1_ln2_g, l1_ln2_b)))
    l2 = dict(zip(("w_in", "b_forget") + names,
                  (l2_w_in, l2_b_forget, l2_w_out, l2_ln1_g, l2_ln1_b, l2_router_w, l2_router_b,
                   l2_w_gu, l2_b_gu, l2_w_dn, l2_b_dn, l2_ln2_g, l2_ln2_b)))
    l3 = dict(zip(("w_in",) + names, (l3_w_in, l3_w_out, l3_ln1_g, l3_ln1_b, l3_router_w, l3_router_b,
                                      l3_w_gu, l3_b_gu, l3_w_dn, l3_b_dn, l3_ln2_g, l3_ln2_b)))
    return _trunk(x, (l0, l1, l2, l3))
```
